```python
import math
import jax
import jax.numpy as jnp
from jax import lax
import numpy as np

D_MODEL = 1024
BATCH = 1
SEQ = 16384
DEPTH = 4

N_A_LAYERS = DEPTH // 2
N_B_LAYERS = DEPTH - N_A_LAYERS

HEAD_DIM = 64
Q_BLOCK = 128
LN_EPS = 1e-5
NEG_INF = -1e30
TINY = 1e-30
F32 = jnp.float32

A_WINDOWS = (128, 512, 2048)
A_DILATIONS = (1, 4, 16)
A_N_GROUPS = len(A_WINDOWS)
A_HEADS = D_MODEL // HEAD_DIM
A_WIDTH = A_HEADS * HEAD_DIM
A_IN_COLS = A_N_GROUPS * 3 * A_WIDTH

B_HEADS = D_MODEL // HEAD_DIM
B_KV_GROUPS = 4
B_HPG = B_HEADS // B_KV_GROUPS
B_WIDTH = B_HEADS * HEAD_DIM
B_N_BRANCH = 3
B_Q_COLS = B_WIDTH + B_N_BRANCH * B_HEADS
B_KV_COLS = 2 * B_N_BRANCH * B_KV_GROUPS * HEAD_DIM
CMP_LEN = 32
CMP_STRIDE = 16
CMP_HIDDEN = 256
SLC_BLOCK = 64
N_SELECT = 16
WIN = 512
FORCE_SCORE = 1e4

N_EXPERT_GROUPS = 4
EXPERTS_PER_GROUP = 4
N_EXPERTS = N_EXPERT_GROUPS * EXPERTS_PER_GROUP
TOP_K_EXPERTS = 2
D_EXPERT = 256
MOE_CHUNK = 2048

DEEPNORM_ALPHA = (2.0 * DEPTH) ** 0.25
DEEPNORM_BETA = (8.0 * DEPTH) ** -0.25

kernel_name = 'yoco_dilated_nsa_hmoe_deepnorm'


def _alibi_slopes(n):
    return jnp.asarray(2.0 ** (-8.0 * np.arange(1, n + 1) / n), dtype=F32)


def layer_norm(x, g, b):
    xf = x.astype(F32)
    mu = jnp.mean(xf, axis=-1, keepdims=True)
    var = jnp.mean(jnp.square(xf - mu), axis=-1, keepdims=True)
    return ((xf - mu) * lax.rsqrt(var + LN_EPS) * g + b).astype(x.dtype)


def _masked_softmax(s, mask):
    s = jnp.where(mask, s, NEG_INF)
    m = jnp.max(s, axis=-1, keepdims=True)
    p = jnp.where(mask, jnp.exp(s - m), 0.0)
    return p / jnp.maximum(jnp.sum(p, axis=-1, keepdims=True), TINY)


def dilated_attention(h, w_in, w_out):
    bsz, seq, _ = h.shape
    qkv = jnp.einsum('bsd,dc->bsc', h, w_in).reshape(bsz, seq, A_N_GROUPS, 3, A_HEADS, HEAD_DIM)
    q_groups = [qkv[:, :, g, 0] for g in range(A_N_GROUPS)]
    k_groups = [qkv[:, :, g, 1] for g in range(A_N_GROUPS)]
    v_groups = [qkv[:, :, g, 2] for g in range(A_N_GROUPS)]
    slopes = _alibi_slopes(A_HEADS)[:, None, None]
    scale = HEAD_DIM ** -0.5

    def block(i):
        t0 = i * Q_BLOCK
        t = t0 + jnp.arange(Q_BLOCK)
        outs, lses = [], []
        for g in range(A_N_GROUPS):
            dil = A_DILATIONS[g]
            dist = jnp.arange(A_WINDOWS[g] // dil + 1) * dil
            idx = t[:, None] - dist[None, :]
            valid = idx >= 0
            idx = jnp.maximum(idx, 0)
            qb = lax.dynamic_slice_in_dim(q_groups[g], t0, Q_BLOCK, axis=1)
            kb = jnp.take(k_groups[g], idx, axis=1)
            vb = jnp.take(v_groups[g], idx, axis=1)
            s = jnp.einsum('bqhd,bqkhd->bhqk', qb, kb).astype(F32) * scale - slopes * dist.astype(F32)
            s = jnp.where(valid, s, NEG_INF)
            m = jnp.max(s, axis=-1, keepdims=True)
            p = jnp.exp(s - m)
            l = jnp.sum(p, axis=-1, keepdims=True)
            outs.append(jnp.einsum('bhqk,bqkhd->bqhd', p / l, vb))
            lses.append(jnp.transpose((m + jnp.log(l))[..., 0], (0, 2, 1)))
        w = jax.nn.softmax(jnp.stack(lses, axis=0), axis=0)
        o = w[0][..., None] * outs[0]
        for g in range(1, A_N_GROUPS):
            o = o + w[g][..., None] * outs[g]
        return o

    o = lax.map(block, jnp.arange(seq // Q_BLOCK))
    o = jnp.transpose(o, (1, 0, 2, 3, 4)).reshape(bsz, seq, A_WIDTH)
    return jnp.einsum('bsc,cd->bsd', o, w_out).astype(h.dtype)


def nsa_shared_kv(h, w_kv, cmp_pos, cmp_w1, cmp_b1, cmp_w2, cmp_b2):
    bsz, seq, _ = h.shape
    kv = jnp.einsum('bsd,dc->bsc', h, w_kv).reshape(bsz, seq, 2 * B_N_BRANCH, B_KV_GROUPS, HEAD_DIM)
    r = CMP_LEN // CMP_STRIDE
    n_chunks = seq // CMP_STRIDE
    n_cmp = n_chunks - r + 1

    def compress(u, j):
        c = u.reshape(bsz, n_chunks, CMP_STRIDE, B_KV_GROUPS, HEAD_DIM)
        blocks = jnp.concatenate([c[:, o:o + n_cmp] for o in range(r)], axis=2)
        blocks = blocks + cmp_pos[j][:, None, :]
        flat = jnp.transpose(blocks, (0, 1, 3, 2, 4)).reshape(bsz, n_cmp, B_KV_GROUPS, CMP_LEN * HEAD_DIM)
        hid = jax.nn.gelu(flat @ cmp_w1[j] + cmp_b1[j])
        return hid @ cmp_w2[j] + cmp_b2[j]

    k_cmp = compress(kv[:, :, 0], 0)
    v_cmp = compress(kv[:, :, 1], 1)
    return (k_cmp, v_cmp, kv[:, :, 2], kv[:, :, 3], kv[:, :, 4], kv[:, :, 5])


def nsa_attention(h, w_q, b_q, w_out, k_cmp, v_cmp, k_slc, v_slc, k_win, v_win):
    bsz, seq, _ = h.shape
    proj = jnp.einsum('bsd,dc->bsc', h, w_q) + b_q
    q = proj[..., :B_WIDTH].reshape(bsz, seq, B_KV_GROUPS, B_HPG, HEAD_DIM)
    gates = jax.nn.sigmoid(proj[..., B_WIDTH:].astype(F32)).reshape(bsz, seq, B_N_BRANCH, B_KV_GROUPS, B_HPG)
    slopes = _alibi_slopes(B_HEADS).reshape(B_KV_GROUPS, B_HPG, 1, 1)
    scale = HEAD_DIM ** -0.5
    n_cmp = k_cmp.shape[1]
    cmp_end = jnp.arange(n_cmp) * CMP_STRIDE + (CMP_LEN - 1)
    n_slc = seq // SLC_BLOCK
    n_sel = min(N_SELECT, n_slc)
    n_keys_sel = n_sel * SLC_BLOCK
    r_s = SLC_BLOCK // CMP_STRIDE
    r_c = CMP_LEN // CMP_STRIDE
    slc_ids = jnp.arange(n_slc)
    map_idx = (r_s * slc_ids[:, None, None] + jnp.arange(r_s)[None, :, None]
               - jnp.arange(r_c)[None, None, :]).reshape(n_slc, r_s * r_c)
    map_valid = ((map_idx >= 0) & (map_idx < n_cmp)).astype(F32)
    map_idx = jnp.clip(map_idx, 0, n_cmp - 1)
    b_ids = jnp.arange(bsz)[:, None, None]
    g_ids = jnp.arange(B_KV_GROUPS)[None, :, None]
    k_win_p = jnp.pad(k_win, ((0, 0), (WIN, 0), (0, 0), (0, 0)))
    v_win_p = jnp.pad(v_win, ((0, 0), (WIN, 0), (0, 0), (0, 0)))

    def block(i):
        t0 = i * Q_BLOCK
        t = t0 + jnp.arange(Q_BLOCK)
        qb = lax.dynamic_slice_in_dim(q, t0, Q_BLOCK, axis=1)
        d_c = t[:, None] - cmp_end[None, :]
        s_c = jnp.einsum('bqgrd,bcgd->bgrqc', qb, k_cmp).astype(F32) * scale - slopes * d_c.astype(F32)
        p_c = _masked_softmax(s_c, d_c >= 0)
        o_c = jnp.einsum('bgrqc,bcgd->bqgrd', p_c, v_cmp)
        imp = jnp.sum(p_c, axis=2)
        p_slc = jnp.sum(jnp.take(imp, map_idx, axis=-1) * map_valid, axis=-1)
        cur = (t // SLC_BLOCK)[:, None]
        forced = (slc_ids[None, :] == 0) | (slc_ids[None, :] == cur) | (slc_ids[None, :] == cur - 1)
        score = jnp.where(forced, FORCE_SCORE, p_slc)
        score = jnp.where(slc_ids[None, :] <= cur, score, -FORCE_SCORE)
        _, sel = lax.top_k(score, n_sel)
        tok = (sel[..., None] * SLC_BLOCK + jnp.arange(SLC_BLOCK)).reshape(bsz, B_KV_GROUPS, Q_BLOCK * n_keys_sel)
        k_s = k_slc[b_ids, tok, g_ids].reshape(bsz, B_KV_GROUPS, Q_BLOCK, n_keys_sel, HEAD_DIM)
        v_s = v_slc[b_ids, tok, g_ids].reshape(bsz, B_KV_GROUPS, Q_BLOCK, n_keys_sel, HEAD_DIM)
        d_s = t[None, None, :, None] - tok.reshape(bsz, B_KV_GROUPS, Q_BLOCK, n_keys_sel)
        s_s = jnp.einsum('bqgrd,bgqkd->bgrqk', qb, k_s).astype(F32) * scale - slopes * d_s[:, :, None].astype(F32)
        p_s = _masked_softmax(s_s, (d_s >= 0)[:, :, None])
        o_s = jnp.einsum('bgrqk,bgqkd->bqgrd', p_s, v_s)
        k_w = lax.dynamic_slice_in_dim(k_win_p, t0, WIN + Q_BLOCK, axis=1)
        v_w = lax.dynamic_slice_in_dim(v_win_p, t0, WIN + Q_BLOCK, axis=1)
        pos_w = t0 - WIN + jnp.arange(WIN + Q_BLOCK)
        d_w = t[:, None] - pos_w[None, :]
        mask_w = (d_w >= 0) & (d_w < WIN) & (pos_w[None, :] >= 0)
        s_w = jnp.einsum('bqgrd,bkgd->bgrqk', qb, k_w).astype(F32) * scale - slopes * d_w.astype(F32)
        p_w = _masked_softmax(s_w, mask_w)
        o_w = jnp.einsum('bgrqk,bkgd->bqgrd', p_w, v_w)
        gb = lax.dynamic_slice_in_dim(gates, t0, Q_BLOCK, axis=1)[..., None]
        return gb[:, :, 0] * o_c + gb[:, :, 1] * o_s + gb[:, :, 2] * o_w

    o = lax.map(block, jnp.arange(seq // Q_BLOCK))
    o = jnp.transpose(o, (1, 0, 2, 3, 4, 5)).reshape(bsz, seq, B_WIDTH)
    return jnp.einsum('bsc,cd->bsd', o, w_out).astype(h.dtype)


def hier_moe(h, w_router, b_router, w_gate_up, w_down):
    bsz, seq, d = h.shape
    n_tok = bsz * seq
    xt = h.reshape(n_tok, d)
    logits = (xt @ w_router + b_router).astype(F32)
    g_logits = logits[:, :N_EXPERT_GROUPS]
    e_logits = logits[:, N_EXPERT_GROUPS:].reshape(n_tok, N_EXPERT_GROUPS, EXPERTS_PER_GROUP)
    p_group = jax.nn.softmax(g_logits, axis=-1)
    g_idx = jnp.argmax(g_logits, axis=-1)
    g_w = jnp.take_along_axis(p_group, g_idx[:, None], axis=1)
    e_in = jnp.take_along_axis(e_logits, g_idx[:, None, None], axis=1)[:, 0]
    e_val, e_idx = lax.top_k(e_in, TOP_K_EXPERTS)
    e_w = jax.nn.softmax(e_val, axis=-1) * g_w
    expert_id = g_idx[:, None] * EXPERTS_PER_GROUP + e_idx
    combine = jnp.sum(jax.nn.one_hot(expert_id, N_EXPERTS, dtype=F32) * e_w[..., None], axis=1)
    chunk = math.gcd(n_tok, MOE_CHUNK)

    def run(args):
        xc, cc = args
        gu = jnp.einsum('td,edf->tef', xc, w_gate_up)
        hid = jax.nn.silu(gu[..., :D_EXPERT]) * gu[..., D_EXPERT:] * cc[..., None].astype(gu.dtype)
        return jnp.einsum('tef,efd->td', hid, w_down)

    y = lax.map(run, (xt.reshape(-1, chunk, d), combine.reshape(-1, chunk, N_EXPERTS)))
    return y.reshape(bsz, seq, d).astype(h.dtype)


def setup_inputs(seed: int = 0) -> dict:
    key = jax.random.key(seed)
    ks = jax.random.split(key, 20)

    def dense(k, shape, fan_in, gain=1.0):
        return jax.random.normal(k, shape, F32) * (gain * fan_in ** -0.5)

    def small(k, shape, s):
        return s * jax.random.normal(k, shape, F32)

    x = jax.random.normal(ks[0], (BATCH, SEQ, D_MODEL), F32)
    v_scale_a = jnp.asarray([1.0, 1.0, DEEPNORM_BETA], F32)[:, None]
    a_w_in = (dense(ks[1], (N_A_LAYERS, D_MODEL, A_N_GROUPS, 3, A_WIDTH), D_MODEL) * v_scale_a).reshape(N_A_LAYERS, D_MODEL, A_IN_COLS)
    a_w_out = dense(ks[2], (N_A_LAYERS, A_WIDTH, D_MODEL), A_WIDTH, DEEPNORM_BETA)
    v_scale_b = jnp.asarray([1.0, DEEPNORM_BETA] * B_N_BRANCH, F32)[:, None]
    b_w_kv = (dense(ks[3], (D_MODEL, 2 * B_N_BRANCH, B_KV_GROUPS * HEAD_DIM), D_MODEL) * v_scale_b).reshape(D_MODEL, B_KV_COLS)
    b_cmp_pos = small(ks[4], (2, CMP_LEN, HEAD_DIM), 0.1)
    b_cmp_w1 = dense(ks[5], (2, CMP_LEN * HEAD_DIM, CMP_HIDDEN), CMP_LEN * HEAD_DIM)
    b_cmp_b1 = small(ks[6], (2, CMP_HIDDEN), 0.01)
    b_cmp_w2 = dense(ks[7], (2, CMP_HIDDEN, HEAD_DIM), CMP_HIDDEN)
    b_cmp_b2 = small(ks[8], (2, HEAD_DIM), 0.01)
    b_w_q = dense(ks[9], (N_B_LAYERS, D_MODEL, B_Q_COLS), D_MODEL)
    b_b_q = small(ks[10], (N_B_LAYERS, B_Q_COLS), 0.01)
    b_w_out = dense(ks[11], (N_B_LAYERS, B_WIDTH, D_MODEL), B_WIDTH, DEEPNORM_BETA)
    moe_w_router = dense(ks[12], (DEPTH, D_MODEL, N_EXPERT_GROUPS + N_EXPERTS), D_MODEL)
    moe_b_router = small(ks[13], (DEPTH, N_EXPERT_GROUPS + N_EXPERTS), 0.01)
    moe_w_gate_up = dense(ks[14], (DEPTH, N_EXPERTS, D_MODEL, 2 * D_EXPERT), D_MODEL)
    moe_w_down = dense(ks[15], (DEPTH, N_EXPERTS, D_EXPERT, D_MODEL), D_EXPERT, DEEPNORM_BETA)
    ln_mix_g = 1.0 + small(ks[16], (DEPTH, D_MODEL), 0.05)
    ln_mix_b = small(ks[17], (DEPTH, D_MODEL), 0.02)
    ln_ffn_g = 1.0 + small(ks[18], (DEPTH, D_MODEL), 0.05)
    ln_ffn_b = small(ks[19], (DEPTH, D_MODEL), 0.02)
    return {'x': x, 'a_w_in': a_w_in, 'a_w_out': a_w_out, 'b_w_kv': b_w_kv,
            'b_cmp_pos': b_cmp_pos, 'b_cmp_w1': b_cmp_w1, 'b_cmp_b1': b_cmp_b1,
            'b_cmp_w2': b_cmp_w2, 'b_cmp_b2': b_cmp_b2, 'b_w_q': b_w_q, 'b_b_q': b_b_q,
            'b_w_out': b_w_out, 'moe_w_router': moe_w_router, 'moe_b_router': moe_b_router,
            'moe_w_gate_up': moe_w_gate_up, 'moe_w_down': moe_w_down,
            'ln_mix_g': ln_mix_g, 'ln_mix_b': ln_mix_b, 'ln_ffn_g': ln_ffn_g, 'ln_ffn_b': ln_ffn_b}


def reference(x, a_w_in, a_w_out, b_w_kv, b_cmp_pos, b_cmp_w1, b_cmp_b1, b_cmp_w2, b_cmp_b2,
              b_w_q, b_b_q, b_w_out, moe_w_router, moe_b_router, moe_w_gate_up, moe_w_down,
              ln_mix_g, ln_mix_b, ln_ffn_g, ln_ffn_b):
    h = x
    shared_kv = None
    for layer in range(DEPTH):
        if layer < N_A_LAYERS:
            mix = dilated_attention(h, a_w_in[layer], a_w_out[layer])
        else:
            if layer == N_A_LAYERS:
                shared_kv = nsa_shared_kv(h, b_w_kv, b_cmp_pos, b_cmp_w1, b_cmp_b1, b_cmp_w2, b_cmp_b2)
            i = layer - N_A_LAYERS
            mix = nsa_attention(h, b_w_q[i], b_b_q[i], b_w_out[i], *shared_kv)
        h = layer_norm(DEEPNORM_ALPHA * h + mix, ln_mix_g[layer], ln_mix_b[layer])
        ffn = hier_moe(h, moe_w_router[layer], moe_b_router[layer], moe_w_gate_up[layer], moe_w_down[layer])
        h = layer_norm(DEEPNORM_ALPHA * h + ffn, ln_ffn_g[layer], ln_ffn_b[layer])
    return h
```

```python
import functools
import math

import numpy as np
import jax
import jax.numpy as jnp
from jax import lax
from jax.experimental import pallas as pl
from jax.experimental.pallas import tpu as pltpu

F32 = jnp.float32
BF16 = jnp.bfloat16
I32 = jnp.int32

D_MODEL = 1024
DEPTH = 4
N_A_LAYERS = DEPTH // 2
HEAD_DIM = 64
N_HEADS = D_MODEL // HEAD_DIM
Q_BLOCK = 128
LN_EPS = 1e-5
NEG_INF = -1e30
TINY = 1e-30

A_WINDOWS = (128, 512, 2048)
A_DILATIONS = (1, 4, 16)
A_N_GROUPS = 3

N_EXPERT_GROUPS = 4
EXPERTS_PER_GROUP = 4
N_EXPERTS = 16
D_EXPERT = 256

DEEPNORM_ALPHA = (2.0 * DEPTH) ** 0.25

LANES = 128
MOE_TILE = 256
VMEM_LIMIT = 56 * 1024 * 1024


def _slopes(n):
    return [float(v) for v in np.asarray(2.0 ** (-8.0 * np.arange(1, n + 1) / n), dtype=np.float32)]


def _params(sem):
    return pltpu.CompilerParams(dimension_semantics=sem, vmem_limit_bytes=VMEM_LIMIT)


def _mm_kernel(x_ref, w_ref, o_ref):
    o_ref[...] = jnp.dot(x_ref[...], w_ref[...], preferred_element_type=F32).astype(o_ref.dtype)


def _matmul(x, w, out_dtype, tm, tn, name):
    m, k = x.shape
    n = w.shape[1]
    return pl.pallas_call(
        _mm_kernel,
        grid=(n // tn, m // tm),
        in_specs=[pl.BlockSpec((tm, k), lambda j, i: (i, 0)),
                  pl.BlockSpec((k, tn), lambda j, i: (0, j))],
        out_specs=pl.BlockSpec((tm, tn), lambda j, i: (i, j)),
        out_shape=jax.ShapeDtypeStruct((m, n), out_dtype),
        compiler_params=_params(("arbitrary", "arbitrary")),
        name=name,
    )(x, w)


def _layer_norm_store(z, g_ref, b_ref, h_ref, hb_ref):
    mu = jnp.mean(z, axis=-1, keepdims=True)
    zc = z - mu
    var = jnp.mean(zc * zc, axis=-1, keepdims=True)
    y = zc * lax.rsqrt(var + LN_EPS) * g_ref[...] + b_ref[...]
    h_ref[...] = y
    hb_ref[...] = y.astype(BF16)


def _split_bf16(x):
    hi = x.astype(BF16)
    lo = (x - hi.astype(F32)).astype(BF16)
    return hi, lo


def _dil_attn_kernel(q_ref, kp_ref, kc_ref, vp_ref, vc_ref, o_ref, lse_ref, bias_ref, *, dil, slopes):
    r = pl.program_id(0)
    jb = pl.program_id(1)
    qb = Q_BLOCK

    @pl.when((r == 0) & (jb == 0))
    def _():
        row = lax.broadcasted_iota(I32, (qb, 2 * qb), 0)
        col = lax.broadcasted_iota(I32, (qb, 2 * qb), 1)
        dist = row + qb - col
        valid = (dist >= 0) & (dist <= qb)
        dist_tok = (dist * dil).astype(F32)
        for h in range(N_HEADS):
            b = jnp.where(valid, -(slopes[h] * dist_tok), NEG_INF)
            bias_ref[1, h] = b
            bias_ref[0, h] = jnp.where(col >= qb, b, NEG_INF)

    first = jnp.minimum(jb, 1)
    lane = lax.broadcasted_iota(I32, (1, LANES), 1)
    lo_half = lane < HEAD_DIM
    lse_acc = jnp.zeros((qb, LANES), F32)
    for hp in range(N_HEADS // 2):
        sl = slice(hp * LANES, (hp + 1) * LANES)
        q2 = q_ref[:, sl] * jnp.asarray(HEAD_DIM ** -0.5, BF16)
        k2 = jnp.concatenate([kp_ref[:, sl], kc_ref[:, sl]], axis=0)
        v2 = jnp.concatenate([vp_ref[:, sl], vc_ref[:, sl]], axis=0)
        outs = []
        for half in range(2):
            h = 2 * hp + half
            keep = lo_half if half == 0 else jnp.logical_not(lo_half)
            qm = jnp.where(keep, q2, jnp.zeros_like(q2))
            s = lax.dot_general(qm, k2, (((1,), (1,)), ((), ())), preferred_element_type=F32)
            s = s + bias_ref[first, h]
            m = jnp.max(s, axis=-1, keepdims=True)
            p = jnp.exp(s - m)
            l = jnp.sum(p, axis=-1, keepdims=True)
            o = jnp.dot(p.astype(BF16), v2, preferred_element_type=F32)
            outs.append(o * (1.0 / l))
            lse_acc = jnp.where(lane == h, m + jnp.log(l), lse_acc)
        o_ref[:, sl] = jnp.where(lo_half, outs[0], outs[1]).astype(BF16)
    lse_ref[...] = lse_acc


def _dilated_group(qkv, g, seq):
    dil = A_DILATIONS[g]
    rows = seq // dil
    nblk = rows // Q_BLOCK
    ncol = 3 * A_N_GROUPS
    qkv_v = qkv.reshape(rows, dil * ncol * D_MODEL)
    cq, ck, cv = 3 * g, 3 * g + 1, 3 * g + 2
    blk = (Q_BLOCK, D_MODEL)
    kern = functools.partial(_dil_attn_kernel, dil=dil, slopes=_slopes(N_HEADS))
    o, lse = pl.pallas_call(
        kern,
        grid=(dil, nblk),
        in_specs=[
            pl.BlockSpec(blk, lambda r, j: (j, r * ncol + cq)),
            pl.BlockSpec(blk, lambda r, j: (jnp.maximum(j - 1, 0), r * ncol + ck)),
            pl.BlockSpec(blk, lambda r, j: (j, r * ncol + ck)),
            pl.BlockSpec(blk, lambda r, j: (jnp.maximum(j - 1, 0), r * ncol + cv)),
            pl.BlockSpec(blk, lambda r, j: (j, r * ncol + cv)),
        ],
        out_specs=[pl.BlockSpec(blk, lambda r, j: (j, r)),
                   pl.BlockSpec((Q_BLOCK, LANES), lambda r, j: (j, r))],
        out_shape=[jax.ShapeDtypeStruct((rows, dil * D_MODEL), BF16),
                   jax.ShapeDtypeStruct((rows, dil * LANES), F32)],
        scratch_shapes=[pltpu.VMEM((2, N_HEADS, Q_BLOCK, 2 * Q_BLOCK), F32)],
        compiler_params=_params(("arbitrary", "arbitrary")),
        name=f"dilated_attn_g{g}",
    )(qkv_v, qkv_v, qkv_v, qkv_v, qkv_v)
    return o.reshape(seq, D_MODEL), lse.reshape(seq, LANES)


def _head_expand_matrix():
    h = np.arange(LANES)[:, None]
    c = np.arange(D_MODEL)[None, :]
    return jnp.asarray((c // HEAD_DIM == h).astype(np.float32), BF16)


def _outproj_a_kernel(o0_ref, o1_ref, o2_ref, l0_ref, l1_ref, l2_ref, h_ref, w_ref, e_ref, g_ref, b_ref,
                      hn_ref, hb_ref):
    l0, l1, l2 = l0_ref[...], l1_ref[...], l2_ref[...]
    m = jnp.maximum(jnp.maximum(l0, l1), l2)
    e0, e1, e2 = jnp.exp(l0 - m), jnp.exp(l1 - m), jnp.exp(l2 - m)
    inv = 1.0 / (e0 + e1 + e2)
    o = jnp.zeros(o0_ref.shape, F32)
    for e, o_ref in ((e0, o0_ref), (e1, o1_ref), (e2, o2_ref)):
        hi, lo = _split_bf16(e * inv)
        wexp = (jnp.dot(hi, e_ref[...], preferred_element_type=F32)
                + jnp.dot(lo, e_ref[...], preferred_element_type=F32))
        o = o + wexp * o_ref[...].astype(F32)
    mix = jnp.dot(o.astype(BF16), w_ref[...], preferred_element_type=F32)
    _layer_norm_store(DEEPNORM_ALPHA * h_ref[...] + mix, g_ref, b_ref, hn_ref, hb_ref)


def _outproj_a(outs, lses, h, w_out, ln_g, ln_b, tm=256):
    seq = h.shape[0]
    row = lambda i: (i, 0)
    fix = lambda i: (0, 0)
    return pl.pallas_call(
        _outproj_a_kernel,
        grid=(seq // tm,),
        in_specs=[pl.BlockSpec((tm, D_MODEL), row)] * 3 + [pl.BlockSpec((tm, LANES), row)] * 3 + [
            pl.BlockSpec((tm, D_MODEL), row),
            pl.BlockSpec((D_MODEL, D_MODEL), fix),
            pl.BlockSpec((LANES, D_MODEL), fix),
            pl.BlockSpec((1, D_MODEL), fix),
            pl.BlockSpec((1, D_MODEL), fix)],
        out_specs=[pl.BlockSpec((tm, D_MODEL), row)] * 2,
        out_shape=[jax.ShapeDtypeStruct((seq, D_MODEL), F32), jax.ShapeDtypeStruct((seq, D_MODEL), BF16)],
        compiler_params=_params(("arbitrary",)),
        name="outproj_ln_a",
    )(*outs, *lses, h, w_out.astype(BF16), _head_expand_matrix(), ln_g.reshape(1, -1), ln_b.reshape(1, -1))


def _dilated_layer(h, hb, w_in, w_out, ln_g, ln_b):
    seq = h.shape[0]
    qkv = _matmul(hb, w_in.astype(BF16), BF16, 512, 1152, "qkv_proj")
    outs, lses = [], []
    for g in range(A_N_GROUPS):
        o, lse = _dilated_group(qkv, g, seq)
        outs.append(o)
        lses.append(lse)
    return _outproj_a(outs, lses, h, w_out, ln_g, ln_b)


ROUTER_ROWS = 24


def _router_kernel(h_ref, w_ref, b_ref, ids_ref, wts_ref, cnt_ref, base_ref):
    i = pl.program_id(0)
    tm = h_ref.shape[0]

    @pl.when(i == 0)
    def _():
        base_ref[...] = jnp.zeros_like(base_ref)

    lt = lax.dot_general(w_ref[...], h_ref[...], (((1,), (1,)), ((), ())),
                         precision=lax.Precision.HIGHEST, preferred_element_type=F32)
    lt = lt + b_ref[:, 0:1]
    gl = [lt[k:k + 1, :] for k in range(N_EXPERT_GROUPS)]
    best, gidx = gl[0], jnp.zeros((1, tm), I32)
    for k in range(1, N_EXPERT_GROUPS):
        take = gl[k] > best
        best = jnp.where(take, gl[k], best)
        gidx = jnp.where(take, k, gidx)
    denom = gl[0] * 0.0
    for k in range(N_EXPERT_GROUPS):
        denom = denom + jnp.exp(gl[k] - best)
    g_w = 1.0 / denom
    e_in = []
    for e in range(EXPERTS_PER_GROUP):
        v = lt[N_EXPERT_GROUPS + e:N_EXPERT_GROUPS + e + 1, :]
        for k in range(1, N_EXPERT_GROUPS):
            row = N_EXPERT_GROUPS + k * EXPERTS_PER_GROUP + e
            v = jnp.where(gidx == k, lt[row:row + 1, :], v)
        e_in.append(v)
    v1, i1 = e_in[0], jnp.zeros((1, tm), I32)
    for e in range(1, EXPERTS_PER_GROUP):
        take = e_in[e] > v1
        v1 = jnp.where(take, e_in[e], v1)
        i1 = jnp.where(take, e, i1)
    v2, i2 = jnp.full((1, tm), -jnp.inf, F32), jnp.zeros((1, tm), I32)
    for e in range(EXPERTS_PER_GROUP):
        take = (e_in[e] > v2) & (i1 != e)
        v2 = jnp.where(take, e_in[e], v2)
        i2 = jnp.where(take, e, i2)
    t = jnp.exp(v2 - v1)
    w1 = g_w / (1.0 + t)
    w2 = g_w * t / (1.0 + t)
    e1 = gidx * EXPERTS_PER_GROUP + i1
    e2 = gidx * EXPERTS_PER_GROUP + i2

    erow = lax.broadcasted_iota(I32, (N_EXPERTS, tm), 0)
    oh1 = (erow == e1).astype(F32)
    oh2 = (erow == e2).astype(F32)
    c = oh1 + oh2
    tri = (lax.broadcasted_iota(I32, (tm, tm), 0) < lax.broadcasted_iota(I32, (tm, tm), 1)).astype(BF16)
    excl = jnp.dot(c.astype(BF16), tri, preferred_element_type=F32) + base_ref[...]
    r1 = jnp.sum(oh1 * excl, axis=0, keepdims=True).astype(I32)
    r2 = jnp.sum(oh2 * excl, axis=0, keepdims=True).astype(I32)
    base_ref[...] = base_ref[...] + jnp.sum(c, axis=1, keepdims=True)

    zi = jnp.zeros((1, tm), I32)
    ids_ref[...] = jnp.concatenate([e1, e2, r1, r2, zi, zi, zi, zi], axis=0)
    zf = jnp.zeros((1, tm), F32)
    wts_ref[...] = jnp.concatenate([w1, w2, zf, zf, zf, zf, zf, zf], axis=0)
    cnt_ref[...] = jnp.broadcast_to(base_ref[...], cnt_ref.shape)


def _router(h, w_router, b_router, tm=512):
    seq = h.shape[0]
    n_log = N_EXPERT_GROUPS + N_EXPERTS
    w_t = jnp.zeros((ROUTER_ROWS, D_MODEL), F32).at[:n_log].set(w_router.T)
    b_t = jnp.zeros((ROUTER_ROWS, LANES), F32).at[:n_log].set(jnp.broadcast_to(b_router[:, None], (n_log, LANES)))
    return pl.pallas_call(
        _router_kernel,
        grid=(seq // tm,),
        in_specs=[pl.BlockSpec((tm, D_MODEL), lambda i: (i, 0)),
                  pl.BlockSpec((ROUTER_ROWS, D_MODEL), lambda i: (0, 0)),
                  pl.BlockSpec((ROUTER_ROWS, LANES), lambda i: (0, 0))],
        out_specs=[pl.BlockSpec((8, tm), lambda i: (0, i)),
                   pl.BlockSpec((8, tm), lambda i: (0, i)),
                   pl.BlockSpec((N_EXPERTS, LANES), lambda i: (0, 0))],
        out_shape=[jax.ShapeDtypeStruct((8, seq), I32), jax.ShapeDtypeStruct((8, seq), F32),
                   jax.ShapeDtypeStruct((N_EXPERTS, LANES), F32)],
        scratch_shapes=[pltpu.VMEM((N_EXPERTS, 1), F32)],
        compiler_params=_params(("arbitrary",)),
        name="moe_router",
    )(h, w_t, b_t)


def _dispatch_kernel(pos_ref, h_ref, xs_in_ref, xs_ref, sem):
    del xs_in_ref
    tm = h_ref.shape[0]

    def row_copy(i, p):
        return pltpu.make_async_copy(h_ref.at[pl.ds(i, 1)], xs_ref.at[pl.ds(p, 1)], sem)

    def start(i, c):
        row_copy(i, pos_ref[0, 0, i]).start()
        row_copy(i, pos_ref[0, 0, tm + i]).start()
        return c

    def wait(i, c):
        row_copy(0, 0).wait()
        row_copy(0, 0).wait()
        return c

    lax.fori_loop(0, tm, start, 0, unroll=8)
    lax.fori_loop(0, tm, wait, 0, unroll=8)


def _dispatch(h, pos, n_rows, tm=256):
    seq = h.shape[0]
    nt = seq // tm
    pos3 = pos.reshape(2, nt, tm).transpose(1, 0, 2).reshape(nt, 1, 2 * tm)
    return pl.pallas_call(
        _dispatch_kernel,
        grid=(nt,),
        in_specs=[pl.BlockSpec((1, 1, 2 * tm), lambda i: (i, 0, 0), memory_space=pltpu.SMEM),
                  pl.BlockSpec((tm, D_MODEL), lambda i: (i, 0)),
                  pl.BlockSpec(memory_space=pl.ANY)],
        out_specs=pl.BlockSpec(memory_space=pl.ANY),
        out_shape=jax.ShapeDtypeStruct((n_rows, D_MODEL), F32),
        scratch_shapes=[pltpu.SemaphoreType.DMA(())],
        input_output_aliases={2: 0},
        compiler_params=_params(("arbitrary",)),
        name="moe_dispatch",
    )(pos3, h, jnp.zeros((n_rows, D_MODEL), F32))


def _expert_kernel(te_ref, x_ref, wgu_ref, wd_ref, y_ref):
    del te_ref
    gu = jnp.dot(x_ref[...].astype(BF16), wgu_ref[0], preferred_element_type=F32)
    gate, up = gu[:, :D_EXPERT], gu[:, D_EXPERT:]
    hid = gate * (1.0 / (1.0 + jnp.exp(-gate))) * up
    y_ref[...] = jnp.dot(hid.astype(BF16), wd_ref[0], preferred_element_type=F32)


def _experts(xs, tile_expert, w_gate_up, w_down):
    n_rows = xs.shape[0]
    tm = MOE_TILE
    return pl.pallas_call(
        _expert_kernel,
        grid_spec=pltpu.PrefetchScalarGridSpec(
            num_scalar_prefetch=1,
            grid=(n_rows // tm,),
            in_specs=[pl.BlockSpec((tm, D_MODEL), lambda i, te: (i, 0)),
                      pl.BlockSpec((1, D_MODEL, 2 * D_EXPERT), lambda i, te: (te[i], 0, 0)),
                      pl.BlockSpec((1, D_EXPERT, D_MODEL), lambda i, te: (te[i], 0, 0))],
            out_specs=pl.BlockSpec((tm, D_MODEL), lambda i, te: (i, 0))),
        out_shape=jax.ShapeDtypeStruct((n_rows, D_MODEL), F32),
        compiler_params=_params(("arbitrary",)),
        name="moe_experts",
    )(tile_expert, xs, w_gate_up.astype(BF16), w_down.astype(BF16))


def _combine_kernel(pos_ref, ys_ref, h_ref, w_ref, g_ref, b_ref, hn_ref, hb_ref, buf_ref, sem):
    tm = h_ref.shape[0]

    def row_copy(k, i, p):
        return pltpu.make_async_copy(ys_ref.at[pl.ds(p, 1)], buf_ref.at[k, pl.ds(i, 1)], sem)

    def start(i, c):
        row_copy(0, i, pos_ref[0, 0, i]).start()
        row_copy(1, i, pos_ref[0, 0, tm + i]).start()
        return c

    def wait(i, c):
        row_copy(0, 0, 0).wait()
        row_copy(1, 0, 0).wait()
        return c

    lax.fori_loop(0, tm, start, 0, unroll=8)
    lax.fori_loop(0, tm, wait, 0, unroll=8)
    w = w_ref[...]
    ffn = w[:, 0:1] * buf_ref[0] + w[:, 1:2] * buf_ref[1]
    _layer_norm_store(DEEPNORM_ALPHA * h_ref[...] + ffn, g_ref, b_ref, hn_ref, hb_ref)


def _combine(ys, pos, wts, h, ln_g, ln_b, tm=256):
    seq = h.shape[0]
    nt = seq // tm
    pos3 = pos.reshape(2, nt, tm).transpose(1, 0, 2).reshape(nt, 1, 2 * tm)
    row = lambda i: (i, 0)
    fix = lambda i: (0, 0)
    return pl.pallas_call(
        _combine_kernel,
        grid=(nt,),
        in_specs=[pl.BlockSpec((1, 1, 2 * tm), lambda i: (i, 0, 0), memory_space=pltpu.SMEM),
                  pl.BlockSpec(memory_space=pl.ANY),
                  pl.BlockSpec((tm, D_MODEL), row),
                  pl.BlockSpec((tm, 2), row),
                  pl.BlockSpec((1, D_MODEL), fix),
                  pl.BlockSpec((1, D_MODEL), fix)],
        out_specs=[pl.BlockSpec((tm, D_MODEL), row)] * 2,
        out_shape=[jax.ShapeDtypeStruct((seq, D_MODEL), F32), jax.ShapeDtypeStruct((seq, D_MODEL), BF16)],
        scratch_shapes=[pltpu.VMEM((2, tm, D_MODEL), F32), pltpu.SemaphoreType.DMA(())],
        compiler_params=_params(("arbitrary",)),
        name="moe_combine_ln",
    )(pos3, ys, h, wts, ln_g.reshape(1, -1), ln_b.reshape(1, -1))


def _moe_layer(h, w_router, b_router, w_gate_up, w_down, ln_g, ln_b):
    seq = h.shape[0]
    ids, wts, cnt = _router(h, w_router, b_router)
    counts = cnt[:, 0].astype(I32)
    padded = ((counts + MOE_TILE - 1) // MOE_TILE) * MOE_TILE
    ends = jnp.cumsum(padded)
    offs = ends - padded
    pos = jnp.take(offs, ids[0:2]) + ids[2:4]
    n_rows = 2 * seq + N_EXPERTS * MOE_TILE
    tile_start = jnp.arange(n_rows // MOE_TILE, dtype=I32) * MOE_TILE
    tile_expert = jnp.minimum(jnp.sum((tile_start[:, None] >= ends[None, :]).astype(I32), axis=1), N_EXPERTS - 1)
    xs = _dispatch(h, pos, n_rows)
    ys = _experts(xs, tile_expert.astype(I32), w_gate_up, w_down)
    return _combine(ys, pos, wts[0:2].T, h, ln_g, ln_b)


B_KV_GROUPS = 4
B_HPG = N_HEADS // B_KV_GROUPS
CMP_LEN = 32
CMP_STRIDE = 16
CMP_HIDDEN = 256
SLC_BLOCK = 64
N_SELECT = 16
WIN = 512
FORCE_SCORE = 1e4
LIST_W = 264


def _group_slopes(g):
    sl = _slopes(N_HEADS)
    out = []
    for r in range(B_HPG):
        v = jnp.full((1, 1), sl[r], F32)
        for k in range(1, B_KV_GROUPS):
            v = jnp.where(g == k, sl[k * B_HPG + r], v)
        out.append(v)
    return out


def _qproj_kernel(hb_ref, w_ref, b_ref, qt_ref, gt_ref):
    pt = lax.dot_general(w_ref[...], hb_ref[...], (((1,), (1,)), ((), ())), preferred_element_type=F32)
    pt = pt + b_ref[:, 0:1]
    qt_ref[...] = (pt[:D_MODEL] * (HEAD_DIM ** -0.5)).astype(BF16)
    gt_ref[...] = 1.0 / (1.0 + jnp.exp(-pt[D_MODEL:]))


def _qproj(hb, w_q, b_q, tm=512):
    seq = hb.shape[0]
    n = w_q.shape[1]
    npad = D_MODEL + LANES
    w_t = jnp.zeros((npad, D_MODEL), BF16).at[:n].set(w_q.T.astype(BF16))
    b_t = jnp.zeros((npad, LANES), F32).at[:n].set(jnp.broadcast_to(b_q[:, None], (n, LANES)))
    return pl.pallas_call(
        _qproj_kernel,
        grid=(seq // tm,),
        in_specs=[pl.BlockSpec((tm, D_MODEL), lambda i: (i, 0)),
                  pl.BlockSpec((npad, D_MODEL), lambda i: (0, 0)),
                  pl.BlockSpec((npad, LANES), lambda i: (0, 0))],
        out_specs=[pl.BlockSpec((D_MODEL, tm), lambda i: (0, i)),
                   pl.BlockSpec((LANES, tm), lambda i: (0, i))],
        out_shape=[jax.ShapeDtypeStruct((D_MODEL, seq), BF16), jax.ShapeDtypeStruct((LANES, seq), F32)],
        compiler_params=_params(("arbitrary",)),
        name="nsa_qproj",
    )(hb, w_t, b_t)


def _compress_kernel(x_ref, w1a_ref, w1b_ref, pa_ref, pb_ref, b1_ref, w2_ref, b2_ref, o_ref):
    x = x_ref[0, 0]
    n = x.shape[0]
    a = jnp.dot((x + pa_ref[0]).astype(BF16), w1a_ref[0], preferred_element_type=F32)
    b = jnp.dot((x + pb_ref[0]).astype(BF16), w1b_ref[0], preferred_element_type=F32)
    b = pltpu.roll(b, n - 1, 0)
    u = a + b + b1_ref[0]
    hid = 0.5 * u * (1.0 + jnp.tanh(math.sqrt(2.0 / math.pi) * (u + 0.044715 * (u * u * u))))
    o_ref[0, 0] = (jnp.dot(hid.astype(BF16), w2_ref[0], preferred_element_type=F32) + b2_ref[0]).astype(BF16)


def _compress(kvc, cmp_pos, cmp_w1, cmp_b1, cmp_w2, cmp_b2):
    seq = kvc.shape[0]
    nch = seq // CMP_STRIDE
    half = CMP_STRIDE * HEAD_DIM
    chunks = kvc.reshape(nch, CMP_STRIDE, 2, B_KV_GROUPS, HEAD_DIM).transpose(2, 3, 0, 1, 4).reshape(
        2, B_KV_GROUPS, nch, half)
    pos = cmp_pos.reshape(2, 1, CMP_LEN * HEAD_DIM)
    w1 = cmp_w1.astype(BF16)
    jfix = lambda j, g: (j, 0, 0)
    return pl.pallas_call(
        _compress_kernel,
        grid=(2, B_KV_GROUPS),
        in_specs=[pl.BlockSpec((1, 1, nch, half), lambda j, g: (j, g, 0, 0)),
                  pl.BlockSpec((1, half, CMP_HIDDEN), jfix),
                  pl.BlockSpec((1, half, CMP_HIDDEN), lambda j, g: (j, 1, 0)),
                  pl.BlockSpec((1, 1, half), jfix),
                  pl.BlockSpec((1, 1, half), lambda j, g: (j, 0, 1)),
                  pl.BlockSpec((1, 1, CMP_HIDDEN), jfix),
                  pl.BlockSpec((1, CMP_HIDDEN, HEAD_DIM), jfix),
                  pl.BlockSpec((1, 1, HEAD_DIM), jfix)],
        out_specs=pl.BlockSpec((1, 1, nch, HEAD_DIM), lambda j, g: (j, g, 0, 0)),
        out_shape=jax.ShapeDtypeStruct((2, B_KV_GROUPS, nch, HEAD_DIM), BF16),
        compiler_params=_params(("arbitrary", "arbitrary")),
        name="nsa_compress",
    )(chunks, w1, w1, pos, pos, cmp_b1.reshape(2, 1, CMP_HIDDEN), cmp_w2.astype(BF16),
      cmp_b2.reshape(2, 1, HEAD_DIM))


def _nsa_shared_kv(hb, w_kv, cmp_pos, cmp_w1, cmp_b1, cmp_w2, cmp_b2):
    seq = hb.shape[0]
    gw = B_KV_GROUPS * HEAD_DIM
    w = w_kv.astype(BF16)
    kvc = _matmul(hb, w[:, :2 * gw], F32, 512, 2 * gw, "kv_proj_cmp")
    kvr = _matmul(hb, w[:, 2 * gw:], BF16, 512, 4 * gw, "kv_proj_rest")
    cmp = _compress(kvc, cmp_pos, cmp_w1, cmp_b1, cmp_w2, cmp_b2)
    n_slc, n_win = seq // SLC_BLOCK, seq // Q_BLOCK

    def blocks(cols, nblk, blk, transpose):
        t = cols.reshape(nblk, blk, B_KV_GROUPS, HEAD_DIM)
        return t.transpose(2, 0, 3, 1) if transpose else t.transpose(2, 0, 1, 3)

    return dict(
        kc=cmp[0], vct=cmp[1].transpose(0, 2, 1),
        ks=blocks(kvr[:, 0 * gw:1 * gw], n_slc, SLC_BLOCK, False),
        vst=blocks(kvr[:, 1 * gw:2 * gw], n_slc, SLC_BLOCK, True),
        kw=blocks(kvr[:, 2 * gw:3 * gw], n_win, Q_BLOCK, False),
        vwt=blocks(kvr[:, 3 * gw:4 * gw], n_win, Q_BLOCK, True))


def _select_matrix(n_slc, nch):
    m = np.zeros((n_slc, nch), np.float32)
    r_s, r_c = SLC_BLOCK // CMP_STRIDE, CMP_LEN // CMP_STRIDE
    for a in range(r_s):
        for b in range(r_c):
            idx = r_s * np.arange(n_slc) + a - b
            ok = (idx >= 0) & (idx < nch - r_c + 1)
            m[np.arange(n_slc)[ok], idx[ok]] += 1.0
    return jnp.asarray(m, BF16)


def _q_heads(qt_ref):
    return jnp.concatenate([qt_ref[r * HEAD_DIM:(r + 1) * HEAD_DIM, :] for r in range(B_HPG)], axis=1)


def _nsa_cmp_kernel(qt_ref, kc_ref, vct_ref, gt_ref, mt_ref, oc_ref, sel_ref, flag_ref):
    g = pl.program_id(0)
    i = pl.program_id(1)
    nch = kc_ref.shape[1]
    n_slc = mt_ref.shape[0]
    t0 = i * Q_BLOCK
    slopes = _group_slopes(g)
    cidx = lax.broadcasted_iota(I32, (nch, Q_BLOCK), 0)
    qidx = lax.broadcasted_iota(I32, (nch, Q_BLOCK), 1)
    dist = (t0 + qidx) - (CMP_STRIDE * cidx + (CMP_LEN - 1))
    valid = dist >= 0
    dist_f = dist.astype(F32)
    st = jnp.dot(kc_ref[0], _q_heads(qt_ref), preferred_element_type=F32)
    imp = jnp.zeros((nch, Q_BLOCK), F32)
    outs = []
    for r in range(B_HPG):
        s = st[:, r * Q_BLOCK:(r + 1) * Q_BLOCK] - slopes[r] * dist_f
        s = jnp.where(valid, s, NEG_INF)
        m = jnp.max(s, axis=0, keepdims=True)
        p = jnp.where(valid, jnp.exp(s - m), 0.0)
        l = jnp.sum(p, axis=0, keepdims=True)
        pn = p * (1.0 / jnp.maximum(l, TINY))
        imp = imp + pn
        o = jnp.dot(vct_ref[0], pn.astype(BF16), preferred_element_type=F32)
        outs.append(o * gt_ref[pl.ds(g * B_HPG + r, 1), :])
    oc_ref[...] = jnp.concatenate(outs, axis=0).T

    hi = imp.astype(BF16)
    r1 = imp - hi.astype(F32)
    mid = r1.astype(BF16)
    lo = (r1 - mid.astype(F32)).astype(BF16)
    mt = mt_ref[...]
    p_slc = (jnp.dot(mt, hi, preferred_element_type=F32) + jnp.dot(mt, mid, preferred_element_type=F32)
             + jnp.dot(mt, lo, preferred_element_type=F32))
    sid = lax.broadcasted_iota(I32, (n_slc, Q_BLOCK), 0)
    cur = (t0 + lax.broadcasted_iota(I32, (n_slc, Q_BLOCK), 1)) // SLC_BLOCK
    forced = (sid == 0) | (sid == cur) | (sid == cur - 1)
    score = jnp.where(forced, FORCE_SCORE, p_slc)
    score = jnp.where(sid <= cur, score, -FORCE_SCORE)
    sel = jnp.zeros((n_slc, Q_BLOCK), F32)
    for _ in range(min(N_SELECT, n_slc)):
        mx = jnp.max(score, axis=0, keepdims=True)
        first = jnp.min(jnp.where(score == mx, sid, n_slc), axis=0, keepdims=True)
        pick = sid == first
        sel = jnp.where(pick, 1.0, sel)
        score = jnp.where(pick, -jnp.inf, score)
    sel = jnp.where(sid <= cur, sel, 0.0)
    sel_ref[0, 0] = sel
    flag_ref[0, 0] = jnp.max(sel, axis=1, keepdims=True).astype(I32)


def _nsa_cmp(qt, gt, kv, seq):
    nb = seq // Q_BLOCK
    nch, n_slc = seq // CMP_STRIDE, seq // SLC_BLOCK
    gw = B_HPG * HEAD_DIM
    return pl.pallas_call(
        _nsa_cmp_kernel,
        grid=(B_KV_GROUPS, nb),
        in_specs=[pl.BlockSpec((gw, Q_BLOCK), lambda g, i: (g, i)),
                  pl.BlockSpec((1, nch, HEAD_DIM), lambda g, i: (g, 0, 0)),
                  pl.BlockSpec((1, HEAD_DIM, nch), lambda g, i: (g, 0, 0)),
                  pl.BlockSpec((LANES, Q_BLOCK), lambda g, i: (0, i)),
                  pl.BlockSpec((n_slc, nch), lambda g, i: (0, 0))],
        out_specs=[pl.BlockSpec((Q_BLOCK, gw), lambda g, i: (i, g)),
                   pl.BlockSpec((1, 1, n_slc, Q_BLOCK), lambda g, i: (g, i, 0, 0)),
                   pl.BlockSpec((1, 1, n_slc, 1), lambda g, i: (g, i, 0, 0))],
        out_shape=[jax.ShapeDtypeStruct((seq, D_MODEL), F32),
                   jax.ShapeDtypeStruct((B_KV_GROUPS, nb, n_slc, Q_BLOCK), F32),
                   jax.ShapeDtypeStruct((B_KV_GROUPS, nb, n_slc, 1), I32)],
        compiler_params=_params(("arbitrary", "arbitrary")),
        name="nsa_compressed_select",
    )(qt, kv["kc"], kv["vct"], gt, _select_matrix(n_slc, nch))


def _softmax_step(st, vt, dist_f, mask, slopes, m_ref, l_ref, acc_ref):
    ps, alphas = [], []
    for r in range(B_HPG):
        cols = slice(r * Q_BLOCK, (r + 1) * Q_BLOCK)
        s = jnp.where(mask, st[:, cols] - slopes[r] * dist_f, NEG_INF)
        m_old = m_ref[0:1, cols]
        m_new = jnp.maximum(m_old, jnp.max(s, axis=0, keepdims=True))
        alpha = jnp.exp(m_old - m_new)
        p = jnp.where(mask, jnp.exp(s - m_new), 0.0)
        l_ref[0:1, cols] = alpha * l_ref[0:1, cols] + jnp.sum(p, axis=0, keepdims=True)
        m_ref[0:1, cols] = m_new
        ps.append(p.astype(BF16))
        alphas.append(alpha)
    pv = jnp.dot(vt, jnp.concatenate(ps, axis=1), preferred_element_type=F32)
    acc_ref[...] = acc_ref[...] * jnp.concatenate(alphas, axis=1) + pv


def _nsa_sparse_kernel(lst_ref, qt_ref, ks_ref, vst_ref, kw_ref, vwt_ref, sel_ref, gt_ref, oc_ref, o_ref,
                       m_ref, l_ref, acc_ref):
    g = pl.program_id(0)
    i = pl.program_id(1)
    t0 = i * Q_BLOCK
    slopes = _group_slopes(g)
    q4 = _q_heads(qt_ref)
    n_br = 2

    for b in range(n_br):
        m_ref[b] = jnp.full(m_ref.shape[1:], NEG_INF, F32)
        l_ref[b] = jnp.zeros(l_ref.shape[1:], F32)
        acc_ref[b] = jnp.zeros(acc_ref.shape[1:], F32)

    rel_s = (lax.broadcasted_iota(I32, (SLC_BLOCK, Q_BLOCK), 1)
             - lax.broadcasted_iota(I32, (SLC_BLOCK, Q_BLOCK), 0))

    def sel_step(j, c):
        blk = lst_ref[0, 0, j]
        dist = rel_s + (t0 - blk * SLC_BLOCK)
        chosen = sel_ref[0, 0, pl.ds(blk, 1), :] > 0.5
        st = jnp.dot(ks_ref[0, blk], q4, preferred_element_type=F32)
        _softmax_step(st, vst_ref[0, blk], dist.astype(F32), chosen & (dist >= 0), slopes,
                      m_ref.at[0], l_ref.at[0], acc_ref.at[0])
        return c

    lax.fori_loop(0, lst_ref[0, 0, LIST_W - 1], sel_step, 0)

    rel_w = (lax.broadcasted_iota(I32, (Q_BLOCK, Q_BLOCK), 1)
             - lax.broadcasted_iota(I32, (Q_BLOCK, Q_BLOCK), 0))
    n_prev = WIN // Q_BLOCK
    for c in range(n_prev + 1):
        blk = i - n_prev + c
        dist = rel_w + (n_prev - c) * Q_BLOCK
        mask = (dist >= 0) & (dist < WIN) & (blk >= 0)
        blk_c = jnp.maximum(blk, 0)
        st = jnp.dot(kw_ref[0, blk_c], q4, preferred_element_type=F32)
        _softmax_step(st, vwt_ref[0, blk_c], dist.astype(F32), mask, slopes,
                      m_ref.at[1], l_ref.at[1], acc_ref.at[1])

    total = jnp.zeros(acc_ref.shape[1:], F32)
    for b in range(n_br):
        gate = jnp.concatenate(
            [gt_ref[pl.ds((b + 1) * N_HEADS + g * B_HPG + r, 1), :] for r in range(B_HPG)], axis=1)
        total = total + acc_ref[b] * (gate / jnp.maximum(l_ref[b, 0:1, :], TINY))
    ot = jnp.concatenate([total[:, r * Q_BLOCK:(r + 1) * Q_BLOCK] for r in range(B_HPG)], axis=0)
    o_ref[...] = (ot.T + oc_ref[...]).astype(BF16)


def _nsa_sparse(lists, qt, gt, kv, sel, oc, seq):
    nb = seq // Q_BLOCK
    n_slc, n_win = seq // SLC_BLOCK, seq // Q_BLOCK
    gw = B_HPG * HEAD_DIM
    grp = lambda g, i: (g, 0, 0, 0)
    return pl.pallas_call(
        _nsa_sparse_kernel,
        grid=(B_KV_GROUPS, nb),
        in_specs=[pl.BlockSpec((1, 1, LIST_W), lambda g, i: (g * nb + i, 0, 0), memory_space=pltpu.SMEM),
                  pl.BlockSpec((gw, Q_BLOCK), lambda g, i: (g, i)),
                  pl.BlockSpec((1, n_slc, SLC_BLOCK, HEAD_DIM), grp),
                  pl.BlockSpec((1, n_slc, HEAD_DIM, SLC_BLOCK), grp),
                  pl.BlockSpec((1, n_win, Q_BLOCK, HEAD_DIM), grp),
                  pl.BlockSpec((1, n_win, HEAD_DIM, Q_BLOCK), grp),
                  pl.BlockSpec((1, 1, n_slc, Q_BLOCK), lambda g, i: (g, i, 0, 0)),
                  pl.BlockSpec((LANES, Q_BLOCK), lambda g, i: (0, i)),
                  pl.BlockSpec((Q_BLOCK, gw), lambda g, i: (i, g))],
        out_specs=pl.BlockSpec((Q_BLOCK, gw), lambda g, i: (i, g)),
        scratch_shapes=[pltpu.VMEM((2, 8, B_HPG * Q_BLOCK), F32),
                        pltpu.VMEM((2, 8, B_HPG * Q_BLOCK), F32),
                        pltpu.VMEM((2, HEAD_DIM, B_HPG * Q_BLOCK), F32)],
        out_shape=jax.ShapeDtypeStruct((seq, D_MODEL), BF16),
        compiler_params=_params(("arbitrary", "arbitrary")),
        name="nsa_selected_window",
    )(lists, qt, kv["ks"], kv["vst"], kv["kw"], kv["vwt"], sel, gt, oc)


def _outproj_b_kernel(o_ref, h_ref, w_ref, g_ref, b_ref, hn_ref, hb_ref):
    mix = jnp.dot(o_ref[...], w_ref[...], preferred_element_type=F32)
    _layer_norm_store(DEEPNORM_ALPHA * h_ref[...] + mix, g_ref, b_ref, hn_ref, hb_ref)


def _outproj_b(o, h, w_out, ln_g, ln_b, tm=512):
    seq = h.shape[0]
    row = lambda i: (i, 0)
    fix = lambda i: (0, 0)
    return pl.pallas_call(
        _outproj_b_kernel,
        grid=(seq // tm,),
        in_specs=[pl.BlockSpec((tm, D_MODEL), row), pl.BlockSpec((tm, D_MODEL), row),
                  pl.BlockSpec((D_MODEL, D_MODEL), fix), pl.BlockSpec((1, D_MODEL), fix),
                  pl.BlockSpec((1, D_MODEL), fix)],
        out_specs=[pl.BlockSpec((tm, D_MODEL), row)] * 2,
        out_shape=[jax.ShapeDtypeStruct((seq, D_MODEL), F32), jax.ShapeDtypeStruct((seq, D_MODEL), BF16)],
        compiler_params=_params(("arbitrary",)),
        name="outproj_ln_b",
    )(o, h, w_out.astype(BF16), ln_g.reshape(1, -1), ln_b.reshape(1, -1))


def _nsa_layer(h, hb, kv, w_q, b_q, w_out, ln_g, ln_b):
    seq = h.shape[0]
    nb, n_slc = seq // Q_BLOCK, seq // SLC_BLOCK
    qt, gt = _qproj(hb, w_q, b_q)
    oc, sel, flags = _nsa_cmp(qt, gt, kv, seq)
    f = flags.reshape(B_KV_GROUPS * nb, n_slc)
    order = jnp.argsort(1 - f, axis=1, stable=True).astype(I32)
    lists = jnp.zeros((B_KV_GROUPS * nb, LIST_W), I32).at[:, :n_slc].set(order)
    lists = lists.at[:, LIST_W - 1].set(jnp.sum(f, axis=1).astype(I32))
    o = _nsa_sparse(lists.reshape(B_KV_GROUPS * nb, 1, LIST_W), qt, gt, kv, sel, oc, seq)
    return _outproj_b(o, h, w_out, ln_g, ln_b)


def kernel(x, a_w_in, a_w_out, b_w_kv, b_cmp_pos, b_cmp_w1, b_cmp_b1, b_cmp_w2, b_cmp_b2, b_w_q, b_b_q, b_w_out, moe_w_router, moe_b_router, moe_w_gate_up, moe_w_down, ln_mix_g, ln_mix_b, ln_ffn_g, ln_ffn_b):
    seq = x.shape[1]
    h = x.reshape(seq, D_MODEL)
    hb = h.astype(BF16)
    for layer in range(DEPTH):
        if layer < N_A_LAYERS:
            h, hb = _dilated_layer(h, hb, a_w_in[layer], a_w_out[layer], ln_mix_g[layer], ln_mix_b[layer])
        else:
            if layer == N_A_LAYERS:
                kv = _nsa_shared_kv(hb, b_w_kv, b_cmp_pos, b_cmp_w1, b_cmp_b1, b_cmp_w2, b_cmp_b2)
            j = layer - N_A_LAYERS
            h, hb = _nsa_layer(h, hb, kv, b_w_q[j], b_b_q[j], b_w_out[j], ln_mix_g[layer], ln_mix_b[layer])
        h, hb = _moe_layer(h, moe_w_router[layer], moe_b_router[layer], moe_w_gate_up[layer],
                           moe_w_down[layer], ln_ffn_g[layer], ln_ffn_b[layer])
    return h.reshape(1, seq, D_MODEL)
```

```python
import functools
import math

import numpy as np
import jax
import jax.numpy as jnp
from jax import lax
from jax.experimental import pallas as pl
from jax.experimental.pallas import tpu as pltpu

F32 = jnp.float32
BF16 = jnp.bfloat16
I32 = jnp.int32

D_MODEL = 1024
DEPTH = 4
N_A_LAYERS = DEPTH // 2
HEAD_DIM = 64
N_HEADS = D_MODEL // HEAD_DIM
Q_BLOCK = 128
LN_EPS = 1e-5
NEG_INF = -1e30
TINY = 1e-30

A_WINDOWS = (128, 512, 2048)
A_DILATIONS = (1, 4, 16)
A_N_GROUPS = 3

N_EXPERT_GROUPS = 4
EXPERTS_PER_GROUP = 4
N_EXPERTS = 16
D_EXPERT = 256

DEEPNORM_ALPHA = (2.0 * DEPTH) ** 0.25

LANES = 128
MOE_TILE = 256
VMEM_LIMIT = 56 * 1024 * 1024


def _slopes(n):
    return [float(v) for v in np.asarray(2.0 ** (-8.0 * np.arange(1, n + 1) / n), dtype=np.float32)]


def _params(sem):
    return pltpu.CompilerParams(dimension_semantics=sem, vmem_limit_bytes=VMEM_LIMIT)


def _mm_kernel(x_ref, w_ref, o_ref):
    o_ref[...] = jnp.dot(x_ref[...], w_ref[...], preferred_element_type=F32).astype(o_ref.dtype)


def _matmul(x, w, out_dtype, tm, tn, name):
    m, k = x.shape
    n = w.shape[1]
    return pl.pallas_call(
        _mm_kernel,
        grid=(n // tn, m // tm),
        in_specs=[pl.BlockSpec((tm, k), lambda j, i: (i, 0)),
                  pl.BlockSpec((k, tn), lambda j, i: (0, j))],
        out_specs=pl.BlockSpec((tm, tn), lambda j, i: (i, j)),
        out_shape=jax.ShapeDtypeStruct((m, n), out_dtype),
        compiler_params=_params(("arbitrary", "arbitrary")),
        name=name,
    )(x, w)


def _layer_norm_store(z, g_ref, b_ref, h_ref, hb_ref):
    mu = jnp.mean(z, axis=-1, keepdims=True)
    zc = z - mu
    var = jnp.mean(zc * zc, axis=-1, keepdims=True)
    y = zc * lax.rsqrt(var + LN_EPS) * g_ref[...] + b_ref[...]
    h_ref[...] = y
    hb_ref[...] = y.astype(BF16)


def _split_bf16(x):
    hi = x.astype(BF16)
    lo = (x - hi.astype(F32)).astype(BF16)
    return hi, lo


def _dil_attn_kernel(q_ref, kp_ref, kc_ref, vp_ref, vc_ref, o_ref, lse_ref, bias_ref, *, dil, slopes):
    r = pl.program_id(0)
    jb = pl.program_id(1)
    qb = Q_BLOCK

    @pl.when((r == 0) & (jb == 0))
    def _():
        row = lax.broadcasted_iota(I32, (qb, 2 * qb), 0)
        col = lax.broadcasted_iota(I32, (qb, 2 * qb), 1)
        dist = row + qb - col
        valid = (dist >= 0) & (dist <= qb)
        dist_tok = (dist * dil).astype(F32)
        for h in range(N_HEADS):
            b = jnp.where(valid, -(slopes[h] * dist_tok), NEG_INF)
            bias_ref[1, h] = b
            bias_ref[0, h] = jnp.where(col >= qb, b, NEG_INF)

    first = jnp.minimum(jb, 1)
    lane = lax.broadcasted_iota(I32, (1, LANES), 1)
    lo_half = lane < HEAD_DIM
    lse_acc = jnp.zeros((qb, LANES), F32)
    for hp in range(N_HEADS // 2):
        sl = slice(hp * LANES, (hp + 1) * LANES)
        q2 = q_ref[:, sl] * jnp.asarray(HEAD_DIM ** -0.5, BF16)
        k2 = jnp.concatenate([kp_ref[:, sl], kc_ref[:, sl]], axis=0)
        v2 = jnp.concatenate([vp_ref[:, sl], vc_ref[:, sl]], axis=0)
        outs = []
        for half in range(2):
            h = 2 * hp + half
            keep = lo_half if half == 0 else jnp.logical_not(lo_half)
            qm = jnp.where(keep, q2, jnp.zeros_like(q2))
            s = lax.dot_general(qm, k2, (((1,), (1,)), ((), ())), preferred_element_type=F32)
            s = s + bias_ref[first, h]
            m = jnp.max(s, axis=-1, keepdims=True)
            p = jnp.exp(s - m)
            l = jnp.sum(p, axis=-1, keepdims=True)
            o = jnp.dot(p.astype(BF16), v2, preferred_element_type=F32)
            outs.append(o * (1.0 / l))
            lse_acc = jnp.where(lane == h, m + jnp.log(l), lse_acc)
        o_ref[:, sl] = jnp.where(lo_half, outs[0], outs[1]).astype(BF16)
    lse_ref[...] = lse_acc


def _qkv_kernel(x_ref, w_ref, o_ref, acc_ref, *, dil):
    acc = jnp.dot(x_ref[...], w_ref[...], preferred_element_type=F32)
    if dil == 1:
        o_ref[0] = acc.astype(BF16)
    else:
        n_chunks = acc.shape[1] // LANES
        for c in range(n_chunks):
            acc_ref[c] = acc[:, c * LANES:(c + 1) * LANES]
        n = acc.shape[0] // dil
        for r in range(dil):
            o_ref[r] = jnp.concatenate(
                [acc_ref[c, pl.ds(r, n, stride=dil), :] for c in range(n_chunks)], axis=1).astype(BF16)


def _qkv_group(hb, w_g, g, tm=512):
    seq = hb.shape[0]
    dil = A_DILATIONS[g]
    return pl.pallas_call(
        functools.partial(_qkv_kernel, dil=dil),
        grid=(3, seq // tm),
        in_specs=[pl.BlockSpec((tm, D_MODEL), lambda c, i: (i, 0)),
                  pl.BlockSpec((D_MODEL, D_MODEL), lambda c, i: (0, c))],
        out_specs=pl.BlockSpec((dil, tm // dil, D_MODEL), lambda c, i: (0, i, c)),
        out_shape=jax.ShapeDtypeStruct((dil, seq // dil, 3 * D_MODEL), BF16),
        scratch_shapes=[pltpu.VMEM((D_MODEL // LANES, tm, LANES), F32)],
        compiler_params=_params(("arbitrary", "arbitrary")),
        name=f"qkv_proj_g{g}",
    )(hb, w_g)


def _dilated_group(qkv_g, g, seq):
    dil = A_DILATIONS[g]
    rows = seq // dil
    nblk = rows // Q_BLOCK
    blk = (None, Q_BLOCK, D_MODEL)
    kern = functools.partial(_dil_attn_kernel, dil=dil, slopes=_slopes(N_HEADS))
    return pl.pallas_call(
        kern,
        grid=(dil, nblk),
        in_specs=[
            pl.BlockSpec(blk, lambda r, j: (r, j, 0)),
            pl.BlockSpec(blk, lambda r, j: (r, jnp.maximum(j - 1, 0), 1)),
            pl.BlockSpec(blk, lambda r, j: (r, j, 1)),
            pl.BlockSpec(blk, lambda r, j: (r, jnp.maximum(j - 1, 0), 2)),
            pl.BlockSpec(blk, lambda r, j: (r, j, 2)),
        ],
        out_specs=[pl.BlockSpec(blk, lambda r, j: (r, j, 0)),
                   pl.BlockSpec((None, Q_BLOCK, LANES), lambda r, j: (r, j, 0))],
        out_shape=[jax.ShapeDtypeStruct((dil, rows, D_MODEL), BF16),
                   jax.ShapeDtypeStruct((dil, rows, LANES), F32)],
        scratch_shapes=[pltpu.VMEM((2, N_HEADS, Q_BLOCK, 2 * Q_BLOCK), F32)],
        compiler_params=_params(("arbitrary", "arbitrary")),
        name=f"dilated_attn_g{g}",
    )(qkv_g, qkv_g, qkv_g, qkv_g, qkv_g)


def _head_expand_matrix():
    h = np.arange(LANES)[:, None]
    c = np.arange(D_MODEL)[None, :]
    return jnp.asarray((c // HEAD_DIM == h).astype(np.float32), BF16)


def _to_token_order(src_ref, dst_ref, dil):
    n = src_ref.shape[1]
    n_chunks = dst_ref.shape[0]
    if dil == 1:
        return src_ref[0].astype(F32)
    for r in range(dil):
        v = src_ref[r].astype(F32)
        for c in range(n_chunks):
            dst_ref[c, pl.ds(r, n, stride=dil), :] = v[:, c * LANES:(c + 1) * LANES]
    return jnp.concatenate([dst_ref[c] for c in range(n_chunks)], axis=1)


def _outproj_a_kernel(o0_ref, o1_ref, o2_ref, l0_ref, l1_ref, l2_ref, h_ref, w_ref, e_ref, g_ref, b_ref,
                      hn_ref, hb_ref, onat_ref, lnat_ref):
    l0, l1, l2 = [_to_token_order(l_ref, lnat_ref, A_DILATIONS[gi])
                  for gi, l_ref in enumerate((l0_ref, l1_ref, l2_ref))]
    m = jnp.maximum(jnp.maximum(l0, l1), l2)
    e0, e1, e2 = jnp.exp(l0 - m), jnp.exp(l1 - m), jnp.exp(l2 - m)
    inv = 1.0 / (e0 + e1 + e2)
    o = jnp.zeros(h_ref.shape, F32)
    for gi, (e, o_ref) in enumerate(((e0, o0_ref), (e1, o1_ref), (e2, o2_ref))):
        hi, lo = _split_bf16(e * inv)
        wexp = (jnp.dot(hi, e_ref[...], preferred_element_type=F32)
                + jnp.dot(lo, e_ref[...], preferred_element_type=F32))
        o = o + wexp * _to_token_order(o_ref, onat_ref, A_DILATIONS[gi])
    mix = jnp.dot(o.astype(BF16), w_ref[...], preferred_element_type=F32)
    _layer_norm_store(DEEPNORM_ALPHA * h_ref[...] + mix, g_ref, b_ref, hn_ref, hb_ref)


def _outproj_a(outs, lses, h, w_out, ln_g, ln_b, tm=512):
    seq = h.shape[0]
    row = lambda i: (i, 0)
    fix = lambda i: (0, 0)
    cls = lambda i: (0, i, 0)
    return pl.pallas_call(
        _outproj_a_kernel,
        grid=(seq // tm,),
        in_specs=[pl.BlockSpec((d, tm // d, D_MODEL), cls) for d in A_DILATIONS]
        + [pl.BlockSpec((d, tm // d, LANES), cls) for d in A_DILATIONS] + [
            pl.BlockSpec((tm, D_MODEL), row),
            pl.BlockSpec((D_MODEL, D_MODEL), fix),
            pl.BlockSpec((LANES, D_MODEL), fix),
            pl.BlockSpec((1, D_MODEL), fix),
            pl.BlockSpec((1, D_MODEL), fix)],
        out_specs=[pl.BlockSpec((tm, D_MODEL), row)] * 2,
        out_shape=[jax.ShapeDtypeStruct((seq, D_MODEL), F32), jax.ShapeDtypeStruct((seq, D_MODEL), BF16)],
        scratch_shapes=[pltpu.VMEM((D_MODEL // LANES, tm, LANES), F32), pltpu.VMEM((1, tm, LANES), F32)],
        compiler_params=_params(("arbitrary",)),
        name="outproj_ln_a",
    )(*outs, *lses, h, w_out.astype(BF16), _head_expand_matrix(), ln_g.reshape(1, -1), ln_b.reshape(1, -1))


def _dilated_layer(h, hb, w_in, w_out, ln_g, ln_b):
    seq = h.shape[0]
    w = w_in.astype(BF16)
    gw = 3 * D_MODEL
    outs, lses = [], []
    for g in range(A_N_GROUPS):
        qkv_g = _qkv_group(hb, w[:, g * gw:(g + 1) * gw], g)
        o, lse = _dilated_group(qkv_g, g, seq)
        outs.append(o)
        lses.append(lse)
    return _outproj_a(outs, lses, h, w_out, ln_g, ln_b)


ROUTER_ROWS = 24


def _router_kernel(h_ref, w_ref, b_ref, ids_ref, wts_ref, cnt_ref, base_ref):
    i = pl.program_id(0)
    tm = h_ref.shape[0]

    @pl.when(i == 0)
    def _():
        base_ref[...] = jnp.zeros_like(base_ref)

    lt = lax.dot_general(w_ref[...], h_ref[...], (((1,), (1,)), ((), ())),
                         precision=lax.Precision.HIGHEST, preferred_element_type=F32)
    lt = lt + b_ref[:, 0:1]
    gl = [lt[k:k + 1, :] for k in range(N_EXPERT_GROUPS)]
    best, gidx = gl[0], jnp.zeros((1, tm), I32)
    for k in range(1, N_EXPERT_GROUPS):
        take = gl[k] > best
        best = jnp.where(take, gl[k], best)
        gidx = jnp.where(take, k, gidx)
    denom = gl[0] * 0.0
    for k in range(N_EXPERT_GROUPS):
        denom = denom + jnp.exp(gl[k] - best)
    g_w = 1.0 / denom
    e_in = []
    for e in range(EXPERTS_PER_GROUP):
        v = lt[N_EXPERT_GROUPS + e:N_EXPERT_GROUPS + e + 1, :]
        for k in range(1, N_EXPERT_GROUPS):
            row = N_EXPERT_GROUPS + k * EXPERTS_PER_GROUP + e
            v = jnp.where(gidx == k, lt[row:row + 1, :], v)
        e_in.append(v)
    v1, i1 = e_in[0], jnp.zeros((1, tm), I32)
    for e in range(1, EXPERTS_PER_GROUP):
        take = e_in[e] > v1
        v1 = jnp.where(take, e_in[e], v1)
        i1 = jnp.where(take, e, i1)
    v2, i2 = jnp.full((1, tm), -jnp.inf, F32), jnp.zeros((1, tm), I32)
    for e in range(EXPERTS_PER_GROUP):
        take = (e_in[e] > v2) & (i1 != e)
        v2 = jnp.where(take, e_in[e], v2)
        i2 = jnp.where(take, e, i2)
    t = jnp.exp(v2 - v1)
    w1 = g_w / (1.0 + t)
    w2 = g_w * t / (1.0 + t)
    e1 = gidx * EXPERTS_PER_GROUP + i1
    e2 = gidx * EXPERTS_PER_GROUP + i2

    erow = lax.broadcasted_iota(I32, (N_EXPERTS, tm), 0)
    oh1 = (erow == e1).astype(F32)
    oh2 = (erow == e2).astype(F32)
    c = oh1 + oh2
    tri = (lax.broadcasted_iota(I32, (tm, tm), 0) < lax.broadcasted_iota(I32, (tm, tm), 1)).astype(BF16)
    excl = jnp.dot(c.astype(BF16), tri, preferred_element_type=F32) + base_ref[...]
    r1 = jnp.sum(oh1 * excl, axis=0, keepdims=True).astype(I32)
    r2 = jnp.sum(oh2 * excl, axis=0, keepdims=True).astype(I32)
    base_ref[...] = base_ref[...] + jnp.sum(c, axis=1, keepdims=True)

    zi = jnp.zeros((1, tm), I32)
    ids_ref[...] = jnp.concatenate([e1, e2, r1, r2, zi, zi, zi, zi], axis=0)
    zf = jnp.zeros((1, tm), F32)
    wts_ref[...] = jnp.concatenate([w1, w2, zf, zf, zf, zf, zf, zf], axis=0)
    cnt_ref[...] = jnp.broadcast_to(base_ref[...], cnt_ref.shape)


def _router(h, w_router, b_router, tm=512):
    seq = h.shape[0]
    n_log = N_EXPERT_GROUPS + N_EXPERTS
    w_t = jnp.zeros((ROUTER_ROWS, D_MODEL), F32).at[:n_log].set(w_router.T)
    b_t = jnp.zeros((ROUTER_ROWS, LANES), F32).at[:n_log].set(jnp.broadcast_to(b_router[:, None], (n_log, LANES)))
    return pl.pallas_call(
        _router_kernel,
        grid=(seq // tm,),
        in_specs=[pl.BlockSpec((tm, D_MODEL), lambda i: (i, 0)),
                  pl.BlockSpec((ROUTER_ROWS, D_MODEL), lambda i: (0, 0)),
                  pl.BlockSpec((ROUTER_ROWS, LANES), lambda i: (0, 0))],
        out_specs=[pl.BlockSpec((8, tm), lambda i: (0, i)),
                   pl.BlockSpec((8, tm), lambda i: (0, i)),
                   pl.BlockSpec((N_EXPERTS, LANES), lambda i: (0, 0))],
        out_shape=[jax.ShapeDtypeStruct((8, seq), I32), jax.ShapeDtypeStruct((8, seq), F32),
                   jax.ShapeDtypeStruct((N_EXPERTS, LANES), F32)],
        scratch_shapes=[pltpu.VMEM((N_EXPERTS, 1), F32)],
        compiler_params=_params(("arbitrary",)),
        name="moe_router",
    )(h, w_t, b_t)


def _dispatch_kernel(pos_ref, h_ref, xs_in_ref, xs_ref, sem):
    del xs_in_ref
    tm = h_ref.shape[0]

    def row_copy(i, p):
        return pltpu.make_async_copy(h_ref.at[pl.ds(i, 1)], xs_ref.at[pl.ds(p, 1)], sem)

    def start(i, c):
        row_copy(i, pos_ref[0, 0, i]).start(priority=0)
        row_copy(i, pos_ref[0, 0, tm + i]).start(priority=1)
        return c

    def wait(i, c):
        row_copy(0, 0).wait()
        row_copy(0, 0).wait()
        return c

    lax.fori_loop(0, tm, start, 0, unroll=8)
    lax.fori_loop(0, tm, wait, 0, unroll=8)


def _dispatch(h, pos, n_rows, tm=256):
    seq = h.shape[0]
    nt = seq // tm
    pos3 = pos.reshape(2, nt, tm).transpose(1, 0, 2).reshape(nt, 1, 2 * tm)
    return pl.pallas_call(
        _dispatch_kernel,
        grid=(nt,),
        in_specs=[pl.BlockSpec((1, 1, 2 * tm), lambda i: (i, 0, 0), memory_space=pltpu.SMEM),
                  pl.BlockSpec((tm, D_MODEL), lambda i: (i, 0)),
                  pl.BlockSpec(memory_space=pl.ANY)],
        out_specs=pl.BlockSpec(memory_space=pl.ANY),
        out_shape=jax.ShapeDtypeStruct((n_rows, D_MODEL), F32),
        scratch_shapes=[pltpu.SemaphoreType.DMA(())],
        input_output_aliases={2: 0},
        compiler_params=_params(("arbitrary",)),
        name="moe_dispatch",
    )(pos3, h, jnp.zeros((n_rows, D_MODEL), F32))


def _expert_kernel(te_ref, x_ref, wgu_ref, wd_ref, y_ref):
    del te_ref
    gu = jnp.dot(x_ref[...].astype(BF16), wgu_ref[0], preferred_element_type=F32)
    gate, up = gu[:, :D_EXPERT], gu[:, D_EXPERT:]
    hid = gate * (1.0 / (1.0 + jnp.exp(-gate))) * up
    y_ref[...] = jnp.dot(hid.astype(BF16), wd_ref[0], preferred_element_type=F32)


def _experts(xs, tile_expert, w_gate_up, w_down):
    n_rows = xs.shape[0]
    tm = MOE_TILE
    return pl.pallas_call(
        _expert_kernel,
        grid_spec=pltpu.PrefetchScalarGridSpec(
            num_scalar_prefetch=1,
            grid=(n_rows // tm,),
            in_specs=[pl.BlockSpec((tm, D_MODEL), lambda i, te: (i, 0)),
                      pl.BlockSpec((1, D_MODEL, 2 * D_EXPERT), lambda i, te: (te[i], 0, 0)),
                      pl.BlockSpec((1, D_EXPERT, D_MODEL), lambda i, te: (te[i], 0, 0))],
            out_specs=pl.BlockSpec((tm, D_MODEL), lambda i, te: (i, 0))),
        out_shape=jax.ShapeDtypeStruct((n_rows, D_MODEL), F32),
        compiler_params=_params(("arbitrary",)),
        name="moe_experts",
    )(tile_expert, xs, w_gate_up.astype(BF16), w_down.astype(BF16))


def _combine_kernel(pos_ref, ys_ref, h_ref, w_ref, g_ref, b_ref, hn_ref, hb_ref, buf_ref, sem):
    tm = h_ref.shape[0]

    def row_copy(k, i, p):
        return pltpu.make_async_copy(ys_ref.at[pl.ds(p, 1)], buf_ref.at[k, pl.ds(i, 1)], sem)

    def start(i, c):
        row_copy(0, i, pos_ref[0, 0, i]).start(priority=0)
        row_copy(1, i, pos_ref[0, 0, tm + i]).start(priority=1)
        return c

    def wait(i, c):
        row_copy(0, 0, 0).wait()
        row_copy(1, 0, 0).wait()
        return c

    lax.fori_loop(0, tm, start, 0, unroll=8)
    lax.fori_loop(0, tm, wait, 0, unroll=8)
    w = w_ref[...]
    ffn = w[:, 0:1] * buf_ref[0] + w[:, 1:2] * buf_ref[1]
    _layer_norm_store(DEEPNORM_ALPHA * h_ref[...] + ffn, g_ref, b_ref, hn_ref, hb_ref)


def _combine(ys, pos, wts, h, ln_g, ln_b, tm=256):
    seq = h.shape[0]
    nt = seq // tm
    pos3 = pos.reshape(2, nt, tm).transpose(1, 0, 2).reshape(nt, 1, 2 * tm)
    row = lambda i: (i, 0)
    fix = lambda i: (0, 0)
    return pl.pallas_call(
        _combine_kernel,
        grid=(nt,),
        in_specs=[pl.BlockSpec((1, 1, 2 * tm), lambda i: (i, 0, 0), memory_space=pltpu.SMEM),
                  pl.BlockSpec(memory_space=pl.ANY),
                  pl.BlockSpec((tm, D_MODEL), row),
                  pl.BlockSpec((tm, 2), row),
                  pl.BlockSpec((1, D_MODEL), fix),
                  pl.BlockSpec((1, D_MODEL), fix)],
        out_specs=[pl.BlockSpec((tm, D_MODEL), row)] * 2,
        out_shape=[jax.ShapeDtypeStruct((seq, D_MODEL), F32), jax.ShapeDtypeStruct((seq, D_MODEL), BF16)],
        scratch_shapes=[pltpu.VMEM((2, tm, D_MODEL), F32), pltpu.SemaphoreType.DMA(())],
        compiler_params=_params(("arbitrary",)),
        name="moe_combine_ln",
    )(pos3, ys, h, wts, ln_g.reshape(1, -1), ln_b.reshape(1, -1))


def _moe_layer(h, w_router, b_router, w_gate_up, w_down, ln_g, ln_b):
    seq = h.shape[0]
    ids, wts, cnt = _router(h, w_router, b_router)
    counts = cnt[:, 0].astype(I32)
    padded = ((counts + MOE_TILE - 1) // MOE_TILE) * MOE_TILE
    ends = jnp.cumsum(padded)
    offs = ends - padded
    eids = jnp.arange(N_EXPERTS, dtype=I32)[:, None, None]
    pos = jnp.sum(jnp.where(ids[None, 0:2] == eids, offs[:, None, None], 0), axis=0) + ids[2:4]
    n_rows = 2 * seq + N_EXPERTS * MOE_TILE
    tile_start = jnp.arange(n_rows // MOE_TILE, dtype=I32) * MOE_TILE
    tile_expert = jnp.minimum(jnp.sum((tile_start[:, None] >= ends[None, :]).astype(I32), axis=1), N_EXPERTS - 1)
    xs = _dispatch(h, pos, n_rows)
    ys = _experts(xs, tile_expert.astype(I32), w_gate_up, w_down)
    return _combine(ys, pos, wts[0:2].T, h, ln_g, ln_b)


B_KV_GROUPS = 4
B_HPG = N_HEADS // B_KV_GROUPS
CMP_LEN = 32
CMP_STRIDE = 16
CMP_HIDDEN = 256
SLC_BLOCK = 64
N_SELECT = 16
WIN = 512
FORCE_SCORE = 1e4
LIST_W = 264
CMP_PARTS = 4
SEL_CHUNK = 8


def _group_slopes(g):
    sl = _slopes(N_HEADS)
    out = []
    for r in range(B_HPG):
        v = jnp.full((1, 1), sl[r], F32)
        for k in range(1, B_KV_GROUPS):
            v = jnp.where(g == k, sl[k * B_HPG + r], v)
        out.append(v)
    return out


def _qproj_kernel(hb_ref, w_ref, b_ref, qt_ref, gt_ref):
    pt = lax.dot_general(w_ref[...], hb_ref[...], (((1,), (1,)), ((), ())), preferred_element_type=F32)
    pt = pt + b_ref[:, 0:1]
    qt_ref[...] = (pt[:D_MODEL] * (HEAD_DIM ** -0.5)).astype(BF16)
    gt_ref[...] = 1.0 / (1.0 + jnp.exp(-pt[D_MODEL:]))


def _qproj(hb, w_q, b_q, tm=512):
    seq = hb.shape[0]
    n = w_q.shape[1]
    npad = D_MODEL + LANES
    w_t = jnp.zeros((npad, D_MODEL), BF16).at[:n].set(w_q.T.astype(BF16))
    b_t = jnp.zeros((npad, LANES), F32).at[:n].set(jnp.broadcast_to(b_q[:, None], (n, LANES)))
    return pl.pallas_call(
        _qproj_kernel,
        grid=(seq // tm,),
        in_specs=[pl.BlockSpec((tm, D_MODEL), lambda i: (i, 0)),
                  pl.BlockSpec((npad, D_MODEL), lambda i: (0, 0)),
                  pl.BlockSpec((npad, LANES), lambda i: (0, 0))],
        out_specs=[pl.BlockSpec((D_MODEL, tm), lambda i: (0, i)),
                   pl.BlockSpec((LANES, tm), lambda i: (0, i))],
        out_shape=[jax.ShapeDtypeStruct((D_MODEL, seq), BF16), jax.ShapeDtypeStruct((LANES, seq), F32)],
        compiler_params=_params(("arbitrary",)),
        name="nsa_qproj",
    )(hb, w_t, b_t)


def _compress_kernel(x_ref, w1a_ref, w1b_ref, pa_ref, pb_ref, b1_ref, w2_ref, b2_ref, o_ref):
    x = x_ref[0, 0]
    n = x.shape[0]
    a = jnp.dot((x + pa_ref[0]).astype(BF16), w1a_ref[0], preferred_element_type=F32)
    b = jnp.dot((x + pb_ref[0]).astype(BF16), w1b_ref[0], preferred_element_type=F32)
    b = pltpu.roll(b, n - 1, 0)
    u = a + b + b1_ref[0]
    hid = 0.5 * u * (1.0 + jnp.tanh(math.sqrt(2.0 / math.pi) * (u + 0.044715 * (u * u * u))))
    o_ref[0, 0] = (jnp.dot(hid.astype(BF16), w2_ref[0], preferred_element_type=F32) + b2_ref[0]).astype(BF16)


def _compress(kvc, cmp_pos, cmp_w1, cmp_b1, cmp_w2, cmp_b2):
    seq = kvc.shape[0]
    nch = seq // CMP_STRIDE
    half = CMP_STRIDE * HEAD_DIM
    chunks = kvc.reshape(nch, CMP_STRIDE, 2, B_KV_GROUPS, HEAD_DIM).transpose(2, 3, 0, 1, 4).reshape(
        2, B_KV_GROUPS, nch, half)
    pos = cmp_pos.reshape(2, 1, CMP_LEN * HEAD_DIM)
    w1 = cmp_w1.astype(BF16)
    jfix = lambda j, g: (j, 0, 0)
    return pl.pallas_call(
        _compress_kernel,
        grid=(2, B_KV_GROUPS),
        in_specs=[pl.BlockSpec((1, 1, nch, half), lambda j, g: (j, g, 0, 0)),
                  pl.BlockSpec((1, half, CMP_HIDDEN), jfix),
                  pl.BlockSpec((1, half, CMP_HIDDEN), lambda j, g: (j, 1, 0)),
                  pl.BlockSpec((1, 1, half), jfix),
                  pl.BlockSpec((1, 1, half), lambda j, g: (j, 0, 1)),
                  pl.BlockSpec((1, 1, CMP_HIDDEN), jfix),
                  pl.BlockSpec((1, CMP_HIDDEN, HEAD_DIM), jfix),
                  pl.BlockSpec((1, 1, HEAD_DIM), jfix)],
        out_specs=pl.BlockSpec((1, 1, nch, HEAD_DIM), lambda j, g: (j, g, 0, 0)),
        out_shape=jax.ShapeDtypeStruct((2, B_KV_GROUPS, nch, HEAD_DIM), BF16),
        compiler_params=_params(("arbitrary", "arbitrary")),
        name="nsa_compress",
    )(chunks, w1, w1, pos, pos, cmp_b1.reshape(2, 1, CMP_HIDDEN), cmp_w2.astype(BF16),
      cmp_b2.reshape(2, 1, HEAD_DIM))


def _nsa_shared_kv(hb, w_kv, cmp_pos, cmp_w1, cmp_b1, cmp_w2, cmp_b2):
    seq = hb.shape[0]
    gw = B_KV_GROUPS * HEAD_DIM
    w = w_kv.astype(BF16)
    kvc = _matmul(hb, w[:, :2 * gw], F32, 512, 2 * gw, "kv_proj_cmp")
    kvr = _matmul(hb, w[:, 2 * gw:], BF16, 512, 4 * gw, "kv_proj_rest")
    cmp = _compress(kvc, cmp_pos, cmp_w1, cmp_b1, cmp_w2, cmp_b2)
    n_slc, n_win = seq // SLC_BLOCK, seq // Q_BLOCK

    def blocks(cols, nblk, blk, transpose):
        t = cols.reshape(nblk, blk, B_KV_GROUPS, HEAD_DIM)
        return t.transpose(2, 0, 3, 1) if transpose else t.transpose(2, 0, 1, 3)

    return dict(
        kc=cmp[0], vct=cmp[1].transpose(0, 2, 1),
        ks=blocks(kvr[:, 0 * gw:1 * gw], n_slc, SLC_BLOCK, False),
        vst=blocks(kvr[:, 1 * gw:2 * gw], n_slc, SLC_BLOCK, True),
        kw=blocks(kvr[:, 2 * gw:3 * gw], n_win, Q_BLOCK, False),
        vwt=blocks(kvr[:, 3 * gw:4 * gw], n_win, Q_BLOCK, True))


def _select_matrix(n_slc, nch):
    m = np.zeros((n_slc, nch), np.float32)
    r_s, r_c = SLC_BLOCK // CMP_STRIDE, CMP_LEN // CMP_STRIDE
    for a in range(r_s):
        for b in range(r_c):
            idx = r_s * np.arange(n_slc) + a - b
            ok = (idx >= 0) & (idx < nch - r_c + 1)
            m[np.arange(n_slc)[ok], idx[ok]] += 1.0
    return jnp.asarray(m, BF16)


def _q_heads(qt_ref):
    return jnp.concatenate([qt_ref[r * HEAD_DIM:(r + 1) * HEAD_DIM, :] for r in range(B_HPG)], axis=1)


def _nsa_cmp_kernel(qt_ref, kc_ref, vct_ref, gt_ref, mt_ref, oc_ref, sel_ref, flag_ref, pslc_ref):
    g = pl.program_id(0)
    i = pl.program_id(1)
    nch = kc_ref.shape[1]
    n_slc = mt_ref.shape[0]
    t0 = i * Q_BLOCK
    slopes = _group_slopes(g)

    def compressed_branch(n):
        cidx = lax.broadcasted_iota(I32, (n, Q_BLOCK), 0)
        qidx = lax.broadcasted_iota(I32, (n, Q_BLOCK), 1)
        dist = (t0 + qidx) - (CMP_STRIDE * cidx + (CMP_LEN - 1))
        valid = dist >= 0
        dist_f = dist.astype(F32)
        st = jnp.dot(kc_ref[0, :n, :], _q_heads(qt_ref), preferred_element_type=F32)
        imp = jnp.zeros((n, Q_BLOCK), F32)
        outs = []
        for r in range(B_HPG):
            s = st[:, r * Q_BLOCK:(r + 1) * Q_BLOCK] - slopes[r] * dist_f
            s = jnp.where(valid, s, NEG_INF)
            m = jnp.max(s, axis=0, keepdims=True)
            p = jnp.where(valid, jnp.exp(s - m), 0.0)
            l = jnp.sum(p, axis=0, keepdims=True)
            pn = p * (1.0 / jnp.maximum(l, TINY))
            imp = imp + pn
            o = jnp.dot(vct_ref[0, :, :n], pn.astype(BF16), preferred_element_type=F32)
            outs.append(o * gt_ref[pl.ds(g * B_HPG + r, 1), :])
        oc_ref[...] = jnp.concatenate(outs, axis=0).T
        hi = imp.astype(BF16)
        r1 = imp - hi.astype(F32)
        mid = r1.astype(BF16)
        lo = (r1 - mid.astype(F32)).astype(BF16)
        mt = mt_ref[:, :n]
        pslc_ref[...] = (jnp.dot(mt, hi, preferred_element_type=F32)
                         + jnp.dot(mt, mid, preferred_element_type=F32)
                         + jnp.dot(mt, lo, preferred_element_type=F32))

    part = nch // CMP_PARTS
    lax.switch(lax.div(8 * i + 6, part),
               [functools.partial(compressed_branch, part * (k + 1)) for k in range(CMP_PARTS)])

    sid = lax.broadcasted_iota(I32, (n_slc, Q_BLOCK), 0)
    cur = (t0 + lax.broadcasted_iota(I32, (n_slc, Q_BLOCK), 1)) // SLC_BLOCK
    forced = (sid == 0) | (sid == cur) | (sid == cur - 1)
    score = jnp.where(forced, FORCE_SCORE, pslc_ref[...])
    score = jnp.where(sid <= cur, score, -FORCE_SCORE)
    for _ in range(min(N_SELECT, n_slc)):
        mx = jnp.max(score, axis=0, keepdims=True)
        first = jnp.min(jnp.where(score == mx, sid, n_slc), axis=0, keepdims=True)
        score = jnp.where(sid == first, -jnp.inf, score)
    sel = jnp.where((score == -jnp.inf) & (sid <= cur), 1.0, 0.0)
    sel_ref[0, 0] = sel
    flag_ref[0, 0] = jnp.max(sel, axis=1, keepdims=True).astype(I32)


def _nsa_cmp(qt, gt, kv, seq):
    nb = seq // Q_BLOCK
    nch, n_slc = seq // CMP_STRIDE, seq // SLC_BLOCK
    gw = B_HPG * HEAD_DIM
    return pl.pallas_call(
        _nsa_cmp_kernel,
        grid=(B_KV_GROUPS, nb),
        in_specs=[pl.BlockSpec((gw, Q_BLOCK), lambda g, i: (g, i)),
                  pl.BlockSpec((1, nch, HEAD_DIM), lambda g, i: (g, 0, 0)),
                  pl.BlockSpec((1, HEAD_DIM, nch), lambda g, i: (g, 0, 0)),
                  pl.BlockSpec((LANES, Q_BLOCK), lambda g, i: (0, i)),
                  pl.BlockSpec((n_slc, nch), lambda g, i: (0, 0))],
        out_specs=[pl.BlockSpec((Q_BLOCK, gw), lambda g, i: (i, g)),
                   pl.BlockSpec((1, 1, n_slc, Q_BLOCK), lambda g, i: (g, i, 0, 0)),
                   pl.BlockSpec((1, 1, n_slc, 1), lambda g, i: (g, i, 0, 0))],
        out_shape=[jax.ShapeDtypeStruct((seq, D_MODEL), F32),
                   jax.ShapeDtypeStruct((B_KV_GROUPS, nb, n_slc, Q_BLOCK), F32),
                   jax.ShapeDtypeStruct((B_KV_GROUPS, nb, n_slc, 1), I32)],
        scratch_shapes=[pltpu.VMEM((n_slc, Q_BLOCK), F32)],
        compiler_params=_params(("arbitrary", "arbitrary")),
        name="nsa_compressed_select",
    )(qt, kv["kc"], kv["vct"], gt, _select_matrix(n_slc, nch))


def _attend_tiles(tiles, q4, slopes, state):
    sts = [jnp.dot(k, q4, preferred_element_type=F32) for k, _, _, _ in tiles]
    ps = [[] for _ in tiles]
    ms, ls, alphas = [], [], []
    for r in range(B_HPG):
        cols = slice(r * Q_BLOCK, (r + 1) * Q_BLOCK)
        ss = []
        for st, (_, _, dist_f, mask) in zip(sts, tiles):
            s = st[:, cols] - slopes[r] * dist_f
            ss.append(s if mask is None else jnp.where(mask, s, NEG_INF))
        m_new = functools.reduce(jnp.maximum, [jnp.max(s, axis=0, keepdims=True) for s in ss])
        if state is not None:
            m_old = state[0][:, cols]
            m_new = jnp.maximum(m_old, m_new)
            alphas.append(jnp.exp(m_old - m_new))
        l_new = jnp.zeros((1, Q_BLOCK), F32)
        for t, s in enumerate(ss):
            p = jnp.exp(s - m_new)
            l_new = l_new + jnp.sum(p, axis=0, keepdims=True)
            ps[t].append(p.astype(BF16))
        ms.append(m_new)
        ls.append(l_new)
    acc = functools.reduce(
        lambda a, b: a + b,
        [jnp.dot(vt, jnp.concatenate(ps[t], axis=1), preferred_element_type=F32)
         for t, (_, vt, _, _) in enumerate(tiles)])
    m, l = jnp.concatenate(ms, axis=1), jnp.concatenate(ls, axis=1)
    if state is not None:
        alpha = jnp.concatenate(alphas, axis=1)
        l = alpha * state[1] + l
        acc = alpha * state[2] + acc
    return m, l, acc


def _nsa_sparse_kernel(lst_ref, qt_ref, ks_ref, vst_ref, kw_ref, vwt_ref, sel_ref, gt_ref, oc_ref, o_ref,
                       m_ref, l_ref, acc_ref):
    g = pl.program_id(0)
    i = pl.program_id(1)
    t0 = i * Q_BLOCK
    slopes = _group_slopes(g)
    q4 = _q_heads(qt_ref)

    def gate(branch):
        return jnp.concatenate(
            [gt_ref[pl.ds(branch * N_HEADS + g * B_HPG + r, 1), :] for r in range(B_HPG)], axis=1)

    rel_w = (lax.broadcasted_iota(I32, (Q_BLOCK, Q_BLOCK), 1)
             - lax.broadcasted_iota(I32, (Q_BLOCK, Q_BLOCK), 0))
    n_prev = WIN // Q_BLOCK
    tiles = []
    for c in range(n_prev + 1):
        blk = i - n_prev + c
        dist = rel_w + (n_prev - c) * Q_BLOCK
        if c == n_prev:
            mask = dist >= 0
        elif c == 0:
            mask = (dist < WIN) & (blk >= 0)
        else:
            mask = jnp.broadcast_to(blk >= 0, dist.shape)
        blk_c = jnp.maximum(blk, 0)
        tiles.append((kw_ref[0, blk_c], vwt_ref[0, blk_c], dist.astype(F32), mask))
    _, l_w, acc_w = _attend_tiles(tiles, q4, slopes, None)
    acc_ref[1] = acc_w * (gate(2) / jnp.maximum(l_w, TINY))

    m_ref[...] = jnp.full(m_ref.shape, NEG_INF, F32)
    l_ref[...] = jnp.zeros(l_ref.shape, F32)
    acc_ref[0] = jnp.zeros(acc_ref.shape[1:], F32)
    rel_s = (lax.broadcasted_iota(I32, (SLC_BLOCK, Q_BLOCK), 1)
             - lax.broadcasted_iota(I32, (SLC_BLOCK, Q_BLOCK), 0))

    def sel_chunk(j, c):
        tiles = []
        for b in range(SEL_CHUNK):
            blk = lst_ref[0, 0, j * SEL_CHUNK + b]
            dist = rel_s + (t0 - blk * SLC_BLOCK)
            chosen = sel_ref[0, 0, pl.ds(blk, 1), :] > 0.5
            tiles.append((ks_ref[0, blk], vst_ref[0, blk], dist.astype(F32), chosen & (dist >= 0)))
        m, l, acc = _attend_tiles(tiles, q4, slopes, (m_ref[0:1, :], l_ref[0:1, :], acc_ref[0]))
        m_ref[0:1, :] = m
        l_ref[0:1, :] = l
        acc_ref[0] = acc
        return c

    lax.fori_loop(0, lst_ref[0, 0, LIST_W - 1], sel_chunk, 0)

    total = acc_ref[1] + acc_ref[0] * (gate(1) / jnp.maximum(l_ref[0:1, :], TINY))
    ot = jnp.concatenate([total[:, r * Q_BLOCK:(r + 1) * Q_BLOCK] for r in range(B_HPG)], axis=0)
    o_ref[...] = (ot.T + oc_ref[...]).astype(BF16)


def _nsa_sparse(lists, qt, gt, kv, sel, oc, seq):
    nb = seq // Q_BLOCK
    n_slc, n_win = seq // SLC_BLOCK, seq // Q_BLOCK
    gw = B_HPG * HEAD_DIM
    grp = lambda g, i: (g, 0, 0, 0)
    return pl.pallas_call(
        _nsa_sparse_kernel,
        grid=(B_KV_GROUPS, nb),
        in_specs=[pl.BlockSpec((1, 1, LIST_W), lambda g, i: (g * nb + i, 0, 0), memory_space=pltpu.SMEM),
                  pl.BlockSpec((gw, Q_BLOCK), lambda g, i: (g, i)),
                  pl.BlockSpec((1, n_slc, SLC_BLOCK, HEAD_DIM), grp),
                  pl.BlockSpec((1, n_slc, HEAD_DIM, SLC_BLOCK), grp),
                  pl.BlockSpec((1, n_win, Q_BLOCK, HEAD_DIM), grp),
                  pl.BlockSpec((1, n_win, HEAD_DIM, Q_BLOCK), grp),
                  pl.BlockSpec((1, 1, n_slc, Q_BLOCK), lambda g, i: (g, i, 0, 0)),
                  pl.BlockSpec((LANES, Q_BLOCK), lambda g, i: (0, i)),
                  pl.BlockSpec((Q_BLOCK, gw), lambda g, i: (i, g))],
        out_specs=pl.BlockSpec((Q_BLOCK, gw), lambda g, i: (i, g)),
        scratch_shapes=[pltpu.VMEM((8, B_HPG * Q_BLOCK), F32),
                        pltpu.VMEM((8, B_HPG * Q_BLOCK), F32),
                        pltpu.VMEM((2, HEAD_DIM, B_HPG * Q_BLOCK), F32)],
        out_shape=jax.ShapeDtypeStruct((seq, D_MODEL), BF16),
        compiler_params=_params(("arbitrary", "arbitrary")),
        name="nsa_selected_window",
    )(lists, qt, kv["ks"], kv["vst"], kv["kw"], kv["vwt"], sel, gt, oc)


def _outproj_b_kernel(o_ref, h_ref, w_ref, g_ref, b_ref, hn_ref, hb_ref):
    mix = jnp.dot(o_ref[...], w_ref[...], preferred_element_type=F32)
    _layer_norm_store(DEEPNORM_ALPHA * h_ref[...] + mix, g_ref, b_ref, hn_ref, hb_ref)


def _outproj_b(o, h, w_out, ln_g, ln_b, tm=512):
    seq = h.shape[0]
    row = lambda i: (i, 0)
    fix = lambda i: (0, 0)
    return pl.pallas_call(
        _outproj_b_kernel,
        grid=(seq // tm,),
        in_specs=[pl.BlockSpec((tm, D_MODEL), row), pl.BlockSpec((tm, D_MODEL), row),
                  pl.BlockSpec((D_MODEL, D_MODEL), fix), pl.BlockSpec((1, D_MODEL), fix),
                  pl.BlockSpec((1, D_MODEL), fix)],
        out_specs=[pl.BlockSpec((tm, D_MODEL), row)] * 2,
        out_shape=[jax.ShapeDtypeStruct((seq, D_MODEL), F32), jax.ShapeDtypeStruct((seq, D_MODEL), BF16)],
        compiler_params=_params(("arbitrary",)),
        name="outproj_ln_b",
    )(o, h, w_out.astype(BF16), ln_g.reshape(1, -1), ln_b.reshape(1, -1))


def _nsa_layer(h, hb, kv, w_q, b_q, w_out, ln_g, ln_b):
    seq = h.shape[0]
    nb, n_slc = seq // Q_BLOCK, seq // SLC_BLOCK
    qt, gt = _qproj(hb, w_q, b_q)
    oc, sel, flags = _nsa_cmp(qt, gt, kv, seq)
    f = flags.reshape(B_KV_GROUPS, nb, n_slc)
    newest = 2 * jnp.arange(nb, dtype=I32)[None, :, None] + 1
    age = jnp.mod(newest - jnp.arange(n_slc, dtype=I32)[None, None, :], n_slc)
    order = jnp.argsort((1 - f) * n_slc + age, axis=2).astype(I32).reshape(B_KV_GROUPS * nb, n_slc)
    n_chunks = (jnp.sum(f, axis=2).reshape(-1) + SEL_CHUNK - 1) // SEL_CHUNK
    lists = jnp.zeros((B_KV_GROUPS * nb, LIST_W), I32).at[:, :n_slc].set(order)
    lists = lists.at[:, LIST_W - 1].set(n_chunks.astype(I32))
    o = _nsa_sparse(lists.reshape(B_KV_GROUPS * nb, 1, LIST_W), qt, gt, kv, sel, oc, seq)
    return _outproj_b(o, h, w_out, ln_g, ln_b)


def kernel(x, a_w_in, a_w_out, b_w_kv, b_cmp_pos, b_cmp_w1, b_cmp_b1, b_cmp_w2, b_cmp_b2, b_w_q, b_b_q, b_w_out, moe_w_router, moe_b_router, moe_w_gate_up, moe_w_down, ln_mix_g, ln_mix_b, ln_ffn_g, ln_ffn_b):
    seq = x.shape[1]
    h = x.reshape(seq, D_MODEL)
    hb = h.astype(BF16)
    for layer in range(DEPTH):
        if layer < N_A_LAYERS:
            h, hb = _dilated_layer(h, hb, a_w_in[layer], a_w_out[layer], ln_mix_g[layer], ln_mix_b[layer])
        else:
            if layer == N_A_LAYERS:
                kv = _nsa_shared_kv(hb, b_w_kv, b_cmp_pos, b_cmp_w1, b_cmp_b1, b_cmp_w2, b_cmp_b2)
            j = layer - N_A_LAYERS
            h, hb = _nsa_layer(h, hb, kv, b_w_q[j], b_b_q[j], b_w_out[j], ln_mix_g[layer], ln_mix_b[layer])
        h, hb = _moe_layer(h, moe_w_router[layer], moe_b_router[layer], moe_w_gate_up[layer],
                           moe_w_down[layer], ln_ffn_g[layer], ln_ffn_b[layer])
    return h.reshape(1, seq, D_MODEL)
```

```python
import functools
import math

import numpy as np
import jax
import jax.numpy as jnp
from jax import lax
from jax.experimental import pallas as pl
from jax.experimental.pallas import tpu as pltpu

F32 = jnp.float32
BF16 = jnp.bfloat16
I32 = jnp.int32

D_MODEL = 1024
DEPTH = 4
N_A_LAYERS = DEPTH // 2
HEAD_DIM = 64
N_HEADS = D_MODEL // HEAD_DIM
Q_BLOCK = 128
LN_EPS = 1e-5
NEG_INF = -1e30
TINY = 1e-30

A_WINDOWS = (128, 512, 2048)
A_DILATIONS = (1, 4, 16)
A_N_GROUPS = 3

N_EXPERT_GROUPS = 4
EXPERTS_PER_GROUP = 4
N_EXPERTS = 16
D_EXPERT = 256

DEEPNORM_ALPHA = (2.0 * DEPTH) ** 0.25

LANES = 128
MOE_TILE = 256
VMEM_LIMIT = 56 * 1024 * 1024


def _slopes(n):
    return [float(v) for v in np.asarray(2.0 ** (-8.0 * np.arange(1, n + 1) / n), dtype=np.float32)]


def _params(sem):
    return pltpu.CompilerParams(dimension_semantics=sem, vmem_limit_bytes=VMEM_LIMIT)


def _mm_kernel(x_ref, w_ref, o_ref):
    o_ref[...] = jnp.dot(x_ref[...], w_ref[...], preferred_element_type=F32).astype(o_ref.dtype)


def _matmul(x, w, out_dtype, tm, tn, name):
    m, k = x.shape
    n = w.shape[1]
    return pl.pallas_call(
        _mm_kernel,
        grid=(n // tn, m // tm),
        in_specs=[pl.BlockSpec((tm, k), lambda j, i: (i, 0)),
                  pl.BlockSpec((k, tn), lambda j, i: (0, j))],
        out_specs=pl.BlockSpec((tm, tn), lambda j, i: (i, j)),
        out_shape=jax.ShapeDtypeStruct((m, n), out_dtype),
        compiler_params=_params(("arbitrary", "arbitrary")),
        name=name,
    )(x, w)


def _layer_norm_store(z, g_ref, b_ref, h_ref, hb_ref):
    mu = jnp.mean(z, axis=-1, keepdims=True)
    zc = z - mu
    var = jnp.mean(zc * zc, axis=-1, keepdims=True)
    y = zc * lax.rsqrt(var + LN_EPS) * g_ref[...] + b_ref[...]
    h_ref[...] = y
    hb_ref[...] = y.astype(BF16)


def _split_bf16(x):
    hi = x.astype(BF16)
    lo = (x - hi.astype(F32)).astype(BF16)
    return hi, lo


def _dil_attn_kernel(q_ref, kp_ref, kc_ref, vp_ref, vc_ref, o_ref, lse_ref, bias_ref, *, dil, slopes):
    r = pl.program_id(0)
    jb = pl.program_id(1)
    qb = Q_BLOCK

    @pl.when((r == 0) & (jb == 0))
    def _():
        row = lax.broadcasted_iota(I32, (qb, 2 * qb), 0)
        col = lax.broadcasted_iota(I32, (qb, 2 * qb), 1)
        dist = row + qb - col
        valid = (dist >= 0) & (dist <= qb)
        dist_tok = (dist * dil).astype(F32)
        for h in range(N_HEADS):
            b = jnp.where(valid, -(slopes[h] * dist_tok), NEG_INF)
            bias_ref[1, h] = b
            bias_ref[0, h] = jnp.where(col >= qb, b, NEG_INF)

    first = jnp.minimum(jb, 1)
    lane = lax.broadcasted_iota(I32, (1, LANES), 1)
    lo_half = lane < HEAD_DIM
    lse_acc = jnp.zeros((qb, LANES), F32)
    for hp in range(N_HEADS // 2):
        sl = slice(hp * LANES, (hp + 1) * LANES)
        q2 = q_ref[:, sl] * jnp.asarray(HEAD_DIM ** -0.5, BF16)
        k2 = jnp.concatenate([kp_ref[:, sl], kc_ref[:, sl]], axis=0)
        v2 = jnp.concatenate([vp_ref[:, sl], vc_ref[:, sl]], axis=0)
        outs = []
        for half in range(2):
            h = 2 * hp + half
            keep = lo_half if half == 0 else jnp.logical_not(lo_half)
            qm = jnp.where(keep, q2, jnp.zeros_like(q2))
            s = lax.dot_general(qm, k2, (((1,), (1,)), ((), ())), preferred_element_type=F32)
            s = s + bias_ref[first, h]
            m = jnp.max(s, axis=-1, keepdims=True)
            p = jnp.exp(s - m)
            l = jnp.sum(p, axis=-1, keepdims=True)
            o = jnp.dot(p.astype(BF16), v2, preferred_element_type=F32)
            outs.append(o * (1.0 / l))
            lse_acc = jnp.where(lane == h, m + jnp.log(l), lse_acc)
        o_ref[:, sl] = jnp.where(lo_half, outs[0], outs[1]).astype(BF16)
    lse_ref[...] = lse_acc


def _qkv_kernel(x_ref, w_ref, o_ref, acc_ref, *, dil):
    acc = jnp.dot(x_ref[...], w_ref[...], preferred_element_type=F32)
    if dil == 1:
        o_ref[0] = acc.astype(BF16)
    else:
        n_chunks = acc.shape[1] // LANES
        for c in range(n_chunks):
            acc_ref[c] = acc[:, c * LANES:(c + 1) * LANES]
        n = acc.shape[0] // dil
        for r in range(dil):
            o_ref[r] = jnp.concatenate(
                [acc_ref[c, pl.ds(r, n, stride=dil), :] for c in range(n_chunks)], axis=1).astype(BF16)


def _qkv_group(hb, w_g, g, tm=512):
    seq = hb.shape[0]
    dil = A_DILATIONS[g]
    return pl.pallas_call(
        functools.partial(_qkv_kernel, dil=dil),
        grid=(3, seq // tm),
        in_specs=[pl.BlockSpec((tm, D_MODEL), lambda c, i: (i, 0)),
                  pl.BlockSpec((D_MODEL, D_MODEL), lambda c, i: (0, c))],
        out_specs=pl.BlockSpec((dil, tm // dil, D_MODEL), lambda c, i: (0, i, c)),
        out_shape=jax.ShapeDtypeStruct((dil, seq // dil, 3 * D_MODEL), BF16),
        scratch_shapes=[pltpu.VMEM((D_MODEL // LANES, tm, LANES), F32)],
        compiler_params=_params(("arbitrary", "arbitrary")),
        name=f"qkv_proj_g{g}",
    )(hb, w_g)


def _dilated_group(qkv_g, g, seq):
    dil = A_DILATIONS[g]
    rows = seq // dil
    nblk = rows // Q_BLOCK
    blk = (None, Q_BLOCK, D_MODEL)
    kern = functools.partial(_dil_attn_kernel, dil=dil, slopes=_slopes(N_HEADS))
    return pl.pallas_call(
        kern,
        grid=(dil, nblk),
        in_specs=[
            pl.BlockSpec(blk, lambda r, j: (r, j, 0)),
            pl.BlockSpec(blk, lambda r, j: (r, jnp.maximum(j - 1, 0), 1)),
            pl.BlockSpec(blk, lambda r, j: (r, j, 1)),
            pl.BlockSpec(blk, lambda r, j: (r, jnp.maximum(j - 1, 0), 2)),
            pl.BlockSpec(blk, lambda r, j: (r, j, 2)),
        ],
        out_specs=[pl.BlockSpec(blk, lambda r, j: (r, j, 0)),
                   pl.BlockSpec((None, Q_BLOCK, LANES), lambda r, j: (r, j, 0))],
        out_shape=[jax.ShapeDtypeStruct((dil, rows, D_MODEL), BF16),
                   jax.ShapeDtypeStruct((dil, rows, LANES), F32)],
        scratch_shapes=[pltpu.VMEM((2, N_HEADS, Q_BLOCK, 2 * Q_BLOCK), F32)],
        compiler_params=_params(("arbitrary", "arbitrary")),
        name=f"dilated_attn_g{g}",
    )(qkv_g, qkv_g, qkv_g, qkv_g, qkv_g)


def _head_expand_matrix():
    h = np.arange(LANES)[:, None]
    c = np.arange(D_MODEL)[None, :]
    return jnp.asarray((c // HEAD_DIM == h).astype(np.float32), BF16)


def _to_token_order(src_ref, dst_ref, dil):
    n = src_ref.shape[1]
    n_chunks = dst_ref.shape[0]
    if dil == 1:
        return src_ref[0].astype(F32)
    for r in range(dil):
        v = src_ref[r].astype(F32)
        for c in range(n_chunks):
            dst_ref[c, pl.ds(r, n, stride=dil), :] = v[:, c * LANES:(c + 1) * LANES]
    return jnp.concatenate([dst_ref[c] for c in range(n_chunks)], axis=1)


def _outproj_a_kernel(o0_ref, o1_ref, o2_ref, l0_ref, l1_ref, l2_ref, h_ref, w_ref, e_ref, g_ref, b_ref,
                      hn_ref, hb_ref, onat_ref, lnat_ref):
    l0, l1, l2 = [_to_token_order(l_ref, lnat_ref, A_DILATIONS[gi])
                  for gi, l_ref in enumerate((l0_ref, l1_ref, l2_ref))]
    m = jnp.maximum(jnp.maximum(l0, l1), l2)
    e0, e1, e2 = jnp.exp(l0 - m), jnp.exp(l1 - m), jnp.exp(l2 - m)
    inv = 1.0 / (e0 + e1 + e2)
    o = jnp.zeros(h_ref.shape, F32)
    for gi, (e, o_ref) in enumerate(((e0, o0_ref), (e1, o1_ref), (e2, o2_ref))):
        hi, lo = _split_bf16(e * inv)
        wexp = (jnp.dot(hi, e_ref[...], preferred_element_type=F32)
                + jnp.dot(lo, e_ref[...], preferred_element_type=F32))
        o = o + wexp * _to_token_order(o_ref, onat_ref, A_DILATIONS[gi])
    mix = jnp.dot(o.astype(BF16), w_ref[...], preferred_element_type=F32)
    _layer_norm_store(DEEPNORM_ALPHA * h_ref[...] + mix, g_ref, b_ref, hn_ref, hb_ref)


def _outproj_a(outs, lses, h, w_out, ln_g, ln_b, tm=512):
    seq = h.shape[0]
    row = lambda i: (i, 0)
    fix = lambda i: (0, 0)
    cls = lambda i: (0, i, 0)
    return pl.pallas_call(
        _outproj_a_kernel,
        grid=(seq // tm,),
        in_specs=[pl.BlockSpec((d, tm // d, D_MODEL), cls) for d in A_DILATIONS]
        + [pl.BlockSpec((d, tm // d, LANES), cls) for d in A_DILATIONS] + [
            pl.BlockSpec((tm, D_MODEL), row),
            pl.BlockSpec((D_MODEL, D_MODEL), fix),
            pl.BlockSpec((LANES, D_MODEL), fix),
            pl.BlockSpec((1, D_MODEL), fix),
            pl.BlockSpec((1, D_MODEL), fix)],
        out_specs=[pl.BlockSpec((tm, D_MODEL), row)] * 2,
        out_shape=[jax.ShapeDtypeStruct((seq, D_MODEL), F32), jax.ShapeDtypeStruct((seq, D_MODEL), BF16)],
        scratch_shapes=[pltpu.VMEM((D_MODEL // LANES, tm, LANES), F32), pltpu.VMEM((1, tm, LANES), F32)],
        compiler_params=_params(("arbitrary",)),
        name="outproj_ln_a",
    )(*outs, *lses, h, w_out.astype(BF16), _head_expand_matrix(), ln_g.reshape(1, -1), ln_b.reshape(1, -1))


def _dilated_layer(h, hb, w_in, w_out, ln_g, ln_b):
    seq = h.shape[0]
    w = w_in.astype(BF16)
    gw = 3 * D_MODEL
    outs, lses = [], []
    for g in range(A_N_GROUPS):
        qkv_g = _qkv_group(hb, w[:, g * gw:(g + 1) * gw], g)
        o, lse = _dilated_group(qkv_g, g, seq)
        outs.append(o)
        lses.append(lse)
    return _outproj_a(outs, lses, h, w_out, ln_g, ln_b)


ROUTER_ROWS = 24


def _router_kernel(h_ref, w_ref, b_ref, ids_ref, wts_ref, cnt_ref, base_ref):
    i = pl.program_id(0)
    tm = h_ref.shape[0]

    @pl.when(i == 0)
    def _():
        base_ref[...] = jnp.zeros_like(base_ref)

    lt = lax.dot_general(w_ref[...], h_ref[...], (((1,), (1,)), ((), ())),
                         precision=lax.Precision.HIGHEST, preferred_element_type=F32)
    lt = lt + b_ref[:, 0:1]
    gl = [lt[k:k + 1, :] for k in range(N_EXPERT_GROUPS)]
    best, gidx = gl[0], jnp.zeros((1, tm), I32)
    for k in range(1, N_EXPERT_GROUPS):
        take = gl[k] > best
        best = jnp.where(take, gl[k], best)
        gidx = jnp.where(take, k, gidx)
    denom = gl[0] * 0.0
    for k in range(N_EXPERT_GROUPS):
        denom = denom + jnp.exp(gl[k] - best)
    g_w = 1.0 / denom
    e_in = []
    for e in range(EXPERTS_PER_GROUP):
        v = lt[N_EXPERT_GROUPS + e:N_EXPERT_GROUPS + e + 1, :]
        for k in range(1, N_EXPERT_GROUPS):
            row = N_EXPERT_GROUPS + k * EXPERTS_PER_GROUP + e
            v = jnp.where(gidx == k, lt[row:row + 1, :], v)
        e_in.append(v)
    v1, i1 = e_in[0], jnp.zeros((1, tm), I32)
    for e in range(1, EXPERTS_PER_GROUP):
        take = e_in[e] > v1
        v1 = jnp.where(take, e_in[e], v1)
        i1 = jnp.where(take, e, i1)
    v2, i2 = jnp.full((1, tm), -jnp.inf, F32), jnp.zeros((1, tm), I32)
    for e in range(EXPERTS_PER_GROUP):
        take = (e_in[e] > v2) & (i1 != e)
        v2 = jnp.where(take, e_in[e], v2)
        i2 = jnp.where(take, e, i2)
    t = jnp.exp(v2 - v1)
    w1 = g_w / (1.0 + t)
    w2 = g_w * t / (1.0 + t)
    e1 = gidx * EXPERTS_PER_GROUP + i1
    e2 = gidx * EXPERTS_PER_GROUP + i2

    erow = lax.broadcasted_iota(I32, (N_EXPERTS, tm), 0)
    oh1 = (erow == e1).astype(F32)
    oh2 = (erow == e2).astype(F32)
    c = oh1 + oh2
    tri = (lax.broadcasted_iota(I32, (tm, tm), 0) < lax.broadcasted_iota(I32, (tm, tm), 1)).astype(BF16)
    excl = jnp.dot(c.astype(BF16), tri, preferred_element_type=F32) + base_ref[...]
    r1 = jnp.sum(oh1 * excl, axis=0, keepdims=True).astype(I32)
    r2 = jnp.sum(oh2 * excl, axis=0, keepdims=True).astype(I32)
    base_ref[...] = base_ref[...] + jnp.sum(c, axis=1, keepdims=True)

    zi = jnp.zeros((1, tm), I32)
    ids_ref[...] = jnp.concatenate([e1, e2, r1, r2, zi, zi, zi, zi], axis=0)
    zf = jnp.zeros((1, tm), F32)
    wts_ref[...] = jnp.concatenate([w1, w2, zf, zf, zf, zf, zf, zf], axis=0)
    cnt_ref[...] = jnp.broadcast_to(base_ref[...], cnt_ref.shape)


def _router(h, w_router, b_router, tm=512):
    seq = h.shape[0]
    n_log = N_EXPERT_GROUPS + N_EXPERTS
    w_t = jnp.zeros((ROUTER_ROWS, D_MODEL), F32).at[:n_log].set(w_router.T)
    b_t = jnp.zeros((ROUTER_ROWS, LANES), F32).at[:n_log].set(jnp.broadcast_to(b_router[:, None], (n_log, LANES)))
    return pl.pallas_call(
        _router_kernel,
        grid=(seq // tm,),
        in_specs=[pl.BlockSpec((tm, D_MODEL), lambda i: (i, 0)),
                  pl.BlockSpec((ROUTER_ROWS, D_MODEL), lambda i: (0, 0)),
                  pl.BlockSpec((ROUTER_ROWS, LANES), lambda i: (0, 0))],
        out_specs=[pl.BlockSpec((8, tm), lambda i: (0, i)),
                   pl.BlockSpec((8, tm), lambda i: (0, i)),
                   pl.BlockSpec((N_EXPERTS, LANES), lambda i: (0, 0))],
        out_shape=[jax.ShapeDtypeStruct((8, seq), I32), jax.ShapeDtypeStruct((8, seq), F32),
                   jax.ShapeDtypeStruct((N_EXPERTS, LANES), F32)],
        scratch_shapes=[pltpu.VMEM((N_EXPERTS, 1), F32)],
        compiler_params=_params(("arbitrary",)),
        name="moe_router",
    )(h, w_t, b_t)


ROW_TILE = D_MODEL // LANES


def _rows_to_tiles(x, dst_ref):
    n = x.shape[0]
    for c in range(ROW_TILE):
        dst_ref[pl.ds(c, n, stride=ROW_TILE), :] = x[:, c * LANES:(c + 1) * LANES]


def _tiles_to_rows(src_ref):
    n = src_ref.shape[0] // ROW_TILE
    return jnp.concatenate([src_ref[pl.ds(c, n, stride=ROW_TILE), :] for c in range(ROW_TILE)], axis=1)


def _row_tile(ref, p):
    return ref.at[pl.ds(pl.multiple_of(p * ROW_TILE, ROW_TILE), ROW_TILE)]


def _dispatch_kernel(pos_ref, h_ref, xs_in_ref, xs_ref, ht_ref, sem):
    del xs_in_ref
    tm = h_ref.shape[0]
    _rows_to_tiles(h_ref[...], ht_ref)

    def row_copy(i, p):
        return pltpu.make_async_copy(_row_tile(ht_ref, i), _row_tile(xs_ref, p), sem)

    def start(i, c):
        row_copy(i, pos_ref[0, 0, i]).start()
        row_copy(i, pos_ref[0, 0, tm + i]).start()
        return c

    def wait(i, c):
        row_copy(0, 0).wait()
        row_copy(0, 0).wait()
        return c

    lax.fori_loop(0, tm, start, 0, unroll=8)
    lax.fori_loop(0, tm, wait, 0, unroll=8)


def _dispatch(h, pos, n_rows, tm=256):
    seq = h.shape[0]
    nt = seq // tm
    pos3 = pos.reshape(2, nt, tm).transpose(1, 0, 2).reshape(nt, 1, 2 * tm)
    return pl.pallas_call(
        _dispatch_kernel,
        grid=(nt,),
        in_specs=[pl.BlockSpec((1, 1, 2 * tm), lambda i: (i, 0, 0), memory_space=pltpu.SMEM),
                  pl.BlockSpec((tm, D_MODEL), lambda i: (i, 0)),
                  pl.BlockSpec(memory_space=pl.ANY)],
        out_specs=pl.BlockSpec(memory_space=pl.ANY),
        out_shape=jax.ShapeDtypeStruct((n_rows * ROW_TILE, LANES), F32),
        scratch_shapes=[pltpu.VMEM((tm * ROW_TILE, LANES), F32), pltpu.SemaphoreType.DMA(())],
        input_output_aliases={2: 0},
        compiler_params=_params(("arbitrary",)),
        name="moe_dispatch",
    )(pos3, h, jnp.zeros((n_rows * ROW_TILE, LANES), F32))


def _expert_kernel(te_ref, x_ref, wgu_ref, wd_ref, y_ref):
    del te_ref
    gu = jnp.dot(_tiles_to_rows(x_ref).astype(BF16), wgu_ref[0], preferred_element_type=F32)
    gate, up = gu[:, :D_EXPERT], gu[:, D_EXPERT:]
    hid = gate * (1.0 / (1.0 + jnp.exp(-gate))) * up
    _rows_to_tiles(jnp.dot(hid.astype(BF16), wd_ref[0], preferred_element_type=F32), y_ref)


def _experts(xs, tile_expert, w_gate_up, w_down):
    blk = MOE_TILE * ROW_TILE
    return pl.pallas_call(
        _expert_kernel,
        grid_spec=pltpu.PrefetchScalarGridSpec(
            num_scalar_prefetch=1,
            grid=(xs.shape[0] // blk,),
            in_specs=[pl.BlockSpec((blk, LANES), lambda i, te: (i, 0)),
                      pl.BlockSpec((1, D_MODEL, 2 * D_EXPERT), lambda i, te: (te[i], 0, 0)),
                      pl.BlockSpec((1, D_EXPERT, D_MODEL), lambda i, te: (te[i], 0, 0))],
            out_specs=pl.BlockSpec((blk, LANES), lambda i, te: (i, 0))),
        out_shape=jax.ShapeDtypeStruct(xs.shape, F32),
        compiler_params=_params(("arbitrary",)),
        name="moe_experts",
    )(tile_expert, xs, w_gate_up.astype(BF16), w_down.astype(BF16))


def _combine_kernel(pos_ref, ys_ref, h_ref, w_ref, g_ref, b_ref, hn_ref, hb_ref, buf_ref, sem):
    tm = h_ref.shape[0]

    def row_copy(k, i, p):
        return pltpu.make_async_copy(_row_tile(ys_ref, p), _row_tile(buf_ref.at[k], i), sem)

    def start(i, c):
        row_copy(0, i, pos_ref[0, 0, i]).start()
        row_copy(1, i, pos_ref[0, 0, tm + i]).start()
        return c

    def wait(i, c):
        row_copy(0, 0, 0).wait()
        row_copy(1, 0, 0).wait()
        return c

    lax.fori_loop(0, tm, start, 0, unroll=8)
    lax.fori_loop(0, tm, wait, 0, unroll=8)
    w = w_ref[...]
    ffn = w[:, 0:1] * _tiles_to_rows(buf_ref.at[0]) + w[:, 1:2] * _tiles_to_rows(buf_ref.at[1])
    _layer_norm_store(DEEPNORM_ALPHA * h_ref[...] + ffn, g_ref, b_ref, hn_ref, hb_ref)


def _combine(ys, pos, wts, h, ln_g, ln_b, tm=256):
    seq = h.shape[0]
    nt = seq // tm
    pos3 = pos.reshape(2, nt, tm).transpose(1, 0, 2).reshape(nt, 1, 2 * tm)
    row = lambda i: (i, 0)
    fix = lambda i: (0, 0)
    return pl.pallas_call(
        _combine_kernel,
        grid=(nt,),
        in_specs=[pl.BlockSpec((1, 1, 2 * tm), lambda i: (i, 0, 0), memory_space=pltpu.SMEM),
                  pl.BlockSpec(memory_space=pl.ANY),
                  pl.BlockSpec((tm, D_MODEL), row),
                  pl.BlockSpec((tm, 2), row),
                  pl.BlockSpec((1, D_MODEL), fix),
                  pl.BlockSpec((1, D_MODEL), fix)],
        out_specs=[pl.BlockSpec((tm, D_MODEL), row)] * 2,
        out_shape=[jax.ShapeDtypeStruct((seq, D_MODEL), F32), jax.ShapeDtypeStruct((seq, D_MODEL), BF16)],
        scratch_shapes=[pltpu.VMEM((2, tm * ROW_TILE, LANES), F32), pltpu.SemaphoreType.DMA(())],
        compiler_params=_params(("arbitrary",)),
        name="moe_combine_ln",
    )(pos3, ys, h, wts, ln_g.reshape(1, -1), ln_b.reshape(1, -1))


def _moe_layer(h, w_router, b_router, w_gate_up, w_down, ln_g, ln_b):
    seq = h.shape[0]
    ids, wts, cnt = _router(h, w_router, b_router)
    counts = cnt[:, 0].astype(I32)
    padded = ((counts + MOE_TILE - 1) // MOE_TILE) * MOE_TILE
    ends = jnp.cumsum(padded)
    offs = ends - padded
    eids = jnp.arange(N_EXPERTS, dtype=I32)[:, None, None]
    pos = jnp.sum(jnp.where(ids[None, 0:2] == eids, offs[:, None, None], 0), axis=0) + ids[2:4]
    n_rows = 2 * seq + N_EXPERTS * MOE_TILE
    tile_start = jnp.arange(n_rows // MOE_TILE, dtype=I32) * MOE_TILE
    tile_expert = jnp.minimum(jnp.sum((tile_start[:, None] >= ends[None, :]).astype(I32), axis=1), N_EXPERTS - 1)
    xs = _dispatch(h, pos, n_rows)
    ys = _experts(xs, tile_expert.astype(I32), w_gate_up, w_down)
    return _combine(ys, pos, wts[0:2].T, h, ln_g, ln_b)


B_KV_GROUPS = 4
B_HPG = N_HEADS // B_KV_GROUPS
CMP_LEN = 32
CMP_STRIDE = 16
CMP_HIDDEN = 256
SLC_BLOCK = 64
N_SELECT = 16
WIN = 512
FORCE_SCORE = 1e4
LIST_W = 272
LOG2E = math.log2(math.e)
MASK_PENALTY = 1e30
M_FLOOR = -1e29
CMP_PARTS = 4
SEL_CHUNK = 4


def _group_slopes(g):
    sl = [v * LOG2E for v in _slopes(N_HEADS)]
    out = []
    for r in range(B_HPG):
        v = jnp.full((1, 1), sl[r], F32)
        for k in range(1, B_KV_GROUPS):
            v = jnp.where(g == k, sl[k * B_HPG + r], v)
        out.append(v)
    return out


def _qproj_kernel(hb_ref, w_ref, b_ref, qt_ref, gt_ref):
    pt = lax.dot_general(w_ref[...], hb_ref[...], (((1,), (1,)), ((), ())), preferred_element_type=F32)
    pt = pt + b_ref[:, 0:1]
    qt_ref[...] = (pt[:D_MODEL] * (HEAD_DIM ** -0.5 * LOG2E)).astype(BF16)
    gt_ref[...] = 1.0 / (1.0 + jnp.exp(-pt[D_MODEL:]))


def _qproj(hb, w_q, b_q, tm=512):
    seq = hb.shape[0]
    n = w_q.shape[1]
    npad = D_MODEL + LANES
    w_t = jnp.zeros((npad, D_MODEL), BF16).at[:n].set(w_q.T.astype(BF16))
    b_t = jnp.zeros((npad, LANES), F32).at[:n].set(jnp.broadcast_to(b_q[:, None], (n, LANES)))
    return pl.pallas_call(
        _qproj_kernel,
        grid=(seq // tm,),
        in_specs=[pl.BlockSpec((tm, D_MODEL), lambda i: (i, 0)),
                  pl.BlockSpec((npad, D_MODEL), lambda i: (0, 0)),
                  pl.BlockSpec((npad, LANES), lambda i: (0, 0))],
        out_specs=[pl.BlockSpec((D_MODEL, tm), lambda i: (0, i)),
                   pl.BlockSpec((LANES, tm), lambda i: (0, i))],
        out_shape=[jax.ShapeDtypeStruct((D_MODEL, seq), BF16), jax.ShapeDtypeStruct((LANES, seq), F32)],
        compiler_params=_params(("arbitrary",)),
        name="nsa_qproj",
    )(hb, w_t, b_t)


def _compress_kernel(x_ref, w1a_ref, w1b_ref, pa_ref, pb_ref, b1_ref, w2_ref, b2_ref, o_ref):
    x = x_ref[0, 0]
    n = x.shape[0]
    a = jnp.dot((x + pa_ref[0]).astype(BF16), w1a_ref[0], preferred_element_type=F32)
    b = jnp.dot((x + pb_ref[0]).astype(BF16), w1b_ref[0], preferred_element_type=F32)
    b = pltpu.roll(b, n - 1, 0)
    u = a + b + b1_ref[0]
    hid = 0.5 * u * (1.0 + jnp.tanh(math.sqrt(2.0 / math.pi) * (u + 0.044715 * (u * u * u))))
    o_ref[0, 0] = (jnp.dot(hid.astype(BF16), w2_ref[0], preferred_element_type=F32) + b2_ref[0]).astype(BF16)


def _compress(kvc, cmp_pos, cmp_w1, cmp_b1, cmp_w2, cmp_b2):
    seq = kvc.shape[0]
    nch = seq // CMP_STRIDE
    half = CMP_STRIDE * HEAD_DIM
    chunks = kvc.reshape(nch, CMP_STRIDE, 2, B_KV_GROUPS, HEAD_DIM).transpose(2, 3, 0, 1, 4).reshape(
        2, B_KV_GROUPS, nch, half)
    pos = cmp_pos.reshape(2, 1, CMP_LEN * HEAD_DIM)
    w1 = cmp_w1.astype(BF16)
    jfix = lambda j, g: (j, 0, 0)
    return pl.pallas_call(
        _compress_kernel,
        grid=(2, B_KV_GROUPS),
        in_specs=[pl.BlockSpec((1, 1, nch, half), lambda j, g: (j, g, 0, 0)),
                  pl.BlockSpec((1, half, CMP_HIDDEN), jfix),
                  pl.BlockSpec((1, half, CMP_HIDDEN), lambda j, g: (j, 1, 0)),
                  pl.BlockSpec((1, 1, half), jfix),
                  pl.BlockSpec((1, 1, half), lambda j, g: (j, 0, 1)),
                  pl.BlockSpec((1, 1, CMP_HIDDEN), jfix),
                  pl.BlockSpec((1, CMP_HIDDEN, HEAD_DIM), jfix),
                  pl.BlockSpec((1, 1, HEAD_DIM), jfix)],
        out_specs=pl.BlockSpec((1, 1, nch, HEAD_DIM), lambda j, g: (j, g, 0, 0)),
        out_shape=jax.ShapeDtypeStruct((2, B_KV_GROUPS, nch, HEAD_DIM), BF16),
        compiler_params=_params(("arbitrary", "arbitrary")),
        name="nsa_compress",
    )(chunks, w1, w1, pos, pos, cmp_b1.reshape(2, 1, CMP_HIDDEN), cmp_w2.astype(BF16),
      cmp_b2.reshape(2, 1, HEAD_DIM))


def _nsa_shared_kv(hb, w_kv, cmp_pos, cmp_w1, cmp_b1, cmp_w2, cmp_b2):
    seq = hb.shape[0]
    gw = B_KV_GROUPS * HEAD_DIM
    w = w_kv.astype(BF16)
    kvc = _matmul(hb, w[:, :2 * gw], F32, 512, 2 * gw, "kv_proj_cmp")
    kvr = _matmul(hb, w[:, 2 * gw:], BF16, 512, 4 * gw, "kv_proj_rest")
    cmp = _compress(kvc, cmp_pos, cmp_w1, cmp_b1, cmp_w2, cmp_b2)
    n_slc, n_win = seq // SLC_BLOCK, seq // Q_BLOCK

    def blocks(cols, nblk, blk, transpose):
        t = cols.reshape(nblk, blk, B_KV_GROUPS, HEAD_DIM)
        return t.transpose(2, 0, 3, 1) if transpose else t.transpose(2, 0, 1, 3)

    return dict(
        kc=cmp[0], vct=cmp[1].transpose(0, 2, 1),
        ks=blocks(kvr[:, 0 * gw:1 * gw], n_slc, SLC_BLOCK, False),
        vst=blocks(kvr[:, 1 * gw:2 * gw], n_slc, SLC_BLOCK, False),
        kw=blocks(kvr[:, 2 * gw:3 * gw], n_win, Q_BLOCK, False),
        vwt=blocks(kvr[:, 3 * gw:4 * gw], n_win, Q_BLOCK, True))


def _select_matrix(n_slc, nch):
    m = np.zeros((n_slc, nch), np.float32)
    r_s, r_c = SLC_BLOCK // CMP_STRIDE, CMP_LEN // CMP_STRIDE
    for a in range(r_s):
        for b in range(r_c):
            idx = r_s * np.arange(n_slc) + a - b
            ok = (idx >= 0) & (idx < nch - r_c + 1)
            m[np.arange(n_slc)[ok], idx[ok]] += 1.0
    return jnp.asarray(m, BF16)


def _q_heads(qt_ref):
    return jnp.concatenate([qt_ref[r * HEAD_DIM:(r + 1) * HEAD_DIM, :] for r in range(B_HPG)], axis=1)


def _nsa_cmp_kernel(qt_ref, kc_ref, vct_ref, gt_ref, mt_ref, oc_ref, sel_ref, flag_ref, pslc_ref, bias_ref):
    g = pl.program_id(0)
    i = pl.program_id(1)
    nch = kc_ref.shape[1]
    n_slc = mt_ref.shape[0]
    t0 = i * Q_BLOCK

    @pl.when(i == 0)
    def _():
        slopes = _group_slopes(g)
        rel = (lax.broadcasted_iota(I32, (nch, Q_BLOCK), 1)
               - (CMP_STRIDE * lax.broadcasted_iota(I32, (nch, Q_BLOCK), 0) + (CMP_LEN - 1))).astype(F32)
        for r in range(B_HPG):
            bias_ref[r] = slopes[r] * rel

    def compressed_branch(n):
        cidx = lax.broadcasted_iota(I32, (n, Q_BLOCK), 0)
        qidx = lax.broadcasted_iota(I32, (n, Q_BLOCK), 1)
        valid = (t0 + qidx) - (CMP_STRIDE * cidx + (CMP_LEN - 1)) >= 0
        st = jnp.dot(kc_ref[0, :n, :], _q_heads(qt_ref), preferred_element_type=F32)
        imp = jnp.zeros((n, Q_BLOCK), F32)
        outs = []
        for r in range(B_HPG):
            s = st[:, r * Q_BLOCK:(r + 1) * Q_BLOCK] - bias_ref[r, :n, :]
            s = jnp.where(valid, s, NEG_INF)
            m = jnp.maximum(jnp.max(s, axis=0, keepdims=True), M_FLOOR)
            p = jnp.exp2(s - m)
            l = jnp.sum(p, axis=0, keepdims=True)
            pn = p * (1.0 / jnp.maximum(l, TINY))
            imp = imp + pn
            o = jnp.dot(vct_ref[0, :, :n], pn.astype(BF16), preferred_element_type=F32)
            outs.append(o * gt_ref[pl.ds(g * B_HPG + r, 1), :])
        oc_ref[...] = jnp.concatenate(outs, axis=0).T
        hi = imp.astype(BF16)
        r1 = imp - hi.astype(F32)
        mid = r1.astype(BF16)
        lo = (r1 - mid.astype(F32)).astype(BF16)
        mt = mt_ref[:, :n]
        pslc_ref[...] = (jnp.dot(mt, hi, preferred_element_type=F32)
                         + jnp.dot(mt, mid, preferred_element_type=F32)
                         + jnp.dot(mt, lo, preferred_element_type=F32))

    part = nch // CMP_PARTS
    lax.switch(lax.div(8 * i + 6, part),
               [functools.partial(compressed_branch, part * (k + 1)) for k in range(CMP_PARTS)])

    sid = lax.broadcasted_iota(I32, (n_slc, Q_BLOCK), 0)
    cur = (t0 + lax.broadcasted_iota(I32, (n_slc, Q_BLOCK), 1)) // SLC_BLOCK
    forced = (sid == 0) | (sid == cur) | (sid == cur - 1)
    score = jnp.where(forced, FORCE_SCORE, pslc_ref[...])
    score = jnp.where(sid <= cur, score, -FORCE_SCORE)
    for _ in range(min(N_SELECT, n_slc)):
        mx = jnp.max(score, axis=0, keepdims=True)
        first = jnp.min(jnp.where(score == mx, sid, n_slc), axis=0, keepdims=True)
        score = jnp.where(sid == first, -jnp.inf, score)
    sel = jnp.where((score == -jnp.inf) & (sid <= cur), 1.0, 0.0)
    sel_ref[0, 0] = sel
    flag_ref[0, 0] = jnp.max(sel, axis=1, keepdims=True).astype(I32)


def _nsa_cmp(qt, gt, kv, seq):
    nb = seq // Q_BLOCK
    nch, n_slc = seq // CMP_STRIDE, seq // SLC_BLOCK
    gw = B_HPG * HEAD_DIM
    return pl.pallas_call(
        _nsa_cmp_kernel,
        grid=(B_KV_GROUPS, nb),
        in_specs=[pl.BlockSpec((gw, Q_BLOCK), lambda g, i: (g, i)),
                  pl.BlockSpec((1, nch, HEAD_DIM), lambda g, i: (g, 0, 0)),
                  pl.BlockSpec((1, HEAD_DIM, nch), lambda g, i: (g, 0, 0)),
                  pl.BlockSpec((LANES, Q_BLOCK), lambda g, i: (0, i)),
                  pl.BlockSpec((n_slc, nch), lambda g, i: (0, 0))],
        out_specs=[pl.BlockSpec((Q_BLOCK, gw), lambda g, i: (i, g)),
                   pl.BlockSpec((1, 1, n_slc, Q_BLOCK), lambda g, i: (g, i, 0, 0)),
                   pl.BlockSpec((1, 1, n_slc, 1), lambda g, i: (g, i, 0, 0))],
        out_shape=[jax.ShapeDtypeStruct((seq, D_MODEL), F32),
                   jax.ShapeDtypeStruct((B_KV_GROUPS, nb, n_slc, Q_BLOCK), F32),
                   jax.ShapeDtypeStruct((B_KV_GROUPS, nb, n_slc, 1), I32)],
        scratch_shapes=[pltpu.VMEM((n_slc, Q_BLOCK), F32), pltpu.VMEM((B_HPG, nch, Q_BLOCK), F32)],
        compiler_params=_params(("arbitrary", "arbitrary")),
        name="nsa_compressed_select",
    )(qt, kv["kc"], kv["vct"], gt, _select_matrix(n_slc, nch))


def _fold8(x, op):
    return functools.reduce(op, [x[8 * k:8 * k + 8] for k in range(x.shape[0] // 8)])


def _softmax_tiles(sts, tiles, rel_bias, slopes, m_old):
    ps = [[] for _ in tiles]
    ms, ls, alphas = [], [], []
    for r in range(B_HPG):
        cols = slice(r * Q_BLOCK, (r + 1) * Q_BLOCK)
        us, shifts, tops = [], [], []
        for st, (mask, penalty, off) in zip(sts, tiles):
            u = st[:, cols] - rel_bias[r]
            u = u if mask is None else jnp.where(mask, u, NEG_INF)
            shift = slopes[r] * jnp.asarray(off, F32)
            shift = shift if penalty is None else shift + penalty
            us.append(u)
            shifts.append(shift)
            tops.append(_fold8(u, jnp.maximum) - shift)
        m_new = jnp.max(functools.reduce(jnp.maximum, tops), axis=0, keepdims=True)
        m_new = jnp.maximum(m_new, M_FLOOR)
        if m_old is not None:
            m_new = jnp.maximum(m_old[:, cols], m_new)
            alphas.append(jnp.exp2(m_old[:, cols] - m_new))
        l8 = jnp.zeros((8, Q_BLOCK), F32)
        for t, (u, shift) in enumerate(zip(us, shifts)):
            p = jnp.exp2(u - (m_new + shift))
            l8 = l8 + _fold8(p, jnp.add)
            ps[t].append(p.astype(BF16))
        ms.append(m_new)
        ls.append(jnp.sum(l8, axis=0, keepdims=True))
    ps = [jnp.concatenate(p, axis=1) for p in ps]
    alpha = None if m_old is None else jnp.concatenate(alphas, axis=1)
    return ps, jnp.concatenate(ms, axis=1), jnp.concatenate(ls, axis=1), alpha


def _nsa_sparse_kernel(lst_ref, qt_ref, ks_ref, vst_ref, kw_ref, vwt_ref, sel_ref, gt_ref, oc_ref, o_ref,
                       m_ref, l_ref, alpha_ref, acc_ref, st_ref, p_ref):
    g = pl.program_id(0)
    i = pl.program_id(1)
    t0 = i * Q_BLOCK
    slopes = _group_slopes(g)
    q4 = _q_heads(qt_ref)

    def gate(branch):
        return jnp.concatenate(
            [gt_ref[pl.ds(branch * N_HEADS + g * B_HPG + r, 1), :] for r in range(B_HPG)], axis=1)

    rel_w = (lax.broadcasted_iota(I32, (Q_BLOCK, Q_BLOCK), 1)
             - lax.broadcasted_iota(I32, (Q_BLOCK, Q_BLOCK), 0))
    n_prev = WIN // Q_BLOCK
    rel_w_f = rel_w.astype(F32)
    bias_w = [slopes[r] * rel_w_f for r in range(B_HPG)]
    tiles, blks = [], []
    for c in range(n_prev + 1):
        blk = i - n_prev + c
        off = (n_prev - c) * Q_BLOCK
        mask = (rel_w >= 0) if c == n_prev else (rel_w < 0) if c == 0 else None
        penalty = None if c == n_prev else jnp.where(blk >= 0, 0.0, MASK_PENALTY).astype(F32)
        tiles.append((mask, penalty, off))
        blks.append(jnp.maximum(blk, 0))
    sts = [jnp.dot(kw_ref[0, blk], q4, preferred_element_type=F32) for blk in blks]
    ps, _, l_w, _ = _softmax_tiles(sts, tiles, bias_w, slopes, None)
    acc_w = functools.reduce(lambda a, b: a + b, [
        jnp.dot(vwt_ref[0, blk], p, preferred_element_type=F32) for blk, p in zip(blks, ps)])
    acc_ref[1] = acc_w * (gate(2) / jnp.maximum(l_w, TINY))

    rel_s = (lax.broadcasted_iota(I32, (SLC_BLOCK, Q_BLOCK), 1)
             - lax.broadcasted_iota(I32, (SLC_BLOCK, Q_BLOCK), 0))
    rel_s_f = rel_s.astype(F32)
    bias_s = [slopes[r] * rel_s_f for r in range(B_HPG)]

    def block_id(j, b):
        return lst_ref[0, 0, j * SEL_CHUNK + b]

    def rows(b):
        return slice(b * SLC_BLOCK, (b + 1) * SLC_BLOCK)

    def scores(j, slot):
        for b in range(SEL_CHUNK):
            st_ref[slot, rows(b), :] = jnp.dot(ks_ref[0, block_id(j, b)], q4, preferred_element_type=F32)

    def weighted_values(j):
        v = jnp.concatenate([vst_ref[0, block_id(j, b)] for b in range(SEL_CHUNK)], axis=0)
        return lax.dot_general(v, p_ref[...], (((0,), (0,)), ((), ())), preferred_element_type=F32)

    m_ref[...] = jnp.full(m_ref.shape, M_FLOOR, F32)
    l_ref[...] = jnp.zeros(l_ref.shape, F32)
    alpha_ref[...] = jnp.ones(alpha_ref.shape, F32)
    acc_ref[0] = jnp.zeros(acc_ref.shape[1:], F32)
    p_ref[...] = jnp.zeros(p_ref.shape, BF16)
    scores(0, 0)

    def sel_chunk(j, c):
        slot = lax.rem(j, 2)
        acc_ref[0] = acc_ref[0] * alpha_ref[0:1, :] + weighted_values(jnp.maximum(j - 1, 0))
        tiles, sts = [], []
        for b in range(SEL_CHUNK):
            blk = block_id(j, b)
            off = t0 - blk * SLC_BLOCK
            penalty = jnp.where(sel_ref[0, 0, pl.ds(blk, 1), :] > 0.5, 0.0, MASK_PENALTY)
            mask = (rel_s >= -off) if b < 2 else None
            tiles.append((mask, penalty, off))
            sts.append(st_ref[slot, rows(b), :])
        ps, m, l, alpha = _softmax_tiles(sts, tiles, bias_s, slopes, m_ref[0:1, :])
        for b in range(SEL_CHUNK):
            p_ref[rows(b), :] = ps[b]
        m_ref[0:1, :] = m
        l_ref[0:1, :] = alpha * l_ref[0:1, :] + l
        alpha_ref[0:1, :] = alpha
        scores(j + 1, 1 - slot)
        return c

    n_chunks = lst_ref[0, 0, LIST_W - 1]
    lax.fori_loop(0, n_chunks, sel_chunk, 0)
    acc_s = acc_ref[0] * alpha_ref[0:1, :] + weighted_values(jnp.maximum(n_chunks - 1, 0))

    total = acc_ref[1] + acc_s * (gate(1) / jnp.maximum(l_ref[0:1, :], TINY))
    ot = jnp.concatenate([total[:, r * Q_BLOCK:(r + 1) * Q_BLOCK] for r in range(B_HPG)], axis=0)
    o_ref[...] = (ot.T + oc_ref[...]).astype(BF16)


def _nsa_sparse(lists, qt, gt, kv, sel, oc, seq):
    nb = seq // Q_BLOCK
    n_slc, n_win = seq // SLC_BLOCK, seq // Q_BLOCK
    gw = B_HPG * HEAD_DIM
    grp = lambda g, i: (g, 0, 0, 0)
    return pl.pallas_call(
        _nsa_sparse_kernel,
        grid=(B_KV_GROUPS, nb),
        in_specs=[pl.BlockSpec((1, 1, LIST_W), lambda g, i: (g * nb + i, 0, 0), memory_space=pltpu.SMEM),
                  pl.BlockSpec((gw, Q_BLOCK), lambda g, i: (g, i)),
                  pl.BlockSpec((1, n_slc, SLC_BLOCK, HEAD_DIM), grp),
                  pl.BlockSpec((1, n_slc, SLC_BLOCK, HEAD_DIM), grp),
                  pl.BlockSpec((1, n_win, Q_BLOCK, HEAD_DIM), grp),
                  pl.BlockSpec((1, n_win, HEAD_DIM, Q_BLOCK), grp),
                  pl.BlockSpec((1, 1, n_slc, Q_BLOCK), lambda g, i: (g, i, 0, 0)),
                  pl.BlockSpec((LANES, Q_BLOCK), lambda g, i: (0, i)),
                  pl.BlockSpec((Q_BLOCK, gw), lambda g, i: (i, g))],
        out_specs=pl.BlockSpec((Q_BLOCK, gw), lambda g, i: (i, g)),
        scratch_shapes=[pltpu.VMEM((8, B_HPG * Q_BLOCK), F32),
                        pltpu.VMEM((8, B_HPG * Q_BLOCK), F32),
                        pltpu.VMEM((8, B_HPG * Q_BLOCK), F32),
                        pltpu.VMEM((2, HEAD_DIM, B_HPG * Q_BLOCK), F32),
                        pltpu.VMEM((2, SEL_CHUNK * SLC_BLOCK, B_HPG * Q_BLOCK), F32),
                        pltpu.VMEM((SEL_CHUNK * SLC_BLOCK, B_HPG * Q_BLOCK), BF16)],
        out_shape=jax.ShapeDtypeStruct((seq, D_MODEL), BF16),
        compiler_params=_params(("arbitrary", "arbitrary")),
        name="nsa_selected_window",
    )(lists, qt, kv["ks"], kv["vst"], kv["kw"], kv["vwt"], sel, gt, oc)


def _outproj_b_kernel(o_ref, h_ref, w_ref, g_ref, b_ref, hn_ref, hb_ref):
    mix = jnp.dot(o_ref[...], w_ref[...], preferred_element_type=F32)
    _layer_norm_store(DEEPNORM_ALPHA * h_ref[...] + mix, g_ref, b_ref, hn_ref, hb_ref)


def _outproj_b(o, h, w_out, ln_g, ln_b, tm=512):
    seq = h.shape[0]
    row = lambda i: (i, 0)
    fix = lambda i: (0, 0)
    return pl.pallas_call(
        _outproj_b_kernel,
        grid=(seq // tm,),
        in_specs=[pl.BlockSpec((tm, D_MODEL), row), pl.BlockSpec((tm, D_MODEL), row),
                  pl.BlockSpec((D_MODEL, D_MODEL), fix), pl.BlockSpec((1, D_MODEL), fix),
                  pl.BlockSpec((1, D_MODEL), fix)],
        out_specs=[pl.BlockSpec((tm, D_MODEL), row)] * 2,
        out_shape=[jax.ShapeDtypeStruct((seq, D_MODEL), F32), jax.ShapeDtypeStruct((seq, D_MODEL), BF16)],
        compiler_params=_params(("arbitrary",)),
        name="outproj_ln_b",
    )(o, h, w_out.astype(BF16), ln_g.reshape(1, -1), ln_b.reshape(1, -1))


def _nsa_layer(h, hb, kv, w_q, b_q, w_out, ln_g, ln_b):
    seq = h.shape[0]
    nb, n_slc = seq // Q_BLOCK, seq // SLC_BLOCK
    qt, gt = _qproj(hb, w_q, b_q)
    oc, sel, flags = _nsa_cmp(qt, gt, kv, seq)
    f = flags.reshape(B_KV_GROUPS, nb, n_slc)
    newest = 2 * jnp.arange(nb, dtype=I32)[None, :, None] + 1
    age = jnp.mod(newest - jnp.arange(n_slc, dtype=I32)[None, None, :], n_slc)
    order = jnp.argsort((1 - f) * n_slc + age, axis=2).astype(I32).reshape(B_KV_GROUPS * nb, n_slc)
    n_chunks = (jnp.sum(f, axis=2).reshape(-1) + SEL_CHUNK - 1) // SEL_CHUNK
    lists = jnp.zeros((B_KV_GROUPS * nb, LIST_W), I32).at[:, :n_slc].set(order)
    lists = lists.at[:, LIST_W - 1].set(n_chunks.astype(I32))
    o = _nsa_sparse(lists.reshape(B_KV_GROUPS * nb, 1, LIST_W), qt, gt, kv, sel, oc, seq)
    return _outproj_b(o, h, w_out, ln_g, ln_b)


def kernel(x, a_w_in, a_w_out, b_w_kv, b_cmp_pos, b_cmp_w1, b_cmp_b1, b_cmp_w2, b_cmp_b2, b_w_q, b_b_q, b_w_out, moe_w_router, moe_b_router, moe_w_gate_up, moe_w_down, ln_mix_g, ln_mix_b, ln_ffn_g, ln_ffn_b):
    seq = x.shape[1]
    h = x.reshape(seq, D_MODEL)
    hb = h.astype(BF16)
    for layer in range(DEPTH):
        if layer < N_A_LAYERS:
            h, hb = _dilated_layer(h, hb, a_w_in[layer], a_w_out[layer], ln_mix_g[layer], ln_mix_b[layer])
        else:
            if layer == N_A_LAYERS:
                kv = _nsa_shared_kv(hb, b_w_kv, b_cmp_pos, b_cmp_w1, b_cmp_b1, b_cmp_w2, b_cmp_b2)
            j = layer - N_A_LAYERS
            h, hb = _nsa_layer(h, hb, kv, b_w_q[j], b_b_q[j], b_w_out[j], ln_mix_g[layer], ln_mix_b[layer])
        h, hb = _moe_layer(h, moe_w_router[layer], moe_b_router[layer], moe_w_gate_up[layer],
                           moe_w_down[layer], ln_ffn_g[layer], ln_ffn_b[layer])
    return h.reshape(1, seq, D_MODEL)
```

```python
import functools
import math

import numpy as np
import jax
import jax.numpy as jnp
from jax import lax
from jax.experimental import pallas as pl
from jax.experimental.pallas import tpu as pltpu

F32 = jnp.float32
BF16 = jnp.bfloat16
I32 = jnp.int32

D_MODEL = 1024
DEPTH = 4
N_A_LAYERS = DEPTH // 2
HEAD_DIM = 64
N_HEADS = D_MODEL // HEAD_DIM
Q_BLOCK = 128
LN_EPS = 1e-5
NEG_INF = -1e30
TINY = 1e-30

A_WINDOWS = (128, 512, 2048)
A_DILATIONS = (1, 4, 16)
A_N_GROUPS = 3

N_EXPERT_GROUPS = 4
EXPERTS_PER_GROUP = 4
N_EXPERTS = 16
D_EXPERT = 256

DEEPNORM_ALPHA = (2.0 * DEPTH) ** 0.25

LANES = 128
MOE_TILE = 256
VMEM_LIMIT = 56 * 1024 * 1024


def _slopes(n):
    return [float(v) for v in np.asarray(2.0 ** (-8.0 * np.arange(1, n + 1) / n), dtype=np.float32)]


def _params(sem):
    return pltpu.CompilerParams(dimension_semantics=sem, vmem_limit_bytes=VMEM_LIMIT)


def _mm_kernel(x_ref, w_ref, o_ref):
    o_ref[...] = jnp.dot(x_ref[...], w_ref[...], preferred_element_type=F32).astype(o_ref.dtype)


def _matmul(x, w, out_dtype, tm, tn, name):
    m, k = x.shape
    n = w.shape[1]
    return pl.pallas_call(
        _mm_kernel,
        grid=(n // tn, m // tm),
        in_specs=[pl.BlockSpec((tm, k), lambda j, i: (i, 0)),
                  pl.BlockSpec((k, tn), lambda j, i: (0, j))],
        out_specs=pl.BlockSpec((tm, tn), lambda j, i: (i, j)),
        out_shape=jax.ShapeDtypeStruct((m, n), out_dtype),
        compiler_params=_params(("arbitrary", "arbitrary")),
        name=name,
    )(x, w)


def _layer_norm_store(z, g_ref, b_ref, h_ref, hb_ref):
    mu = jnp.mean(z, axis=-1, keepdims=True)
    zc = z - mu
    var = jnp.mean(zc * zc, axis=-1, keepdims=True)
    y = zc * lax.rsqrt(var + LN_EPS) * g_ref[...] + b_ref[...]
    h_ref[...] = y
    hb_ref[...] = y.astype(BF16)


def _split_bf16(x):
    hi = x.astype(BF16)
    lo = (x - hi.astype(F32)).astype(BF16)
    return hi, lo


def _dil_attn_kernel(q_ref, kp_ref, kc_ref, vp_ref, vc_ref, o_ref, lse_ref, bias_ref, *, dil, slopes):
    r = pl.program_id(0)
    jb = pl.program_id(1)
    qb = Q_BLOCK

    @pl.when((r == 0) & (jb == 0))
    def _():
        row = lax.broadcasted_iota(I32, (qb, 2 * qb), 0)
        col = lax.broadcasted_iota(I32, (qb, 2 * qb), 1)
        dist = row + qb - col
        valid = (dist >= 0) & (dist <= qb)
        dist_tok = (dist * dil).astype(F32)
        for h in range(N_HEADS):
            b = jnp.where(valid, -(slopes[h] * LOG2E * dist_tok), NEG_INF)
            bias_ref[1, h] = b
            bias_ref[0, h] = jnp.where(col >= qb, b, NEG_INF)

    first = jnp.minimum(jb, 1)
    lane = lax.broadcasted_iota(I32, (1, LANES), 1)
    lo_half = lane < HEAD_DIM
    lse_acc = jnp.zeros((qb, LANES), F32)
    for hp in range(N_HEADS // 2):
        sl = slice(hp * LANES, (hp + 1) * LANES)
        q2 = q_ref[:, sl]
        k2 = jnp.concatenate([kp_ref[:, sl], kc_ref[:, sl]], axis=0)
        v2 = jnp.concatenate([vp_ref[:, sl], vc_ref[:, sl]], axis=0)
        outs = []
        for half in range(2):
            h = 2 * hp + half
            keep = lo_half if half == 0 else jnp.logical_not(lo_half)
            qm = jnp.where(keep, q2, jnp.zeros_like(q2))
            s = lax.dot_general(qm, k2, (((1,), (1,)), ((), ())), preferred_element_type=F32)
            s = s + bias_ref[first, h]
            m = jnp.max(s, axis=-1, keepdims=True)
            p = jnp.exp2(s - m)
            l = jnp.sum(p, axis=-1, keepdims=True)
            o = jnp.dot(p.astype(BF16), v2, preferred_element_type=F32)
            outs.append(o * (1.0 / l))
            lse_acc = jnp.where(lane == h, m + jnp.log2(l), lse_acc)
        o_ref[:, sl] = jnp.where(lo_half, outs[0], outs[1]).astype(BF16)
    lse_ref[...] = lse_acc


def _qkv_kernel(x_ref, w_ref, o_ref, acc_ref, *, dil):
    acc = jnp.dot(x_ref[...], w_ref[...], preferred_element_type=F32)
    acc = acc * jnp.where(pl.program_id(0) == 0, HEAD_DIM ** -0.5 * LOG2E, 1.0)
    if dil == 1:
        o_ref[0] = acc.astype(BF16)
    else:
        n_chunks = acc.shape[1] // LANES
        for c in range(n_chunks):
            acc_ref[c] = acc[:, c * LANES:(c + 1) * LANES]
        n = acc.shape[0] // dil
        for r in range(dil):
            o_ref[r] = jnp.concatenate(
                [acc_ref[c, pl.ds(r, n, stride=dil), :] for c in range(n_chunks)], axis=1).astype(BF16)


def _qkv_group(hb, w_g, g, tm=512):
    seq = hb.shape[0]
    dil = A_DILATIONS[g]
    return pl.pallas_call(
        functools.partial(_qkv_kernel, dil=dil),
        grid=(3, seq // tm),
        in_specs=[pl.BlockSpec((tm, D_MODEL), lambda c, i: (i, 0)),
                  pl.BlockSpec((D_MODEL, D_MODEL), lambda c, i: (0, c))],
        out_specs=pl.BlockSpec((dil, tm // dil, D_MODEL), lambda c, i: (0, i, c)),
        out_shape=jax.ShapeDtypeStruct((dil, seq // dil, 3 * D_MODEL), BF16),
        scratch_shapes=[pltpu.VMEM((D_MODEL // LANES, tm, LANES), F32)],
        compiler_params=_params(("arbitrary", "arbitrary")),
        name=f"qkv_proj_g{g}",
    )(hb, w_g)


def _dilated_group(qkv_g, g, seq):
    dil = A_DILATIONS[g]
    rows = seq // dil
    nblk = rows // Q_BLOCK
    blk = (None, Q_BLOCK, D_MODEL)
    kern = functools.partial(_dil_attn_kernel, dil=dil, slopes=_slopes(N_HEADS))
    return pl.pallas_call(
        kern,
        grid=(dil, nblk),
        in_specs=[
            pl.BlockSpec(blk, lambda r, j: (r, j, 0)),
            pl.BlockSpec(blk, lambda r, j: (r, jnp.maximum(j - 1, 0), 1)),
            pl.BlockSpec(blk, lambda r, j: (r, j, 1)),
            pl.BlockSpec(blk, lambda r, j: (r, jnp.maximum(j - 1, 0), 2)),
            pl.BlockSpec(blk, lambda r, j: (r, j, 2)),
        ],
        out_specs=[pl.BlockSpec(blk, lambda r, j: (r, j, 0)),
                   pl.BlockSpec((None, Q_BLOCK, LANES), lambda r, j: (r, j, 0))],
        out_shape=[jax.ShapeDtypeStruct((dil, rows, D_MODEL), BF16),
                   jax.ShapeDtypeStruct((dil, rows, LANES), F32)],
        scratch_shapes=[pltpu.VMEM((2, N_HEADS, Q_BLOCK, 2 * Q_BLOCK), F32)],
        compiler_params=_params(("arbitrary", "arbitrary")),
        name=f"dilated_attn_g{g}",
    )(qkv_g, qkv_g, qkv_g, qkv_g, qkv_g)


def _head_expand_matrix():
    h = np.arange(LANES)[:, None]
    c = np.arange(D_MODEL)[None, :]
    return jnp.asarray((c // HEAD_DIM == h).astype(np.float32), BF16)


def _to_token_order(src_ref, dst_ref, dil):
    n = src_ref.shape[1]
    n_chunks = dst_ref.shape[0]
    if dil == 1:
        return src_ref[0].astype(F32)
    for r in range(dil):
        v = src_ref[r].astype(F32)
        for c in range(n_chunks):
            dst_ref[c, pl.ds(r, n, stride=dil), :] = v[:, c * LANES:(c + 1) * LANES]
    return jnp.concatenate([dst_ref[c] for c in range(n_chunks)], axis=1)


def _outproj_a_kernel(o0_ref, o1_ref, o2_ref, l0_ref, l1_ref, l2_ref, h_ref, w_ref, e_ref, g_ref, b_ref,
                      hn_ref, hb_ref, onat_ref, lnat_ref):
    l0, l1, l2 = [_to_token_order(l_ref, lnat_ref, A_DILATIONS[gi])
                  for gi, l_ref in enumerate((l0_ref, l1_ref, l2_ref))]
    m = jnp.maximum(jnp.maximum(l0, l1), l2)
    e0, e1, e2 = jnp.exp2(l0 - m), jnp.exp2(l1 - m), jnp.exp2(l2 - m)
    inv = 1.0 / (e0 + e1 + e2)
    o = jnp.zeros(h_ref.shape, F32)
    for gi, (e, o_ref) in enumerate(((e0, o0_ref), (e1, o1_ref), (e2, o2_ref))):
        hi, lo = _split_bf16(e * inv)
        wexp = (jnp.dot(hi, e_ref[...], preferred_element_type=F32)
                + jnp.dot(lo, e_ref[...], preferred_element_type=F32))
        o = o + wexp * _to_token_order(o_ref, onat_ref, A_DILATIONS[gi])
    mix = jnp.dot(o.astype(BF16), w_ref[...], preferred_element_type=F32)
    _layer_norm_store(DEEPNORM_ALPHA * h_ref[...] + mix, g_ref, b_ref, hn_ref, hb_ref)


def _outproj_a(outs, lses, h, w_out, ln_g, ln_b, tm=512):
    seq = h.shape[0]
    row = lambda i: (i, 0)
    fix = lambda i: (0, 0)
    cls = lambda i: (0, i, 0)
    return pl.pallas_call(
        _outproj_a_kernel,
        grid=(seq // tm,),
        in_specs=[pl.BlockSpec((d, tm // d, D_MODEL), cls) for d in A_DILATIONS]
        + [pl.BlockSpec((d, tm // d, LANES), cls) for d in A_DILATIONS] + [
            pl.BlockSpec((tm, D_MODEL), row),
            pl.BlockSpec((D_MODEL, D_MODEL), fix),
            pl.BlockSpec((LANES, D_MODEL), fix),
            pl.BlockSpec((1, D_MODEL), fix),
            pl.BlockSpec((1, D_MODEL), fix)],
        out_specs=[pl.BlockSpec((tm, D_MODEL), row)] * 2,
        out_shape=[jax.ShapeDtypeStruct((seq, D_MODEL), F32), jax.ShapeDtypeStruct((seq, D_MODEL), BF16)],
        scratch_shapes=[pltpu.VMEM((D_MODEL // LANES, tm, LANES), F32), pltpu.VMEM((1, tm, LANES), F32)],
        compiler_params=_params(("arbitrary",)),
        name="outproj_ln_a",
    )(*outs, *lses, h, w_out.astype(BF16), _head_expand_matrix(), ln_g.reshape(1, -1), ln_b.reshape(1, -1))


def _dilated_layer(h, hb, w_in, w_out, ln_g, ln_b):
    seq = h.shape[0]
    w = w_in.astype(BF16)
    gw = 3 * D_MODEL
    outs, lses = [], []
    for g in range(A_N_GROUPS):
        qkv_g = _qkv_group(hb, w[:, g * gw:(g + 1) * gw], g)
        o, lse = _dilated_group(qkv_g, g, seq)
        outs.append(o)
        lses.append(lse)
    return _outproj_a(outs, lses, h, w_out, ln_g, ln_b)


ROUTER_ROWS = 24


def _router_kernel(h_ref, w_ref, b_ref, ids_ref, wts_ref, cnt_ref, base_ref):
    i = pl.program_id(0)
    tm = h_ref.shape[0]

    @pl.when(i == 0)
    def _():
        base_ref[...] = jnp.zeros_like(base_ref)

    lt = lax.dot_general(w_ref[...], h_ref[...], (((1,), (1,)), ((), ())),
                         precision=lax.Precision.HIGHEST, preferred_element_type=F32)
    lt = lt + b_ref[:, 0:1]
    gl = [lt[k:k + 1, :] for k in range(N_EXPERT_GROUPS)]
    best, gidx = gl[0], jnp.zeros((1, tm), I32)
    for k in range(1, N_EXPERT_GROUPS):
        take = gl[k] > best
        best = jnp.where(take, gl[k], best)
        gidx = jnp.where(take, k, gidx)
    denom = gl[0] * 0.0
    for k in range(N_EXPERT_GROUPS):
        denom = denom + jnp.exp(gl[k] - best)
    g_w = 1.0 / denom
    e_in = []
    for e in range(EXPERTS_PER_GROUP):
        v = lt[N_EXPERT_GROUPS + e:N_EXPERT_GROUPS + e + 1, :]
        for k in range(1, N_EXPERT_GROUPS):
            row = N_EXPERT_GROUPS + k * EXPERTS_PER_GROUP + e
            v = jnp.where(gidx == k, lt[row:row + 1, :], v)
        e_in.append(v)
    v1, i1 = e_in[0], jnp.zeros((1, tm), I32)
    for e in range(1, EXPERTS_PER_GROUP):
        take = e_in[e] > v1
        v1 = jnp.where(take, e_in[e], v1)
        i1 = jnp.where(take, e, i1)
    v2, i2 = jnp.full((1, tm), -jnp.inf, F32), jnp.zeros((1, tm), I32)
    for e in range(EXPERTS_PER_GROUP):
        take = (e_in[e] > v2) & (i1 != e)
        v2 = jnp.where(take, e_in[e], v2)
        i2 = jnp.where(take, e, i2)
    t = jnp.exp(v2 - v1)
    w1 = g_w / (1.0 + t)
    w2 = g_w * t / (1.0 + t)
    e1 = gidx * EXPERTS_PER_GROUP + i1
    e2 = gidx * EXPERTS_PER_GROUP + i2

    erow = lax.broadcasted_iota(I32, (N_EXPERTS, tm), 0)
    oh1 = (erow == e1).astype(F32)
    oh2 = (erow == e2).astype(F32)
    c = oh1 + oh2
    tri = (lax.broadcasted_iota(I32, (tm, tm), 0) < lax.broadcasted_iota(I32, (tm, tm), 1)).astype(BF16)
    excl = jnp.dot(c.astype(BF16), tri, preferred_element_type=F32) + base_ref[...]
    r1 = jnp.sum(oh1 * excl, axis=0, keepdims=True).astype(I32)
    r2 = jnp.sum(oh2 * excl, axis=0, keepdims=True).astype(I32)
    base_ref[...] = base_ref[...] + jnp.sum(c, axis=1, keepdims=True)

    zi = jnp.zeros((1, tm), I32)
    ids_ref[...] = jnp.concatenate([e1, e2, r1, r2, zi, zi, zi, zi], axis=0)
    zf = jnp.zeros((1, tm), F32)
    wts_ref[...] = jnp.concatenate([w1, w2, zf, zf, zf, zf, zf, zf], axis=0)
    cnt_ref[...] = jnp.broadcast_to(base_ref[...], cnt_ref.shape)


def _router(h, w_router, b_router, tm=512):
    seq = h.shape[0]
    n_log = N_EXPERT_GROUPS + N_EXPERTS
    w_t = jnp.zeros((ROUTER_ROWS, D_MODEL), F32).at[:n_log].set(w_router.T)
    b_t = jnp.zeros((ROUTER_ROWS, LANES), F32).at[:n_log].set(jnp.broadcast_to(b_router[:, None], (n_log, LANES)))
    return pl.pallas_call(
        _router_kernel,
        grid=(seq // tm,),
        in_specs=[pl.BlockSpec((tm, D_MODEL), lambda i: (i, 0)),
                  pl.BlockSpec((ROUTER_ROWS, D_MODEL), lambda i: (0, 0)),
                  pl.BlockSpec((ROUTER_ROWS, LANES), lambda i: (0, 0))],
        out_specs=[pl.BlockSpec((8, tm), lambda i: (0, i)),
                   pl.BlockSpec((8, tm), lambda i: (0, i)),
                   pl.BlockSpec((N_EXPERTS, LANES), lambda i: (0, 0))],
        out_shape=[jax.ShapeDtypeStruct((8, seq), I32), jax.ShapeDtypeStruct((8, seq), F32),
                   jax.ShapeDtypeStruct((N_EXPERTS, LANES), F32)],
        scratch_shapes=[pltpu.VMEM((N_EXPERTS, 1), F32)],
        compiler_params=_params(("arbitrary",)),
        name="moe_router",
    )(h, w_t, b_t)


ROW_TILE = D_MODEL // LANES


def _rows_to_tiles(x, dst_ref):
    n = x.shape[0]
    for c in range(ROW_TILE):
        dst_ref[pl.ds(c, n, stride=ROW_TILE), :] = x[:, c * LANES:(c + 1) * LANES]


def _tiles_to_rows(src_ref):
    n = src_ref.shape[0] // ROW_TILE
    return jnp.concatenate([src_ref[pl.ds(c, n, stride=ROW_TILE), :] for c in range(ROW_TILE)], axis=1)


def _row_tile(ref, p):
    return ref.at[pl.ds(pl.multiple_of(p * ROW_TILE, ROW_TILE), ROW_TILE)]


def _dispatch_kernel(pos_ref, h_ref, xs_in_ref, xs_ref, ht_ref, sem):
    del xs_in_ref
    tm = h_ref.shape[0]
    _rows_to_tiles(h_ref[...], ht_ref)

    def row_copy(i, p):
        return pltpu.make_async_copy(_row_tile(ht_ref, i), _row_tile(xs_ref, p), sem)

    def start(i, c):
        row_copy(i, pos_ref[0, 0, i]).start(priority=0)
        row_copy(i, pos_ref[0, 0, tm + i]).start(priority=1)
        return c

    def wait(i, c):
        row_copy(0, 0).wait()
        row_copy(0, 0).wait()
        return c

    lax.fori_loop(0, tm, start, 0, unroll=8)
    lax.fori_loop(0, tm, wait, 0, unroll=8)


def _dispatch(h, pos, n_rows, tm=256):
    seq = h.shape[0]
    nt = seq // tm
    pos3 = pos.reshape(2, nt, tm).transpose(1, 0, 2).reshape(nt, 1, 2 * tm)
    return pl.pallas_call(
        _dispatch_kernel,
        grid=(nt,),
        in_specs=[pl.BlockSpec((1, 1, 2 * tm), lambda i: (i, 0, 0), memory_space=pltpu.SMEM),
                  pl.BlockSpec((tm, D_MODEL), lambda i: (i, 0)),
                  pl.BlockSpec(memory_space=pl.ANY)],
        out_specs=pl.BlockSpec(memory_space=pl.ANY),
        out_shape=jax.ShapeDtypeStruct((n_rows * ROW_TILE, LANES), F32),
        scratch_shapes=[pltpu.VMEM((tm * ROW_TILE, LANES), F32), pltpu.SemaphoreType.DMA(())],
        input_output_aliases={2: 0},
        compiler_params=_params(("arbitrary",)),
        name="moe_dispatch",
    )(pos3, h, jnp.zeros((n_rows * ROW_TILE, LANES), F32))


def _expert_kernel(te_ref, x_ref, wgu_ref, wd_ref, y_ref):
    del te_ref
    gu = jnp.dot(_tiles_to_rows(x_ref).astype(BF16), wgu_ref[0], preferred_element_type=F32)
    gate, up = gu[:, :D_EXPERT], gu[:, D_EXPERT:]
    hid = gate * (1.0 / (1.0 + jnp.exp(-gate))) * up
    _rows_to_tiles(jnp.dot(hid.astype(BF16), wd_ref[0], preferred_element_type=F32), y_ref)


def _experts(xs, tile_expert, w_gate_up, w_down):
    blk = MOE_TILE * ROW_TILE
    return pl.pallas_call(
        _expert_kernel,
        grid_spec=pltpu.PrefetchScalarGridSpec(
            num_scalar_prefetch=1,
            grid=(xs.shape[0] // blk,),
            in_specs=[pl.BlockSpec((blk, LANES), lambda i, te: (i, 0)),
                      pl.BlockSpec((1, D_MODEL, 2 * D_EXPERT), lambda i, te: (te[i], 0, 0)),
                      pl.BlockSpec((1, D_EXPERT, D_MODEL), lambda i, te: (te[i], 0, 0))],
            out_specs=pl.BlockSpec((blk, LANES), lambda i, te: (i, 0))),
        out_shape=jax.ShapeDtypeStruct(xs.shape, F32),
        compiler_params=_params(("arbitrary",)),
        name="moe_experts",
    )(tile_expert, xs, w_gate_up.astype(BF16), w_down.astype(BF16))


def _combine_kernel(pos_ref, pos_next_ref, ys_ref, h_ref, w_ref, g_ref, b_ref, hn_ref, hb_ref, buf_ref, sem):
    step = pl.program_id(0)
    tm = h_ref.shape[0]
    slot = lax.rem(step, 2)

    def row_copy(s, k, i, p):
        return pltpu.make_async_copy(_row_tile(ys_ref, p), _row_tile(buf_ref.at[s, k], i), sem.at[s])

    def gather(p_ref, s):
        def start(i, c):
            row_copy(s, 0, i, p_ref[0, 0, i]).start(priority=0)
            row_copy(s, 1, i, p_ref[0, 0, tm + i]).start(priority=1)
            return c
        lax.fori_loop(0, tm, start, 0, unroll=8)

    @pl.when(step == 0)
    def _():
        gather(pos_ref, 0)

    @pl.when(step + 1 < pl.num_programs(0))
    def _():
        gather(pos_next_ref, 1 - slot)

    def wait(i, c):
        row_copy(slot, 0, 0, 0).wait()
        row_copy(slot, 1, 0, 0).wait()
        return c

    lax.fori_loop(0, tm, wait, 0, unroll=8)
    w = w_ref[...]
    ffn = (w[:, 0:1] * _tiles_to_rows(buf_ref.at[slot, 0])
           + w[:, 1:2] * _tiles_to_rows(buf_ref.at[slot, 1]))
    _layer_norm_store(DEEPNORM_ALPHA * h_ref[...] + ffn, g_ref, b_ref, hn_ref, hb_ref)


def _combine(ys, pos, wts, h, ln_g, ln_b, tm=256):
    seq = h.shape[0]
    nt = seq // tm
    pos3 = pos.reshape(2, nt, tm).transpose(1, 0, 2).reshape(nt, 1, 2 * tm)
    row = lambda i: (i, 0)
    fix = lambda i: (0, 0)
    return pl.pallas_call(
        _combine_kernel,
        grid=(nt,),
        in_specs=[pl.BlockSpec((1, 1, 2 * tm), lambda i: (i, 0, 0), memory_space=pltpu.SMEM),
                  pl.BlockSpec((1, 1, 2 * tm), lambda i: (jnp.minimum(i + 1, nt - 1), 0, 0),
                               memory_space=pltpu.SMEM),
                  pl.BlockSpec(memory_space=pl.ANY),
                  pl.BlockSpec((tm, D_MODEL), row),
                  pl.BlockSpec((tm, 2), row),
                  pl.BlockSpec((1, D_MODEL), fix),
                  pl.BlockSpec((1, D_MODEL), fix)],
        out_specs=[pl.BlockSpec((tm, D_MODEL), row)] * 2,
        out_shape=[jax.ShapeDtypeStruct((seq, D_MODEL), F32), jax.ShapeDtypeStruct((seq, D_MODEL), BF16)],
        scratch_shapes=[pltpu.VMEM((2, 2, tm * ROW_TILE, LANES), F32), pltpu.SemaphoreType.DMA((2,))],
        compiler_params=_params(("arbitrary",)),
        name="moe_combine_ln",
    )(pos3, pos3, ys, h, wts, ln_g.reshape(1, -1), ln_b.reshape(1, -1))


def _moe_layer(h, w_router, b_router, w_gate_up, w_down, ln_g, ln_b):
    seq = h.shape[0]
    ids, wts, cnt = _router(h, w_router, b_router)
    counts = cnt[:, 0].astype(I32)
    padded = ((counts + MOE_TILE - 1) // MOE_TILE) * MOE_TILE
    ends = jnp.cumsum(padded)
    offs = ends - padded
    eids = jnp.arange(N_EXPERTS, dtype=I32)[:, None, None]
    pos = jnp.sum(jnp.where(ids[None, 0:2] == eids, offs[:, None, None], 0), axis=0) + ids[2:4]
    n_rows = 2 * seq + N_EXPERTS * MOE_TILE
    tile_start = jnp.arange(n_rows // MOE_TILE, dtype=I32) * MOE_TILE
    tile_expert = jnp.minimum(jnp.sum((tile_start[:, None] >= ends[None, :]).astype(I32), axis=1), N_EXPERTS - 1)
    xs = _dispatch(h, pos, n_rows)
    ys = _experts(xs, tile_expert.astype(I32), w_gate_up, w_down)
    return _combine(ys, pos, wts[0:2].T, h, ln_g, ln_b)


B_KV_GROUPS = 4
B_HPG = N_HEADS // B_KV_GROUPS
CMP_LEN = 32
CMP_STRIDE = 16
CMP_HIDDEN = 256
SLC_BLOCK = 64
N_SELECT = 16
WIN = 512
FORCE_SCORE = 1e4
LIST_W = 272
LOG2E = math.log2(math.e)
MASK_PENALTY = 1e30
M_FLOOR = -1e29
CMP_PARTS = 4
SEL_CHUNK = 4


def _group_slopes(g):
    sl = [v * LOG2E for v in _slopes(N_HEADS)]
    out = []
    for r in range(B_HPG):
        v = jnp.full((1, 1), sl[r], F32)
        for k in range(1, B_KV_GROUPS):
            v = jnp.where(g == k, sl[k * B_HPG + r], v)
        out.append(v)
    return out


def _qproj_kernel(hb_ref, w_ref, b_ref, qt_ref, gt_ref):
    pt = lax.dot_general(w_ref[...], hb_ref[...], (((1,), (1,)), ((), ())), preferred_element_type=F32)
    pt = pt + b_ref[:, 0:1]
    qt_ref[...] = (pt[:D_MODEL] * (HEAD_DIM ** -0.5 * LOG2E)).astype(BF16)
    gt_ref[...] = 1.0 / (1.0 + jnp.exp(-pt[D_MODEL:]))


def _qproj(hb, w_q, b_q, tm=512):
    seq = hb.shape[0]
    n = w_q.shape[1]
    npad = D_MODEL + LANES
    w_t = jnp.zeros((npad, D_MODEL), BF16).at[:n].set(w_q.T.astype(BF16))
    b_t = jnp.zeros((npad, LANES), F32).at[:n].set(jnp.broadcast_to(b_q[:, None], (n, LANES)))
    return pl.pallas_call(
        _qproj_kernel,
        grid=(seq // tm,),
        in_specs=[pl.BlockSpec((tm, D_MODEL), lambda i: (i, 0)),
                  pl.BlockSpec((npad, D_MODEL), lambda i: (0, 0)),
                  pl.BlockSpec((npad, LANES), lambda i: (0, 0))],
        out_specs=[pl.BlockSpec((D_MODEL, tm), lambda i: (0, i)),
                   pl.BlockSpec((LANES, tm), lambda i: (0, i))],
        out_shape=[jax.ShapeDtypeStruct((D_MODEL, seq), BF16), jax.ShapeDtypeStruct((LANES, seq), F32)],
        compiler_params=_params(("arbitrary",)),
        name="nsa_qproj",
    )(hb, w_t, b_t)


def _compress_kernel(x_ref, w1a_ref, w1b_ref, pa_ref, pb_ref, b1_ref, w2_ref, b2_ref, o_ref):
    x = x_ref[0, 0]
    n = x.shape[0]
    a = jnp.dot((x + pa_ref[0]).astype(BF16), w1a_ref[0], preferred_element_type=F32)
    b = jnp.dot((x + pb_ref[0]).astype(BF16), w1b_ref[0], preferred_element_type=F32)
    b = pltpu.roll(b, n - 1, 0)
    u = a + b + b1_ref[0]
    hid = 0.5 * u * (1.0 + jnp.tanh(math.sqrt(2.0 / math.pi) * (u + 0.044715 * (u * u * u))))
    o_ref[0, 0] = (jnp.dot(hid.astype(BF16), w2_ref[0], preferred_element_type=F32) + b2_ref[0]).astype(BF16)


def _compress(kvc, cmp_pos, cmp_w1, cmp_b1, cmp_w2, cmp_b2):
    seq = kvc.shape[0]
    nch = seq // CMP_STRIDE
    half = CMP_STRIDE * HEAD_DIM
    chunks = kvc.reshape(nch, CMP_STRIDE, 2, B_KV_GROUPS, HEAD_DIM).transpose(2, 3, 0, 1, 4).reshape(
        2, B_KV_GROUPS, nch, half)
    pos = cmp_pos.reshape(2, 1, CMP_LEN * HEAD_DIM)
    w1 = cmp_w1.astype(BF16)
    jfix = lambda j, g: (j, 0, 0)
    return pl.pallas_call(
        _compress_kernel,
        grid=(2, B_KV_GROUPS),
        in_specs=[pl.BlockSpec((1, 1, nch, half), lambda j, g: (j, g, 0, 0)),
                  pl.BlockSpec((1, half, CMP_HIDDEN), jfix),
                  pl.BlockSpec((1, half, CMP_HIDDEN), lambda j, g: (j, 1, 0)),
                  pl.BlockSpec((1, 1, half), jfix),
                  pl.BlockSpec((1, 1, half), lambda j, g: (j, 0, 1)),
                  pl.BlockSpec((1, 1, CMP_HIDDEN), jfix),
                  pl.BlockSpec((1, CMP_HIDDEN, HEAD_DIM), jfix),
                  pl.BlockSpec((1, 1, HEAD_DIM), jfix)],
        out_specs=pl.BlockSpec((1, 1, nch, HEAD_DIM), lambda j, g: (j, g, 0, 0)),
        out_shape=jax.ShapeDtypeStruct((2, B_KV_GROUPS, nch, HEAD_DIM), BF16),
        compiler_params=_params(("arbitrary", "arbitrary")),
        name="nsa_compress",
    )(chunks, w1, w1, pos, pos, cmp_b1.reshape(2, 1, CMP_HIDDEN), cmp_w2.astype(BF16),
      cmp_b2.reshape(2, 1, HEAD_DIM))


def _nsa_shared_kv(hb, w_kv, cmp_pos, cmp_w1, cmp_b1, cmp_w2, cmp_b2):
    seq = hb.shape[0]
    gw = B_KV_GROUPS * HEAD_DIM
    w = w_kv.astype(BF16)
    kvc = _matmul(hb, w[:, :2 * gw], F32, 512, 2 * gw, "kv_proj_cmp")
    kvr = _matmul(hb, w[:, 2 * gw:], BF16, 512, 4 * gw, "kv_proj_rest")
    cmp = _compress(kvc, cmp_pos, cmp_w1, cmp_b1, cmp_w2, cmp_b2)
    n_slc, n_win = seq // SLC_BLOCK, seq // Q_BLOCK

    def blocks(cols, nblk, blk, transpose):
        t = cols.reshape(nblk, blk, B_KV_GROUPS, HEAD_DIM)
        return t.transpose(2, 0, 3, 1) if transpose else t.transpose(2, 0, 1, 3)

    return dict(
        kc=cmp[0], vct=cmp[1].transpose(0, 2, 1),
        ks=blocks(kvr[:, 0 * gw:1 * gw], n_slc, SLC_BLOCK, False),
        vst=blocks(kvr[:, 1 * gw:2 * gw], n_slc, SLC_BLOCK, False),
        kw=blocks(kvr[:, 2 * gw:3 * gw], n_win, Q_BLOCK, False),
        vwt=blocks(kvr[:, 3 * gw:4 * gw], n_win, Q_BLOCK, True))


def _select_matrix(n_slc, nch):
    m = np.zeros((n_slc, nch), np.float32)
    r_s, r_c = SLC_BLOCK // CMP_STRIDE, CMP_LEN // CMP_STRIDE
    for a in range(r_s):
        for b in range(r_c):
            idx = r_s * np.arange(n_slc) + a - b
            ok = (idx >= 0) & (idx < nch - r_c + 1)
            m[np.arange(n_slc)[ok], idx[ok]] += 1.0
    return jnp.asarray(m, BF16)


def _q_heads(qt_ref):
    return jnp.concatenate([qt_ref[r * HEAD_DIM:(r + 1) * HEAD_DIM, :] for r in range(B_HPG)], axis=1)


def _nsa_cmp_kernel(qt_ref, kc_ref, vct_ref, gt_ref, mt_ref, oc_ref, sel_ref, flag_ref, bias_ref):
    g = pl.program_id(0)
    i = pl.program_id(1)
    nch = kc_ref.shape[1]
    n_slc = mt_ref.shape[0]
    t0 = i * Q_BLOCK

    @pl.when(i == 0)
    def _():
        slopes = _group_slopes(g)
        rel = (lax.broadcasted_iota(I32, (nch, Q_BLOCK), 1)
               - (CMP_STRIDE * lax.broadcasted_iota(I32, (nch, Q_BLOCK), 0) + (CMP_LEN - 1))).astype(F32)
        for r in range(B_HPG):
            bias_ref[r] = slopes[r] * rel

    def compressed_branch(n):
        cidx = lax.broadcasted_iota(I32, (n, Q_BLOCK), 0)
        qidx = lax.broadcasted_iota(I32, (n, Q_BLOCK), 1)
        valid = (t0 + qidx) - (CMP_STRIDE * cidx + (CMP_LEN - 1)) >= 0
        st = jnp.dot(kc_ref[0, :n, :], _q_heads(qt_ref), preferred_element_type=F32)
        imp = jnp.zeros((n, Q_BLOCK), F32)
        outs = []
        for r in range(B_HPG):
            s = st[:, r * Q_BLOCK:(r + 1) * Q_BLOCK] - bias_ref[r, :n, :]
            s = jnp.where(valid, s, NEG_INF)
            m = jnp.maximum(jnp.max(s, axis=0, keepdims=True), M_FLOOR)
            p = jnp.exp2(s - m)
            l = jnp.sum(p, axis=0, keepdims=True)
            pn = p * (1.0 / jnp.maximum(l, TINY))
            imp = imp + pn
            o = jnp.dot(vct_ref[0, :, :n], pn.astype(BF16), preferred_element_type=F32)
            outs.append(o * gt_ref[pl.ds(g * B_HPG + r, 1), :])
        oc_ref[...] = jnp.concatenate(outs, axis=0).T
        hi = imp.astype(BF16)
        r1 = imp - hi.astype(F32)
        mid = r1.astype(BF16)
        lo = (r1 - mid.astype(F32)).astype(BF16)
        ns = n * n_slc // nch
        mt = mt_ref[:ns, :n]
        p_slc = (jnp.dot(mt, hi, preferred_element_type=F32) + jnp.dot(mt, mid, preferred_element_type=F32)
                 + jnp.dot(mt, lo, preferred_element_type=F32))
        sid = lax.broadcasted_iota(I32, (ns, Q_BLOCK), 0)
        cur = (t0 + lax.broadcasted_iota(I32, (ns, Q_BLOCK), 1)) // SLC_BLOCK
        forced = (sid == 0) | (sid == cur) | (sid == cur - 1)
        score = jnp.where(forced, FORCE_SCORE, p_slc)
        score = jnp.where(sid <= cur, score, -FORCE_SCORE)
        for _ in range(N_SELECT):
            mx = jnp.max(score, axis=0, keepdims=True)
            first = jnp.min(jnp.where(score == mx, sid, ns), axis=0, keepdims=True)
            score = jnp.where(sid == first, -jnp.inf, score)
        sel = jnp.where((score == -jnp.inf) & (sid <= cur), 1.0, 0.0)
        sel_ref[0, 0, :ns, :] = sel
        flag_ref[0, 0, :ns, :] = jnp.max(sel, axis=1, keepdims=True).astype(I32)
        if ns < n_slc:
            sel_ref[0, 0, ns:, :] = jnp.zeros((n_slc - ns, Q_BLOCK), F32)
            flag_ref[0, 0, ns:, :] = jnp.zeros((n_slc - ns, 1), I32)

    part = nch // CMP_PARTS
    assert part * n_slc // nch >= N_SELECT
    lax.switch(lax.div(8 * i + 6, part),
               [functools.partial(compressed_branch, part * (k + 1)) for k in range(CMP_PARTS)])


def _nsa_cmp(qt, gt, kv, seq):
    nb = seq // Q_BLOCK
    nch, n_slc = seq // CMP_STRIDE, seq // SLC_BLOCK
    gw = B_HPG * HEAD_DIM
    return pl.pallas_call(
        _nsa_cmp_kernel,
        grid=(B_KV_GROUPS, nb),
        in_specs=[pl.BlockSpec((gw, Q_BLOCK), lambda g, i: (g, i)),
                  pl.BlockSpec((1, nch, HEAD_DIM), lambda g, i: (g, 0, 0)),
                  pl.BlockSpec((1, HEAD_DIM, nch), lambda g, i: (g, 0, 0)),
                  pl.BlockSpec((LANES, Q_BLOCK), lambda g, i: (0, i)),
                  pl.BlockSpec((n_slc, nch), lambda g, i: (0, 0))],
        out_specs=[pl.BlockSpec((Q_BLOCK, gw), lambda g, i: (i, g)),
                   pl.BlockSpec((1, 1, n_slc, Q_BLOCK), lambda g, i: (g, i, 0, 0)),
                   pl.BlockSpec((1, 1, n_slc, 1), lambda g, i: (g, i, 0, 0))],
        out_shape=[jax.ShapeDtypeStruct((seq, D_MODEL), F32),
                   jax.ShapeDtypeStruct((B_KV_GROUPS, nb, n_slc, Q_BLOCK), F32),
                   jax.ShapeDtypeStruct((B_KV_GROUPS, nb, n_slc, 1), I32)],
        scratch_shapes=[pltpu.VMEM((B_HPG, nch, Q_BLOCK), F32)],
        compiler_params=_params(("arbitrary", "arbitrary")),
        name="nsa_compressed_select",
    )(qt, kv["kc"], kv["vct"], gt, _select_matrix(n_slc, nch))


def _fold8(x, op):
    return functools.reduce(op, [x[8 * k:8 * k + 8] for k in range(x.shape[0] // 8)])


def _softmax_tiles(sts, tiles, rel_bias, slopes, m_old):
    ps = [[] for _ in tiles]
    ms, ls, alphas = [], [], []
    for r in range(B_HPG):
        cols = slice(r * Q_BLOCK, (r + 1) * Q_BLOCK)
        us, shifts, tops = [], [], []
        for st, (mask, penalty, off) in zip(sts, tiles):
            u = st[:, cols] - rel_bias[r]
            u = u if mask is None else jnp.where(mask, u, NEG_INF)
            shift = slopes[r] * jnp.asarray(off, F32)
            shift = shift if penalty is None else shift + penalty
            us.append(u)
            shifts.append(shift)
            tops.append(_fold8(u, jnp.maximum) - shift)
        m_new = jnp.max(functools.reduce(jnp.maximum, tops), axis=0, keepdims=True)
        m_new = jnp.maximum(m_new, M_FLOOR)
        if m_old is not None:
            m_new = jnp.maximum(m_old[:, cols], m_new)
            alphas.append(jnp.exp2(m_old[:, cols] - m_new))
        l8 = jnp.zeros((8, Q_BLOCK), F32)
        for t, (u, shift) in enumerate(zip(us, shifts)):
            p = jnp.exp2(u - (m_new + shift))
            l8 = l8 + _fold8(p, jnp.add)
            ps[t].append(p.astype(BF16))
        ms.append(m_new)
        ls.append(jnp.sum(l8, axis=0, keepdims=True))
    ps = [jnp.concatenate(p, axis=1) for p in ps]
    alpha = None if m_old is None else jnp.concatenate(alphas, axis=1)
    return ps, jnp.concatenate(ms, axis=1), jnp.concatenate(ls, axis=1), alpha


def _nsa_sparse_kernel(lst_ref, qt_ref, ks_ref, vst_ref, kw_ref, vwt_ref, sel_ref, gt_ref, oc_ref, o_ref,
                       m_ref, l_ref, alpha_ref, acc_ref, st_ref, p_ref):
    g = pl.program_id(0)
    i = pl.program_id(1)
    t0 = i * Q_BLOCK
    slopes = _group_slopes(g)
    q4 = _q_heads(qt_ref)

    def gate(branch):
        return jnp.concatenate(
            [gt_ref[pl.ds(branch * N_HEADS + g * B_HPG + r, 1), :] for r in range(B_HPG)], axis=1)

    rel_w = (lax.broadcasted_iota(I32, (Q_BLOCK, Q_BLOCK), 1)
             - lax.broadcasted_iota(I32, (Q_BLOCK, Q_BLOCK), 0))
    n_prev = WIN // Q_BLOCK
    rel_w_f = rel_w.astype(F32)
    bias_w = [slopes[r] * rel_w_f for r in range(B_HPG)]
    tiles, blks = [], []
    for c in range(n_prev + 1):
        blk = i - n_prev + c
        off = (n_prev - c) * Q_BLOCK
        mask = (rel_w >= 0) if c == n_prev else (rel_w < 0) if c == 0 else None
        penalty = None if c == n_prev else jnp.where(blk >= 0, 0.0, MASK_PENALTY).astype(F32)
        tiles.append((mask, penalty, off))
        blks.append(jnp.maximum(blk, 0))
    sts = [jnp.dot(kw_ref[0, blk], q4, preferred_element_type=F32) for blk in blks]
    ps, _, l_w, _ = _softmax_tiles(sts, tiles, bias_w, slopes, None)
    acc_w = functools.reduce(lambda a, b: a + b, [
        jnp.dot(vwt_ref[0, blk], p, preferred_element_type=F32) for blk, p in zip(blks, ps)])
    acc_ref[1] = acc_w * (gate(2) / jnp.maximum(l_w, TINY))

    rel_s = (lax.broadcasted_iota(I32, (SLC_BLOCK, Q_BLOCK), 1)
             - lax.broadcasted_iota(I32, (SLC_BLOCK, Q_BLOCK), 0))
    rel_s_f = rel_s.astype(F32)
    bias_s = [slopes[r] * rel_s_f for r in range(B_HPG)]

    def block_id(j, b):
        return lst_ref[0, 0, j * SEL_CHUNK + b]

    def rows(b):
        return slice(b * SLC_BLOCK, (b + 1) * SLC_BLOCK)

    def scores(j, slot):
        for b in range(SEL_CHUNK):
            st_ref[slot, rows(b), :] = jnp.dot(ks_ref[0, block_id(j, b)], q4, preferred_element_type=F32)

    def weighted_values(j):
        v = jnp.concatenate([vst_ref[0, block_id(j, b)] for b in range(SEL_CHUNK)], axis=0)
        return lax.dot_general(v, p_ref[...], (((0,), (0,)), ((), ())), preferred_element_type=F32)

    m_ref[...] = jnp.full(m_ref.shape, M_FLOOR, F32)
    l_ref[...] = jnp.zeros(l_ref.shape, F32)
    alpha_ref[...] = jnp.ones(alpha_ref.shape, F32)
    acc_ref[0] = jnp.zeros(acc_ref.shape[1:], F32)
    p_ref[...] = jnp.zeros(p_ref.shape, BF16)
    scores(0, 0)

    def sel_chunk(j, c):
        slot = lax.rem(j, 2)
        acc_ref[0] = acc_ref[0] * alpha_ref[0:1, :] + weighted_values(jnp.maximum(j - 1, 0))
        tiles, sts = [], []
        for b in range(SEL_CHUNK):
            blk = block_id(j, b)
            off = t0 - blk * SLC_BLOCK
            penalty = jnp.where(sel_ref[0, 0, pl.ds(blk, 1), :] > 0.5, 0.0, MASK_PENALTY)
            mask = (rel_s >= -off) if b < 2 else None
            tiles.append((mask, penalty, off))
            sts.append(st_ref[slot, rows(b), :])
        ps, m, l, alpha = _softmax_tiles(sts, tiles, bias_s, slopes, m_ref[0:1, :])
        for b in range(SEL_CHUNK):
            p_ref[rows(b), :] = ps[b]
        m_ref[0:1, :] = m
        l_ref[0:1, :] = alpha * l_ref[0:1, :] + l
        alpha_ref[0:1, :] = alpha
        scores(j + 1, 1 - slot)
        return c

    n_chunks = lst_ref[0, 0, LIST_W - 1]
    lax.fori_loop(0, n_chunks, sel_chunk, 0)
    acc_s = acc_ref[0] * alpha_ref[0:1, :] + weighted_values(jnp.maximum(n_chunks - 1, 0))

    total = acc_ref[1] + acc_s * (gate(1) / jnp.maximum(l_ref[0:1, :], TINY))
    ot = jnp.concatenate([total[:, r * Q_BLOCK:(r + 1) * Q_BLOCK] for r in range(B_HPG)], axis=0)
    o_ref[...] = (ot.T + oc_ref[...]).astype(BF16)


def _nsa_sparse(lists, qt, gt, kv, sel, oc, seq):
    nb = seq // Q_BLOCK
    n_slc, n_win = seq // SLC_BLOCK, seq // Q_BLOCK
    gw = B_HPG * HEAD_DIM
    grp = lambda g, i: (g, 0, 0, 0)
    return pl.pallas_call(
        _nsa_sparse_kernel,
        grid=(B_KV_GROUPS, nb),
        in_specs=[pl.BlockSpec((1, 1, LIST_W), lambda g, i: (g * nb + i, 0, 0), memory_space=pltpu.SMEM),
                  pl.BlockSpec((gw, Q_BLOCK), lambda g, i: (g, i)),
                  pl.BlockSpec((1, n_slc, SLC_BLOCK, HEAD_DIM), grp),
                  pl.BlockSpec((1, n_slc, SLC_BLOCK, HEAD_DIM), grp),
                  pl.BlockSpec((1, n_win, Q_BLOCK, HEAD_DIM), grp),
                  pl.BlockSpec((1, n_win, HEAD_DIM, Q_BLOCK), grp),
                  pl.BlockSpec((1, 1, n_slc, Q_BLOCK), lambda g, i: (g, i, 0, 0)),
                  pl.BlockSpec((LANES, Q_BLOCK), lambda g, i: (0, i)),
                  pl.BlockSpec((Q_BLOCK, gw), lambda g, i: (i, g))],
        out_specs=pl.BlockSpec((Q_BLOCK, gw), lambda g, i: (i, g)),
        scratch_shapes=[pltpu.VMEM((8, B_HPG * Q_BLOCK), F32),
                        pltpu.VMEM((8, B_HPG * Q_BLOCK), F32),
                        pltpu.VMEM((8, B_HPG * Q_BLOCK), F32),
                        pltpu.VMEM((2, HEAD_DIM, B_HPG * Q_BLOCK), F32),
                        pltpu.VMEM((2, SEL_CHUNK * SLC_BLOCK, B_HPG * Q_BLOCK), F32),
                        pltpu.VMEM((SEL_CHUNK * SLC_BLOCK, B_HPG * Q_BLOCK), BF16)],
        out_shape=jax.ShapeDtypeStruct((seq, D_MODEL), BF16),
        compiler_params=_params(("arbitrary", "arbitrary")),
        name="nsa_selected_window",
    )(lists, qt, kv["ks"], kv["vst"], kv["kw"], kv["vwt"], sel, gt, oc)


def _outproj_b_kernel(o_ref, h_ref, w_ref, g_ref, b_ref, hn_ref, hb_ref):
    mix = jnp.dot(o_ref[...], w_ref[...], preferred_element_type=F32)
    _layer_norm_store(DEEPNORM_ALPHA * h_ref[...] + mix, g_ref, b_ref, hn_ref, hb_ref)


def _outproj_b(o, h, w_out, ln_g, ln_b, tm=512):
    seq = h.shape[0]
    row = lambda i: (i, 0)
    fix = lambda i: (0, 0)
    return pl.pallas_call(
        _outproj_b_kernel,
        grid=(seq // tm,),
        in_specs=[pl.BlockSpec((tm, D_MODEL), row), pl.BlockSpec((tm, D_MODEL), row),
                  pl.BlockSpec((D_MODEL, D_MODEL), fix), pl.BlockSpec((1, D_MODEL), fix),
                  pl.BlockSpec((1, D_MODEL), fix)],
        out_specs=[pl.BlockSpec((tm, D_MODEL), row)] * 2,
        out_shape=[jax.ShapeDtypeStruct((seq, D_MODEL), F32), jax.ShapeDtypeStruct((seq, D_MODEL), BF16)],
        compiler_params=_params(("arbitrary",)),
        name="outproj_ln_b",
    )(o, h, w_out.astype(BF16), ln_g.reshape(1, -1), ln_b.reshape(1, -1))


def _nsa_layer(h, hb, kv, w_q, b_q, w_out, ln_g, ln_b):
    seq = h.shape[0]
    nb, n_slc = seq // Q_BLOCK, seq // SLC_BLOCK
    qt, gt = _qproj(hb, w_q, b_q)
    oc, sel, flags = _nsa_cmp(qt, gt, kv, seq)
    f = flags.reshape(B_KV_GROUPS, nb, n_slc)
    newest = 2 * jnp.arange(nb, dtype=I32)[None, :, None] + 1
    age = jnp.mod(newest - jnp.arange(n_slc, dtype=I32)[None, None, :], n_slc)
    order = jnp.argsort((1 - f) * n_slc + age, axis=2).astype(I32).reshape(B_KV_GROUPS * nb, n_slc)
    n_chunks = (jnp.sum(f, axis=2).reshape(-1) + SEL_CHUNK - 1) // SEL_CHUNK
    lists = jnp.concatenate(
        [order, jnp.zeros((B_KV_GROUPS * nb, LIST_W - n_slc - 1), I32), n_chunks.astype(I32)[:, None]], axis=1)
    o = _nsa_sparse(lists.reshape(B_KV_GROUPS * nb, 1, LIST_W), qt, gt, kv, sel, oc, seq)
    return _outproj_b(o, h, w_out, ln_g, ln_b)


def kernel(x, a_w_in, a_w_out, b_w_kv, b_cmp_pos, b_cmp_w1, b_cmp_b1, b_cmp_w2, b_cmp_b2, b_w_q, b_b_q, b_w_out, moe_w_router, moe_b_router, moe_w_gate_up, moe_w_down, ln_mix_g, ln_mix_b, ln_ffn_g, ln_ffn_b):
    seq = x.shape[1]
    h = x.reshape(seq, D_MODEL)
    hb = h.astype(BF16)
    for layer in range(DEPTH):
        if layer < N_A_LAYERS:
            h, hb = _dilated_layer(h, hb, a_w_in[layer], a_w_out[layer], ln_mix_g[layer], ln_mix_b[layer])
        else:
            if layer == N_A_LAYERS:
                kv = _nsa_shared_kv(hb, b_w_kv, b_cmp_pos, b_cmp_w1, b_cmp_b1, b_cmp_w2, b_cmp_b2)
            j = layer - N_A_LAYERS
            h, hb = _nsa_layer(h, hb, kv, b_w_q[j], b_b_q[j], b_w_out[j], ln_mix_g[layer], ln_mix_b[layer])
        h, hb = _moe_layer(h, moe_w_router[layer], moe_b_router[layer], moe_w_gate_up[layer],
                           moe_w_down[layer], ln_ffn_g[layer], ln_ffn_b[layer])
    return h.reshape(1, seq, D_MODEL)
```

```python
import functools
import math

import numpy as np
import jax
import jax.numpy as jnp
from jax import lax
from jax.experimental import pallas as pl
from jax.experimental.pallas import tpu as pltpu

F32 = jnp.float32
BF16 = jnp.bfloat16
I32 = jnp.int32

D_MODEL = 1024
DEPTH = 4
N_A_LAYERS = DEPTH // 2
HEAD_DIM = 64
N_HEADS = D_MODEL // HEAD_DIM
Q_BLOCK = 128
LN_EPS = 1e-5
NEG_INF = -1e30
TINY = 1e-30

A_WINDOWS = (128, 512, 2048)
A_DILATIONS = (1, 4, 16)
A_N_GROUPS = 3

N_EXPERT_GROUPS = 4
EXPERTS_PER_GROUP = 4
N_EXPERTS = 16
D_EXPERT = 256

DEEPNORM_ALPHA = (2.0 * DEPTH) ** 0.25

LANES = 128
MOE_TILE = 256
VMEM_LIMIT = 56 * 1024 * 1024


def _slopes(n):
    return [float(v) for v in np.asarray(2.0 ** (-8.0 * np.arange(1, n + 1) / n), dtype=np.float32)]


def _params(sem):
    return pltpu.CompilerParams(dimension_semantics=sem, vmem_limit_bytes=VMEM_LIMIT)


def _mm_kernel(x_ref, w_ref, o_ref):
    o_ref[...] = jnp.dot(x_ref[...], w_ref[...], preferred_element_type=F32).astype(o_ref.dtype)


def _matmul(x, w, out_dtype, tm, tn, name):
    m, k = x.shape
    n = w.shape[1]
    return pl.pallas_call(
        _mm_kernel,
        grid=(n // tn, m // tm),
        in_specs=[pl.BlockSpec((tm, k), lambda j, i: (i, 0)),
                  pl.BlockSpec((k, tn), lambda j, i: (0, j))],
        out_specs=pl.BlockSpec((tm, tn), lambda j, i: (i, j)),
        out_shape=jax.ShapeDtypeStruct((m, n), out_dtype),
        compiler_params=_params(("arbitrary", "arbitrary")),
        name=name,
    )(x, w)


def _layer_norm_store(z, g_ref, b_ref, h_ref, hb_ref):
    mu = jnp.mean(z, axis=-1, keepdims=True)
    zc = z - mu
    var = jnp.mean(zc * zc, axis=-1, keepdims=True)
    y = zc * lax.rsqrt(var + LN_EPS) * g_ref[...] + b_ref[...]
    h_ref[...] = y
    hb_ref[...] = y.astype(BF16)


def _split_bf16(x):
    hi = x.astype(BF16)
    lo = (x - hi.astype(F32)).astype(BF16)
    return hi, lo


def _dil_attn_kernel(q_ref, kp_ref, kc_ref, vp_ref, vc_ref, o_ref, lse_ref, bias_ref, *, dil, slopes):
    r = pl.program_id(0)
    jb = pl.program_id(1)
    qb = Q_BLOCK

    @pl.when((r == 0) & (jb == 0))
    def _():
        row = lax.broadcasted_iota(I32, (qb, 2 * qb), 0)
        col = lax.broadcasted_iota(I32, (qb, 2 * qb), 1)
        dist = row + qb - col
        valid = (dist >= 0) & (dist <= qb)
        dist_tok = (dist * dil).astype(F32)
        for h in range(N_HEADS):
            b = jnp.where(valid, -(slopes[h] * LOG2E * dist_tok), NEG_INF)
            bias_ref[1, h] = b
            bias_ref[0, h] = jnp.where(col >= qb, b, NEG_INF)

    first = jnp.minimum(jb, 1)
    lane = lax.broadcasted_iota(I32, (1, LANES), 1)
    lo_half = lane < HEAD_DIM
    lse_acc = jnp.zeros((qb, LANES), F32)
    for hp in range(N_HEADS // 2):
        sl = slice(hp * LANES, (hp + 1) * LANES)
        q2 = q_ref[:, sl]
        k2 = jnp.concatenate([kp_ref[:, sl], kc_ref[:, sl]], axis=0)
        v2 = jnp.concatenate([vp_ref[:, sl], vc_ref[:, sl]], axis=0)
        outs = []
        for half in range(2):
            h = 2 * hp + half
            keep = lo_half if half == 0 else jnp.logical_not(lo_half)
            qm = jnp.where(keep, q2, jnp.zeros_like(q2))
            s = lax.dot_general(qm, k2, (((1,), (1,)), ((), ())), preferred_element_type=F32)
            s = s + bias_ref[first, h]
            m = jnp.max(s, axis=-1, keepdims=True)
            p = jnp.exp2(s - m)
            l = jnp.sum(p, axis=-1, keepdims=True)
            o = jnp.dot(p.astype(BF16), v2, preferred_element_type=F32)
            outs.append(o * (1.0 / l))
            lse_acc = jnp.where(lane == h, m + jnp.log2(l), lse_acc)
        o_ref[:, sl] = jnp.where(lo_half, outs[0], outs[1]).astype(BF16)
    lse_ref[...] = lse_acc


def _qkv_kernel(x_ref, w_ref, o_ref, acc_ref, *, dil):
    acc = jnp.dot(x_ref[...], w_ref[...], preferred_element_type=F32)
    acc = acc * jnp.where(pl.program_id(0) == 0, HEAD_DIM ** -0.5 * LOG2E, 1.0)
    if dil == 1:
        o_ref[0] = acc.astype(BF16)
    else:
        n_chunks = acc.shape[1] // LANES
        for c in range(n_chunks):
            acc_ref[c] = acc[:, c * LANES:(c + 1) * LANES]
        n = acc.shape[0] // dil
        for r in range(dil):
            o_ref[r] = jnp.concatenate(
                [acc_ref[c, pl.ds(r, n, stride=dil), :] for c in range(n_chunks)], axis=1).astype(BF16)


def _qkv_group(hb, w_in, layer, g, tm=1024):
    seq = hb.shape[0]
    dil = A_DILATIONS[g]
    return pl.pallas_call(
        functools.partial(_qkv_kernel, dil=dil),
        grid=(3, seq // tm),
        in_specs=[pl.BlockSpec((tm, D_MODEL), lambda c, i: (i, 0)),
                  pl.BlockSpec((None, D_MODEL, D_MODEL), lambda c, i: (layer, 0, 3 * g + c))],
        out_specs=pl.BlockSpec((dil, tm // dil, D_MODEL), lambda c, i: (0, i, c)),
        out_shape=jax.ShapeDtypeStruct((dil, seq // dil, 3 * D_MODEL), BF16),
        scratch_shapes=[pltpu.VMEM((D_MODEL // LANES, tm, LANES), F32)],
        compiler_params=_params(("arbitrary", "arbitrary")),
        name=f"qkv_proj_g{g}",
    )(hb, w_in)


def _dilated_group(qkv_g, g, seq):
    dil = A_DILATIONS[g]
    rows = seq // dil
    nblk = rows // Q_BLOCK
    blk = (None, Q_BLOCK, D_MODEL)
    kern = functools.partial(_dil_attn_kernel, dil=dil, slopes=_slopes(N_HEADS))
    return pl.pallas_call(
        kern,
        grid=(dil, nblk),
        in_specs=[
            pl.BlockSpec(blk, lambda r, j: (r, j, 0)),
            pl.BlockSpec(blk, lambda r, j: (r, jnp.maximum(j - 1, 0), 1)),
            pl.BlockSpec(blk, lambda r, j: (r, j, 1)),
            pl.BlockSpec(blk, lambda r, j: (r, jnp.maximum(j - 1, 0), 2)),
            pl.BlockSpec(blk, lambda r, j: (r, j, 2)),
        ],
        out_specs=[pl.BlockSpec(blk, lambda r, j: (r, j, 0)),
                   pl.BlockSpec((None, Q_BLOCK, LANES), lambda r, j: (r, j, 0))],
        out_shape=[jax.ShapeDtypeStruct((dil, rows, D_MODEL), BF16),
                   jax.ShapeDtypeStruct((dil, rows, LANES), F32)],
        scratch_shapes=[pltpu.VMEM((2, N_HEADS, Q_BLOCK, 2 * Q_BLOCK), F32)],
        compiler_params=_params(("arbitrary", "arbitrary")),
        name=f"dilated_attn_g{g}",
    )(qkv_g, qkv_g, qkv_g, qkv_g, qkv_g)


def _head_expand_matrix():
    h = np.arange(LANES)[:, None]
    c = np.arange(D_MODEL)[None, :]
    return jnp.asarray((c // HEAD_DIM == h).astype(np.float32), BF16)


def _to_token_order(src_ref, dst_ref, dil):
    n = src_ref.shape[1]
    n_chunks = dst_ref.shape[0]
    if dil == 1:
        return src_ref[0].astype(F32)
    for r in range(dil):
        v = src_ref[r].astype(F32)
        for c in range(n_chunks):
            dst_ref[c, pl.ds(r, n, stride=dil), :] = v[:, c * LANES:(c + 1) * LANES]
    return jnp.concatenate([dst_ref[c] for c in range(n_chunks)], axis=1)


def _outproj_a_kernel(o0_ref, o1_ref, o2_ref, l0_ref, l1_ref, l2_ref, h_ref, w_ref, e_ref, g_ref, b_ref,
                      hn_ref, hb_ref, onat_ref, lnat_ref):
    l0, l1, l2 = [_to_token_order(l_ref, lnat_ref, A_DILATIONS[gi])
                  for gi, l_ref in enumerate((l0_ref, l1_ref, l2_ref))]
    m = jnp.maximum(jnp.maximum(l0, l1), l2)
    e0, e1, e2 = jnp.exp2(l0 - m), jnp.exp2(l1 - m), jnp.exp2(l2 - m)
    inv = 1.0 / (e0 + e1 + e2)
    o = jnp.zeros(h_ref.shape, F32)
    for gi, (e, o_ref) in enumerate(((e0, o0_ref), (e1, o1_ref), (e2, o2_ref))):
        hi, lo = _split_bf16(e * inv)
        wexp = (jnp.dot(hi, e_ref[...], preferred_element_type=F32)
                + jnp.dot(lo, e_ref[...], preferred_element_type=F32))
        o = o + wexp * _to_token_order(o_ref, onat_ref, A_DILATIONS[gi])
    mix = jnp.dot(o.astype(BF16), w_ref[...], preferred_element_type=F32)
    _layer_norm_store(DEEPNORM_ALPHA * h_ref[...] + mix, g_ref, b_ref, hn_ref, hb_ref)


def _outproj_a(outs, lses, h, w_out, ln_g, ln_b, tm=512):
    seq = h.shape[0]
    row = lambda i: (i, 0)
    fix = lambda i: (0, 0)
    cls = lambda i: (0, i, 0)
    return pl.pallas_call(
        _outproj_a_kernel,
        grid=(seq // tm,),
        in_specs=[pl.BlockSpec((d, tm // d, D_MODEL), cls) for d in A_DILATIONS]
        + [pl.BlockSpec((d, tm // d, LANES), cls) for d in A_DILATIONS] + [
            pl.BlockSpec((tm, D_MODEL), row),
            pl.BlockSpec((D_MODEL, D_MODEL), fix),
            pl.BlockSpec((LANES, D_MODEL), fix),
            pl.BlockSpec((1, D_MODEL), fix),
            pl.BlockSpec((1, D_MODEL), fix)],
        out_specs=[pl.BlockSpec((tm, D_MODEL), row)] * 2,
        out_shape=[jax.ShapeDtypeStruct((seq, D_MODEL), F32), jax.ShapeDtypeStruct((seq, D_MODEL), BF16)],
        scratch_shapes=[pltpu.VMEM((D_MODEL // LANES, tm, LANES), F32), pltpu.VMEM((1, tm, LANES), F32)],
        compiler_params=_params(("arbitrary",)),
        name="outproj_ln_a",
    )(*outs, *lses, h, w_out.astype(BF16), _head_expand_matrix(), ln_g.reshape(1, -1), ln_b.reshape(1, -1))


def _dilated_layer(h, hb, w_in, layer, w_out, ln_g, ln_b):
    seq = h.shape[0]
    outs, lses = [], []
    for g in range(A_N_GROUPS):
        qkv_g = _qkv_group(hb, w_in, layer, g)
        o, lse = _dilated_group(qkv_g, g, seq)
        outs.append(o)
        lses.append(lse)
    return _outproj_a(outs, lses, h, w_out, ln_g, ln_b)


ROUTER_ROWS = 24


def _router_kernel(h_ref, w_ref, b_ref, ids_ref, wts_ref, cnt_ref, base_ref):
    i = pl.program_id(0)
    tm = h_ref.shape[0]

    @pl.when(i == 0)
    def _():
        base_ref[...] = jnp.zeros_like(base_ref)

    lt = lax.dot_general(w_ref[...], h_ref[...], (((1,), (1,)), ((), ())),
                         precision=lax.Precision.HIGHEST, preferred_element_type=F32)
    lt = lt + b_ref[:, 0:1]
    gl = [lt[k:k + 1, :] for k in range(N_EXPERT_GROUPS)]
    best, gidx = gl[0], jnp.zeros((1, tm), I32)
    for k in range(1, N_EXPERT_GROUPS):
        take = gl[k] > best
        best = jnp.where(take, gl[k], best)
        gidx = jnp.where(take, k, gidx)
    denom = gl[0] * 0.0
    for k in range(N_EXPERT_GROUPS):
        denom = denom + jnp.exp(gl[k] - best)
    g_w = 1.0 / denom
    e_in = []
    for e in range(EXPERTS_PER_GROUP):
        v = lt[N_EXPERT_GROUPS + e:N_EXPERT_GROUPS + e + 1, :]
        for k in range(1, N_EXPERT_GROUPS):
            row = N_EXPERT_GROUPS + k * EXPERTS_PER_GROUP + e
            v = jnp.where(gidx == k, lt[row:row + 1, :], v)
        e_in.append(v)
    v1, i1 = e_in[0], jnp.zeros((1, tm), I32)
    for e in range(1, EXPERTS_PER_GROUP):
        take = e_in[e] > v1
        v1 = jnp.where(take, e_in[e], v1)
        i1 = jnp.where(take, e, i1)
    v2, i2 = jnp.full((1, tm), -jnp.inf, F32), jnp.zeros((1, tm), I32)
    for e in range(EXPERTS_PER_GROUP):
        take = (e_in[e] > v2) & (i1 != e)
        v2 = jnp.where(take, e_in[e], v2)
        i2 = jnp.where(take, e, i2)
    t = jnp.exp(v2 - v1)
    w1 = g_w / (1.0 + t)
    w2 = g_w * t / (1.0 + t)
    e1 = gidx * EXPERTS_PER_GROUP + i1
    e2 = gidx * EXPERTS_PER_GROUP + i2

    erow = lax.broadcasted_iota(I32, (N_EXPERTS, tm), 0)
    oh1 = (erow == e1).astype(F32)
    oh2 = (erow == e2).astype(F32)
    c = oh1 + oh2
    tri = (lax.broadcasted_iota(I32, (tm, tm), 0) < lax.broadcasted_iota(I32, (tm, tm), 1)).astype(BF16)
    excl = jnp.dot(c.astype(BF16), tri, preferred_element_type=F32) + base_ref[...]
    r1 = jnp.sum(oh1 * excl, axis=0, keepdims=True).astype(I32)
    r2 = jnp.sum(oh2 * excl, axis=0, keepdims=True).astype(I32)
    base_ref[...] = base_ref[...] + jnp.sum(c, axis=1, keepdims=True)

    zi = jnp.zeros((1, tm), I32)
    ids_ref[...] = jnp.concatenate([e1, e2, r1, r2, zi, zi, zi, zi], axis=0)
    zf = jnp.zeros((1, tm), F32)
    wts_ref[...] = jnp.concatenate([w1, w2, zf, zf, zf, zf, zf, zf], axis=0)
    cnt_ref[...] = jnp.broadcast_to(base_ref[...], cnt_ref.shape)


def _router(h, w_router, b_router, tm=512):
    seq = h.shape[0]
    n_log = N_EXPERT_GROUPS + N_EXPERTS
    w_t = jnp.zeros((ROUTER_ROWS, D_MODEL), F32).at[:n_log].set(w_router.T)
    b_t = jnp.zeros((ROUTER_ROWS, LANES), F32).at[:n_log].set(jnp.broadcast_to(b_router[:, None], (n_log, LANES)))
    return pl.pallas_call(
        _router_kernel,
        grid=(seq // tm,),
        in_specs=[pl.BlockSpec((tm, D_MODEL), lambda i: (i, 0)),
                  pl.BlockSpec((ROUTER_ROWS, D_MODEL), lambda i: (0, 0)),
                  pl.BlockSpec((ROUTER_ROWS, LANES), lambda i: (0, 0))],
        out_specs=[pl.BlockSpec((8, tm), lambda i: (0, i)),
                   pl.BlockSpec((8, tm), lambda i: (0, i)),
                   pl.BlockSpec((N_EXPERTS, LANES), lambda i: (0, 0))],
        out_shape=[jax.ShapeDtypeStruct((8, seq), I32), jax.ShapeDtypeStruct((8, seq), F32),
                   jax.ShapeDtypeStruct((N_EXPERTS, LANES), F32)],
        scratch_shapes=[pltpu.VMEM((N_EXPERTS, 1), F32)],
        compiler_params=_params(("arbitrary",)),
        name="moe_router",
    )(h, w_t, b_t)


ROW_TILE = D_MODEL // LANES


def _rows_to_tiles(x, dst_ref):
    n = x.shape[0]
    for c in range(ROW_TILE):
        dst_ref[pl.ds(c, n, stride=ROW_TILE), :] = x[:, c * LANES:(c + 1) * LANES]


def _tiles_to_rows(src_ref):
    n = src_ref.shape[0] // ROW_TILE
    return jnp.concatenate([src_ref[pl.ds(c, n, stride=ROW_TILE), :] for c in range(ROW_TILE)], axis=1)


def _row_tile(ref, p):
    return ref.at[pl.ds(pl.multiple_of(p * ROW_TILE, ROW_TILE), ROW_TILE)]


def _dispatch_kernel(pos_ref, h_ref, xs_in_ref, xs_ref, ht_ref, sem):
    del xs_in_ref
    tm = h_ref.shape[0]
    _rows_to_tiles(h_ref[...], ht_ref)

    def row_copy(i, p):
        return pltpu.make_async_copy(_row_tile(ht_ref, i), _row_tile(xs_ref, p), sem)

    def start(i, c):
        row_copy(i, pos_ref[0, 0, i]).start(priority=0)
        row_copy(i, pos_ref[0, 0, tm + i]).start(priority=1)
        return c

    def wait(i, c):
        row_copy(0, 0).wait()
        row_copy(0, 0).wait()
        return c

    lax.fori_loop(0, tm, start, 0, unroll=8)
    lax.fori_loop(0, tm, wait, 0, unroll=8)


def _dispatch(h, pos, buf, tm=256):
    seq = h.shape[0]
    nt = seq // tm
    pos3 = pos.reshape(2, nt, tm).transpose(1, 0, 2).reshape(nt, 1, 2 * tm)
    return pl.pallas_call(
        _dispatch_kernel,
        grid=(nt,),
        in_specs=[pl.BlockSpec((1, 1, 2 * tm), lambda i: (i, 0, 0), memory_space=pltpu.SMEM),
                  pl.BlockSpec((tm, D_MODEL), lambda i: (i, 0)),
                  pl.BlockSpec(memory_space=pl.ANY)],
        out_specs=pl.BlockSpec(memory_space=pl.ANY),
        out_shape=jax.ShapeDtypeStruct(buf.shape, F32),
        scratch_shapes=[pltpu.VMEM((tm * ROW_TILE, LANES), F32), pltpu.SemaphoreType.DMA(())],
        input_output_aliases={2: 0},
        compiler_params=_params(("arbitrary",)),
        name="moe_dispatch",
    )(pos3, h, buf)


def _expert_kernel(te_ref, x_ref, wgu_ref, wd_ref, y_ref):
    del te_ref
    gu = jnp.dot(_tiles_to_rows(x_ref).astype(BF16), wgu_ref[...], preferred_element_type=F32)
    gate, up = gu[:, :D_EXPERT], gu[:, D_EXPERT:]
    hid = gate * (1.0 / (1.0 + jnp.exp(-gate))) * up
    _rows_to_tiles(jnp.dot(hid.astype(BF16), wd_ref[...], preferred_element_type=F32), y_ref)


def _experts(xs, tile_expert, w_gate_up, w_down, layer):
    blk = MOE_TILE * ROW_TILE
    return pl.pallas_call(
        _expert_kernel,
        grid_spec=pltpu.PrefetchScalarGridSpec(
            num_scalar_prefetch=1,
            grid=(xs.shape[0] // blk,),
            in_specs=[pl.BlockSpec((blk, LANES), lambda i, te: (i, 0)),
                      pl.BlockSpec((None, None, D_MODEL, 2 * D_EXPERT), lambda i, te: (layer, te[i], 0, 0)),
                      pl.BlockSpec((None, None, D_EXPERT, D_MODEL), lambda i, te: (layer, te[i], 0, 0))],
            out_specs=pl.BlockSpec((blk, LANES), lambda i, te: (i, 0))),
        out_shape=jax.ShapeDtypeStruct(xs.shape, F32),
        compiler_params=_params(("arbitrary",)),
        name="moe_experts",
    )(tile_expert, xs, w_gate_up, w_down)


def _combine_kernel(pos_ref, pos_next_ref, ys_ref, h_ref, w_ref, g_ref, b_ref, hn_ref, hb_ref, buf_ref, sem):
    step = pl.program_id(0)
    tm = h_ref.shape[0]
    slot = lax.rem(step, 2)

    def row_copy(s, k, i, p):
        return pltpu.make_async_copy(_row_tile(ys_ref, p), _row_tile(buf_ref.at[s, k], i), sem.at[s])

    def gather(p_ref, s):
        def start(i, c):
            row_copy(s, 0, i, p_ref[0, 0, i]).start(priority=0)
            row_copy(s, 1, i, p_ref[0, 0, tm + i]).start(priority=1)
            return c
        lax.fori_loop(0, tm, start, 0, unroll=8)

    @pl.when(step == 0)
    def _():
        gather(pos_ref, 0)

    @pl.when(step + 1 < pl.num_programs(0))
    def _():
        gather(pos_next_ref, 1 - slot)

    def wait(i, c):
        row_copy(slot, 0, 0, 0).wait()
        row_copy(slot, 1, 0, 0).wait()
        return c

    lax.fori_loop(0, tm, wait, 0, unroll=8)
    w = w_ref[...]
    ffn = (w[:, 0:1] * _tiles_to_rows(buf_ref.at[slot, 0])
           + w[:, 1:2] * _tiles_to_rows(buf_ref.at[slot, 1]))
    _layer_norm_store(DEEPNORM_ALPHA * h_ref[...] + ffn, g_ref, b_ref, hn_ref, hb_ref)


def _combine(ys, pos, wts, h, ln_g, ln_b, tm=256):
    seq = h.shape[0]
    nt = seq // tm
    pos3 = pos.reshape(2, nt, tm).transpose(1, 0, 2).reshape(nt, 1, 2 * tm)
    row = lambda i: (i, 0)
    fix = lambda i: (0, 0)
    return pl.pallas_call(
        _combine_kernel,
        grid=(nt,),
        in_specs=[pl.BlockSpec((1, 1, 2 * tm), lambda i: (i, 0, 0), memory_space=pltpu.SMEM),
                  pl.BlockSpec((1, 1, 2 * tm), lambda i: (jnp.minimum(i + 1, nt - 1), 0, 0),
                               memory_space=pltpu.SMEM),
                  pl.BlockSpec(memory_space=pl.ANY),
                  pl.BlockSpec((tm, D_MODEL), row),
                  pl.BlockSpec((tm, 2), row),
                  pl.BlockSpec((1, D_MODEL), fix),
                  pl.BlockSpec((1, D_MODEL), fix)],
        out_specs=[pl.BlockSpec((tm, D_MODEL), row)] * 2,
        out_shape=[jax.ShapeDtypeStruct((seq, D_MODEL), F32), jax.ShapeDtypeStruct((seq, D_MODEL), BF16)],
        scratch_shapes=[pltpu.VMEM((2, 2, tm * ROW_TILE, LANES), F32), pltpu.SemaphoreType.DMA((2,))],
        compiler_params=_params(("arbitrary",)),
        name="moe_combine_ln",
    )(pos3, pos3, ys, h, wts, ln_g.reshape(1, -1), ln_b.reshape(1, -1))


def _moe_layer(h, w_router, b_router, w_gate_up, w_down, layer, ln_g, ln_b, buf):
    seq = h.shape[0]
    ids, wts, cnt = _router(h, w_router, b_router)
    counts = cnt[:, 0].astype(I32)
    padded = ((counts + MOE_TILE - 1) // MOE_TILE) * MOE_TILE
    ends = jnp.cumsum(padded)
    offs = ends - padded
    eids = jnp.arange(N_EXPERTS, dtype=I32)[:, None, None]
    pos = jnp.sum(jnp.where(ids[None, 0:2] == eids, offs[:, None, None], 0), axis=0) + ids[2:4]
    n_rows = 2 * seq + N_EXPERTS * MOE_TILE
    tile_start = jnp.arange(n_rows // MOE_TILE, dtype=I32) * MOE_TILE
    tile_expert = jnp.minimum(jnp.sum((tile_start[:, None] >= ends[None, :]).astype(I32), axis=1), N_EXPERTS - 1)
    if buf is None:
        buf = jnp.zeros((n_rows * ROW_TILE, LANES), F32)
    xs = _dispatch(h, pos, buf)
    ys = _experts(xs, tile_expert.astype(I32), w_gate_up, w_down, layer)
    return _combine(ys, pos, wts[0:2].T, h, ln_g, ln_b) + (ys,)


B_KV_GROUPS = 4
B_HPG = N_HEADS // B_KV_GROUPS
CMP_LEN = 32
CMP_STRIDE = 16
CMP_HIDDEN = 256
SLC_BLOCK = 64
N_SELECT = 16
WIN = 512
FORCE_SCORE = 1e4
N_FORCED = 3
LIST_W = 272
LOG2E = math.log2(math.e)
MASK_PENALTY = 1e30
M_FLOOR = -1e29
CMP_PARTS = 4
SEL_CHUNK = 4


def _group_slopes(g):
    sl = [v * LOG2E for v in _slopes(N_HEADS)]
    out = []
    for r in range(B_HPG):
        v = jnp.full((1, 1), sl[r], F32)
        for k in range(1, B_KV_GROUPS):
            v = jnp.where(g == k, sl[k * B_HPG + r], v)
        out.append(v)
    return out


def _qproj_kernel(hb_ref, w_ref, b_ref, qt_ref, gt_ref):
    pt = lax.dot_general(w_ref[...], hb_ref[...], (((1,), (1,)), ((), ())), preferred_element_type=F32)
    pt = pt + b_ref[:, 0:1]
    qt_ref[...] = (pt[:D_MODEL] * (HEAD_DIM ** -0.5 * LOG2E)).astype(BF16)
    gt_ref[...] = 1.0 / (1.0 + jnp.exp(-pt[D_MODEL:]))


def _qproj(hb, w_q, b_q, tm=512):
    seq = hb.shape[0]
    n = w_q.shape[1]
    npad = D_MODEL + LANES
    w_t = jnp.zeros((npad, D_MODEL), BF16).at[:n].set(w_q.T.astype(BF16))
    b_t = jnp.zeros((npad, LANES), F32).at[:n].set(jnp.broadcast_to(b_q[:, None], (n, LANES)))
    return pl.pallas_call(
        _qproj_kernel,
        grid=(seq // tm,),
        in_specs=[pl.BlockSpec((tm, D_MODEL), lambda i: (i, 0)),
                  pl.BlockSpec((npad, D_MODEL), lambda i: (0, 0)),
                  pl.BlockSpec((npad, LANES), lambda i: (0, 0))],
        out_specs=[pl.BlockSpec((D_MODEL, tm), lambda i: (0, i)),
                   pl.BlockSpec((LANES, tm), lambda i: (0, i))],
        out_shape=[jax.ShapeDtypeStruct((D_MODEL, seq), BF16), jax.ShapeDtypeStruct((LANES, seq), F32)],
        compiler_params=_params(("arbitrary",)),
        name="nsa_qproj",
    )(hb, w_t, b_t)


def _compress_kernel(x_ref, w1a_ref, w1b_ref, pa_ref, pb_ref, b1_ref, w2_ref, b2_ref, o_ref):
    x = x_ref[0, 0]
    n = x.shape[0]
    a = jnp.dot((x + pa_ref[0]).astype(BF16), w1a_ref[0], preferred_element_type=F32)
    b = jnp.dot((x + pb_ref[0]).astype(BF16), w1b_ref[0], preferred_element_type=F32)
    b = pltpu.roll(b, n - 1, 0)
    u = a + b + b1_ref[0]
    hid = 0.5 * u * (1.0 + jnp.tanh(math.sqrt(2.0 / math.pi) * (u + 0.044715 * (u * u * u))))
    o_ref[0, 0] = (jnp.dot(hid.astype(BF16), w2_ref[0], preferred_element_type=F32) + b2_ref[0]).astype(BF16)


def _compress(kvc, cmp_pos, cmp_w1, cmp_b1, cmp_w2, cmp_b2):
    seq = kvc.shape[0]
    nch = seq // CMP_STRIDE
    half = CMP_STRIDE * HEAD_DIM
    chunks = kvc.reshape(nch, CMP_STRIDE, 2, B_KV_GROUPS, HEAD_DIM).transpose(2, 3, 0, 1, 4).reshape(
        2, B_KV_GROUPS, nch, half)
    pos = cmp_pos.reshape(2, 1, CMP_LEN * HEAD_DIM)
    w1 = cmp_w1.astype(BF16)
    jfix = lambda j, g: (j, 0, 0)
    return pl.pallas_call(
        _compress_kernel,
        grid=(2, B_KV_GROUPS),
        in_specs=[pl.BlockSpec((1, 1, nch, half), lambda j, g: (j, g, 0, 0)),
                  pl.BlockSpec((1, half, CMP_HIDDEN), jfix),
                  pl.BlockSpec((1, half, CMP_HIDDEN), lambda j, g: (j, 1, 0)),
                  pl.BlockSpec((1, 1, half), jfix),
                  pl.BlockSpec((1, 1, half), lambda j, g: (j, 0, 1)),
                  pl.BlockSpec((1, 1, CMP_HIDDEN), jfix),
                  pl.BlockSpec((1, CMP_HIDDEN, HEAD_DIM), jfix),
                  pl.BlockSpec((1, 1, HEAD_DIM), jfix)],
        out_specs=pl.BlockSpec((1, 1, nch, HEAD_DIM), lambda j, g: (j, g, 0, 0)),
        out_shape=jax.ShapeDtypeStruct((2, B_KV_GROUPS, nch, HEAD_DIM), BF16),
        compiler_params=_params(("arbitrary", "arbitrary")),
        name="nsa_compress",
    )(chunks, w1, w1, pos, pos, cmp_b1.reshape(2, 1, CMP_HIDDEN), cmp_w2.astype(BF16),
      cmp_b2.reshape(2, 1, HEAD_DIM))


def _nsa_shared_kv(hb, w_kv, cmp_pos, cmp_w1, cmp_b1, cmp_w2, cmp_b2):
    seq = hb.shape[0]
    gw = B_KV_GROUPS * HEAD_DIM
    w = w_kv.astype(BF16)
    kvc = _matmul(hb, w[:, :2 * gw], F32, 512, 2 * gw, "kv_proj_cmp")
    kvr = _matmul(hb, w[:, 2 * gw:], BF16, 512, 4 * gw, "kv_proj_rest")
    cmp = _compress(kvc, cmp_pos, cmp_w1, cmp_b1, cmp_w2, cmp_b2)
    n_slc, n_win = seq // SLC_BLOCK, seq // Q_BLOCK

    def blocks(cols, nblk, blk, transpose):
        t = cols.reshape(nblk, blk, B_KV_GROUPS, HEAD_DIM)
        return t.transpose(2, 0, 3, 1) if transpose else t.transpose(2, 0, 1, 3)

    return dict(
        kc=cmp[0], vct=cmp[1].transpose(0, 2, 1),
        ks=blocks(kvr[:, 0 * gw:1 * gw], n_slc, SLC_BLOCK, False),
        vst=blocks(kvr[:, 1 * gw:2 * gw], n_slc, SLC_BLOCK, False),
        kw=blocks(kvr[:, 2 * gw:3 * gw], n_win, Q_BLOCK, False),
        vwt=blocks(kvr[:, 3 * gw:4 * gw], n_win, Q_BLOCK, True))


def _select_matrix(n_slc, nch):
    m = np.zeros((n_slc, nch), np.float32)
    r_s, r_c = SLC_BLOCK // CMP_STRIDE, CMP_LEN // CMP_STRIDE
    for a in range(r_s):
        for b in range(r_c):
            idx = r_s * np.arange(n_slc) + a - b
            ok = (idx >= 0) & (idx < nch - r_c + 1)
            m[np.arange(n_slc)[ok], idx[ok]] += 1.0
    return jnp.asarray(m, BF16)


def _q_heads(qt_ref):
    return jnp.concatenate([qt_ref[r * HEAD_DIM:(r + 1) * HEAD_DIM, :] for r in range(B_HPG)], axis=1)


def _nsa_cmp_kernel(qt_ref, kc_ref, vct_ref, gt_ref, mt_ref, oc_ref, sel_ref, flag_ref, bias_ref):
    g = pl.program_id(0)
    i = pl.program_id(1)
    nch = kc_ref.shape[1]
    n_slc = mt_ref.shape[0]
    t0 = i * Q_BLOCK

    @pl.when(i == 0)
    def _():
        slopes = _group_slopes(g)
        rel = (lax.broadcasted_iota(I32, (nch, Q_BLOCK), 1)
               - (CMP_STRIDE * lax.broadcasted_iota(I32, (nch, Q_BLOCK), 0) + (CMP_LEN - 1))).astype(F32)
        for r in range(B_HPG):
            bias_ref[r] = slopes[r] * rel

    def compressed_branch(n):
        cidx = lax.broadcasted_iota(I32, (n, Q_BLOCK), 0)
        qidx = lax.broadcasted_iota(I32, (n, Q_BLOCK), 1)
        valid = (t0 + qidx) - (CMP_STRIDE * cidx + (CMP_LEN - 1)) >= 0
        st = jnp.dot(kc_ref[0, :n, :], _q_heads(qt_ref), preferred_element_type=F32)
        imp = jnp.zeros((n, Q_BLOCK), F32)
        outs = []
        for r in range(B_HPG):
            s = st[:, r * Q_BLOCK:(r + 1) * Q_BLOCK] - bias_ref[r, :n, :]
            s = jnp.where(valid, s, NEG_INF)
            m = jnp.maximum(jnp.max(s, axis=0, keepdims=True), M_FLOOR)
            p = jnp.exp2(s - m)
            l = jnp.sum(p, axis=0, keepdims=True)
            pn = p * (1.0 / jnp.maximum(l, TINY))
            imp = imp + pn
            o = jnp.dot(vct_ref[0, :, :n], pn.astype(BF16), preferred_element_type=F32)
            outs.append(o * gt_ref[pl.ds(g * B_HPG + r, 1), :])
        oc_ref[...] = jnp.concatenate(outs, axis=0).T
        hi = imp.astype(BF16)
        r1 = imp - hi.astype(F32)
        mid = r1.astype(BF16)
        lo = (r1 - mid.astype(F32)).astype(BF16)
        ns = n * n_slc // nch
        mt = mt_ref[:ns, :n]
        p_slc = (jnp.dot(mt, hi, preferred_element_type=F32) + jnp.dot(mt, mid, preferred_element_type=F32)
                 + jnp.dot(mt, lo, preferred_element_type=F32))
        sid = lax.broadcasted_iota(I32, (ns, Q_BLOCK), 0)
        cur = (t0 + lax.broadcasted_iota(I32, (ns, Q_BLOCK), 1)) // SLC_BLOCK
        forced = (sid == 0) | (sid == cur) | (sid == cur - 1)
        score = jnp.where(forced, -jnp.inf, p_slc)
        score = jnp.where(sid <= cur, score, -FORCE_SCORE)
        for _ in range(N_SELECT - N_FORCED):
            mx = jnp.max(score, axis=0, keepdims=True)
            first = jnp.min(jnp.where(score == mx, sid, ns), axis=0, keepdims=True)
            score = jnp.where(sid == first, -jnp.inf, score)
        sel = jnp.where((score == -jnp.inf) & (sid <= cur), 1.0, 0.0)
        sel_ref[0, 0, :ns, :] = sel
        flag_ref[0, 0, :ns, :] = jnp.max(sel, axis=1, keepdims=True).astype(I32)
        if ns < n_slc:
            sel_ref[0, 0, ns:, :] = jnp.zeros((n_slc - ns, Q_BLOCK), F32)
            flag_ref[0, 0, ns:, :] = jnp.zeros((n_slc - ns, 1), I32)

    part = nch // CMP_PARTS
    assert part * n_slc // nch >= N_SELECT
    lax.switch(lax.div(8 * i + 6, part),
               [functools.partial(compressed_branch, part * (k + 1)) for k in range(CMP_PARTS)])


def _nsa_cmp(qt, gt, kv, seq):
    nb = seq // Q_BLOCK
    nch, n_slc = seq // CMP_STRIDE, seq // SLC_BLOCK
    gw = B_HPG * HEAD_DIM
    return pl.pallas_call(
        _nsa_cmp_kernel,
        grid=(B_KV_GROUPS, nb),
        in_specs=[pl.BlockSpec((gw, Q_BLOCK), lambda g, i: (g, i)),
                  pl.BlockSpec((1, nch, HEAD_DIM), lambda g, i: (g, 0, 0)),
                  pl.BlockSpec((1, HEAD_DIM, nch), lambda g, i: (g, 0, 0)),
                  pl.BlockSpec((LANES, Q_BLOCK), lambda g, i: (0, i)),
                  pl.BlockSpec((n_slc, nch), lambda g, i: (0, 0))],
        out_specs=[pl.BlockSpec((Q_BLOCK, gw), lambda g, i: (i, g)),
                   pl.BlockSpec((1, 1, n_slc, Q_BLOCK), lambda g, i: (g, i, 0, 0)),
                   pl.BlockSpec((1, 1, n_slc, 1), lambda g, i: (g, i, 0, 0))],
        out_shape=[jax.ShapeDtypeStruct((seq, D_MODEL), F32),
                   jax.ShapeDtypeStruct((B_KV_GROUPS, nb, n_slc, Q_BLOCK), F32),
                   jax.ShapeDtypeStruct((B_KV_GROUPS, nb, n_slc, 1), I32)],
        scratch_shapes=[pltpu.VMEM((B_HPG, nch, Q_BLOCK), F32)],
        compiler_params=_params(("arbitrary", "arbitrary")),
        name="nsa_compressed_select",
    )(qt, kv["kc"], kv["vct"], gt, _select_matrix(n_slc, nch))


def _fold8(x, op):
    return functools.reduce(op, [x[8 * k:8 * k + 8] for k in range(x.shape[0] // 8)])


def _softmax_tiles(sts, tiles, rel_bias, slopes, m_old):
    ps = [[] for _ in tiles]
    ms, ls, alphas = [], [], []
    for r in range(B_HPG):
        cols = slice(r * Q_BLOCK, (r + 1) * Q_BLOCK)
        us, shifts, tops = [], [], []
        for st, (mask, penalty, off) in zip(sts, tiles):
            u = st[:, cols] - rel_bias[r]
            u = u if mask is None else jnp.where(mask, u, NEG_INF)
            shift = slopes[r] * jnp.asarray(off, F32)
            shift = shift if penalty is None else shift + penalty
            us.append(u)
            shifts.append(shift)
            tops.append(_fold8(u, jnp.maximum) - shift)
        m_new = jnp.max(functools.reduce(jnp.maximum, tops), axis=0, keepdims=True)
        m_new = jnp.maximum(m_new, M_FLOOR)
        if m_old is not None:
            m_new = jnp.maximum(m_old[:, cols], m_new)
            alphas.append(jnp.exp2(m_old[:, cols] - m_new))
        l8 = jnp.zeros((8, Q_BLOCK), F32)
        for t, (u, shift) in enumerate(zip(us, shifts)):
            p = jnp.exp2(u - (m_new + shift))
            l8 = l8 + _fold8(p, jnp.add)
            ps[t].append(p.astype(BF16))
        ms.append(m_new)
        ls.append(jnp.sum(l8, axis=0, keepdims=True))
    ps = [jnp.concatenate(p, axis=1) for p in ps]
    alpha = None if m_old is None else jnp.concatenate(alphas, axis=1)
    return ps, jnp.concatenate(ms, axis=1), jnp.concatenate(ls, axis=1), alpha


def _nsa_sparse_kernel(lst_ref, qt_ref, ks_ref, vst_ref, kw_ref, vwt_ref, sel_ref, gt_ref, oc_ref, o_ref,
                       m_ref, alpha_ref, l_ref, accw_ref, acc_ref, st_ref, p_ref):
    g = pl.program_id(0)
    i = pl.program_id(1)
    t0 = i * Q_BLOCK
    slopes = _group_slopes(g)
    q4 = _q_heads(qt_ref)

    def gate(branch):
        return jnp.concatenate(
            [gt_ref[pl.ds(branch * N_HEADS + g * B_HPG + r, 1), :] for r in range(B_HPG)], axis=1)

    rel_w = (lax.broadcasted_iota(I32, (Q_BLOCK, Q_BLOCK), 1)
             - lax.broadcasted_iota(I32, (Q_BLOCK, Q_BLOCK), 0))
    n_prev = WIN // Q_BLOCK
    rel_w_f = rel_w.astype(F32)
    bias_w = [slopes[r] * rel_w_f for r in range(B_HPG)]
    tiles, blks = [], []
    for c in range(n_prev + 1):
        blk = i - n_prev + c
        off = (n_prev - c) * Q_BLOCK
        mask = (rel_w >= 0) if c == n_prev else (rel_w < 0) if c == 0 else None
        penalty = None if c == n_prev else jnp.where(blk >= 0, 0.0, MASK_PENALTY).astype(F32)
        tiles.append((mask, penalty, off))
        blks.append(jnp.maximum(blk, 0))
    sts = [jnp.dot(kw_ref[0, blk], q4, preferred_element_type=F32) for blk in blks]
    ps, _, l_w, _ = _softmax_tiles(sts, tiles, bias_w, slopes, None)
    acc_w = functools.reduce(lambda a, b: a + b, [
        jnp.dot(vwt_ref[0, blk], p, preferred_element_type=F32) for blk, p in zip(blks, ps)])
    accw_ref[...] = acc_w * (gate(2) / jnp.maximum(l_w, TINY))

    rel_s = (lax.broadcasted_iota(I32, (SLC_BLOCK, Q_BLOCK), 1)
             - lax.broadcasted_iota(I32, (SLC_BLOCK, Q_BLOCK), 0))
    rel_s_f = rel_s.astype(F32)
    bias_s = [slopes[r] * rel_s_f for r in range(B_HPG)]

    def block_id(j, b):
        return lst_ref[0, 0, j * SEL_CHUNK + b]

    def rows(b):
        return slice(b * SLC_BLOCK, (b + 1) * SLC_BLOCK)

    def scores(j, slot):
        for b in range(SEL_CHUNK):
            st_ref[slot, rows(b), :] = jnp.dot(ks_ref[0, block_id(j, b)], q4, preferred_element_type=F32)

    def weighted_values(j):
        v = jnp.concatenate([vst_ref[0, block_id(j, b)] for b in range(SEL_CHUNK)], axis=0)
        return lax.dot_general(v, p_ref[...], (((0,), (0,)), ((), ())), preferred_element_type=F32)

    m_ref[...] = jnp.full(m_ref.shape, M_FLOOR, F32)
    alpha_ref[...] = jnp.ones(alpha_ref.shape, F32)
    l_ref[...] = jnp.zeros(l_ref.shape, F32)
    acc_ref[...] = jnp.zeros(acc_ref.shape, F32)
    p_ref[...] = jnp.zeros(p_ref.shape, BF16)
    scores(0, 0)

    def sel_chunk(j, c):
        slot = lax.rem(j, 2)
        acc_ref[...] = acc_ref[...] * alpha_ref[0:1, :] + weighted_values(jnp.maximum(j - 1, 0))
        tiles, sts = [], []
        for b in range(SEL_CHUNK):
            blk = block_id(j, b)
            off = t0 - blk * SLC_BLOCK
            penalty = jnp.where(sel_ref[0, 0, pl.ds(blk, 1), :] > 0.5, 0.0, MASK_PENALTY)
            mask = (rel_s >= -off) if b < 2 else None
            tiles.append((mask, penalty, off))
            sts.append(st_ref[slot, rows(b), :])
        ps, m, l, alpha = _softmax_tiles(sts, tiles, bias_s, slopes, m_ref[0:1, :])
        for b in range(SEL_CHUNK):
            p_ref[rows(b), :] = ps[b]
        m_ref[0:1, :] = m
        l_ref[0:1, :] = alpha * l_ref[0:1, :] + l
        alpha_ref[0:1, :] = alpha
        scores(j + 1, 1 - slot)
        return c

    n_chunks = lst_ref[0, 0, LIST_W - 1]
    lax.fori_loop(0, n_chunks, sel_chunk, 0)
    acc_s = acc_ref[...] * alpha_ref[0:1, :] + weighted_values(jnp.maximum(n_chunks - 1, 0))

    total = accw_ref[...] + acc_s * (gate(1) / jnp.maximum(l_ref[0:1, :], TINY))
    ot = jnp.concatenate([total[:, r * Q_BLOCK:(r + 1) * Q_BLOCK] for r in range(B_HPG)], axis=0)
    o_ref[...] = (ot.T + oc_ref[...]).astype(BF16)


def _nsa_sparse(lists, qt, gt, kv, sel, oc, seq):
    nb = seq // Q_BLOCK
    n_slc, n_win = seq // SLC_BLOCK, seq // Q_BLOCK
    gw = B_HPG * HEAD_DIM
    grp = lambda g, i: (g, 0, 0, 0)
    return pl.pallas_call(
        _nsa_sparse_kernel,
        grid=(B_KV_GROUPS, nb),
        in_specs=[pl.BlockSpec((1, 1, LIST_W), lambda g, i: (g * nb + i, 0, 0), memory_space=pltpu.SMEM),
                  pl.BlockSpec((gw, Q_BLOCK), lambda g, i: (g, i)),
                  pl.BlockSpec((1, n_slc, SLC_BLOCK, HEAD_DIM), grp),
                  pl.BlockSpec((1, n_slc, SLC_BLOCK, HEAD_DIM), grp),
                  pl.BlockSpec((1, n_win, Q_BLOCK, HEAD_DIM), grp),
                  pl.BlockSpec((1, n_win, HEAD_DIM, Q_BLOCK), grp),
                  pl.BlockSpec((1, 1, n_slc, Q_BLOCK), lambda g, i: (g, i, 0, 0)),
                  pl.BlockSpec((LANES, Q_BLOCK), lambda g, i: (0, i)),
                  pl.BlockSpec((Q_BLOCK, gw), lambda g, i: (i, g))],
        out_specs=pl.BlockSpec((Q_BLOCK, gw), lambda g, i: (i, g)),
        scratch_shapes=[pltpu.VMEM((8, B_HPG * Q_BLOCK), F32),
                        pltpu.VMEM((8, B_HPG * Q_BLOCK), F32),
                        pltpu.VMEM((8, B_HPG * Q_BLOCK), F32),
                        pltpu.VMEM((HEAD_DIM, B_HPG * Q_BLOCK), F32),
                        pltpu.VMEM((HEAD_DIM, B_HPG * Q_BLOCK), F32),
                        pltpu.VMEM((2, SEL_CHUNK * SLC_BLOCK, B_HPG * Q_BLOCK), F32),
                        pltpu.VMEM((SEL_CHUNK * SLC_BLOCK, B_HPG * Q_BLOCK), BF16)],
        out_shape=jax.ShapeDtypeStruct((seq, D_MODEL), BF16),
        compiler_params=_params(("arbitrary", "arbitrary")),
        name="nsa_selected_window",
    )(lists, qt, kv["ks"], kv["vst"], kv["kw"], kv["vwt"], sel, gt, oc)


def _outproj_b_kernel(o_ref, h_ref, w_ref, g_ref, b_ref, hn_ref, hb_ref):
    mix = jnp.dot(o_ref[...], w_ref[...], preferred_element_type=F32)
    _layer_norm_store(DEEPNORM_ALPHA * h_ref[...] + mix, g_ref, b_ref, hn_ref, hb_ref)


def _outproj_b(o, h, w_out, ln_g, ln_b, tm=512):
    seq = h.shape[0]
    row = lambda i: (i, 0)
    fix = lambda i: (0, 0)
    return pl.pallas_call(
        _outproj_b_kernel,
        grid=(seq // tm,),
        in_specs=[pl.BlockSpec((tm, D_MODEL), row), pl.BlockSpec((tm, D_MODEL), row),
                  pl.BlockSpec((D_MODEL, D_MODEL), fix), pl.BlockSpec((1, D_MODEL), fix),
                  pl.BlockSpec((1, D_MODEL), fix)],
        out_specs=[pl.BlockSpec((tm, D_MODEL), row)] * 2,
        out_shape=[jax.ShapeDtypeStruct((seq, D_MODEL), F32), jax.ShapeDtypeStruct((seq, D_MODEL), BF16)],
        compiler_params=_params(("arbitrary",)),
        name="outproj_ln_b",
    )(o, h, w_out.astype(BF16), ln_g.reshape(1, -1), ln_b.reshape(1, -1))


def _nsa_layer(h, hb, kv, w_q, b_q, w_out, ln_g, ln_b):
    seq = h.shape[0]
    nb, n_slc = seq // Q_BLOCK, seq // SLC_BLOCK
    qt, gt = _qproj(hb, w_q, b_q)
    oc, sel, flags = _nsa_cmp(qt, gt, kv, seq)
    f = flags.reshape(B_KV_GROUPS, nb, n_slc)
    newest = 2 * jnp.arange(nb, dtype=I32)[None, :, None] + 1
    age = jnp.mod(newest - jnp.arange(n_slc, dtype=I32)[None, None, :], n_slc)
    order = jnp.argsort((1 - f) * n_slc + age, axis=2).astype(I32).reshape(B_KV_GROUPS * nb, n_slc)
    n_chunks = (jnp.sum(f, axis=2).reshape(-1) + SEL_CHUNK - 1) // SEL_CHUNK
    lists = jnp.concatenate(
        [order, jnp.zeros((B_KV_GROUPS * nb, LIST_W - n_slc - 1), I32), n_chunks.astype(I32)[:, None]], axis=1)
    o = _nsa_sparse(lists.reshape(B_KV_GROUPS * nb, 1, LIST_W), qt, gt, kv, sel, oc, seq)
    return _outproj_b(o, h, w_out, ln_g, ln_b)


def kernel(x, a_w_in, a_w_out, b_w_kv, b_cmp_pos, b_cmp_w1, b_cmp_b1, b_cmp_w2, b_cmp_b2, b_w_q, b_b_q, b_w_out, moe_w_router, moe_b_router, moe_w_gate_up, moe_w_down, ln_mix_g, ln_mix_b, ln_ffn_g, ln_ffn_b):
    seq = x.shape[1]
    h = x.reshape(seq, D_MODEL)
    hb = h.astype(BF16)
    a_w_in_b, w_gate_up_b, w_down_b = a_w_in.astype(BF16), moe_w_gate_up.astype(BF16), moe_w_down.astype(BF16)
    buf = None
    for layer in range(DEPTH):
        if layer < N_A_LAYERS:
            h, hb = _dilated_layer(h, hb, a_w_in_b, layer, a_w_out[layer], ln_mix_g[layer], ln_mix_b[layer])
        else:
            if layer == N_A_LAYERS:
                kv = _nsa_shared_kv(hb, b_w_kv, b_cmp_pos, b_cmp_w1, b_cmp_b1, b_cmp_w2, b_cmp_b2)
            j = layer - N_A_LAYERS
            h, hb = _nsa_layer(h, hb, kv, b_w_q[j], b_b_q[j], b_w_out[j], ln_mix_g[layer], ln_mix_b[layer])
        h, hb, buf = _moe_layer(h, moe_w_router[layer], moe_b_router[layer], w_gate_up_b, w_down_b, layer,
                                ln_ffn_g[layer], ln_ffn_b[layer], buf)
    return h.reshape(1, seq, D_MODEL)
```

```python
import functools
import math

import numpy as np
import jax
import jax.numpy as jnp
from jax import lax
from jax.experimental import pallas as pl
from jax.experimental.pallas import tpu as pltpu

F32 = jnp.float32
BF16 = jnp.bfloat16
I32 = jnp.int32

D_MODEL = 1024
DEPTH = 4
N_A_LAYERS = DEPTH // 2
HEAD_DIM = 64
N_HEADS = D_MODEL // HEAD_DIM
Q_BLOCK = 128
LN_EPS = 1e-5
NEG_INF = -1e30
TINY = 1e-30

A_WINDOWS = (128, 512, 2048)
A_DILATIONS = (1, 4, 16)
A_N_GROUPS = 3

N_EXPERT_GROUPS = 4
EXPERTS_PER_GROUP = 4
N_EXPERTS = 16
D_EXPERT = 256

DEEPNORM_ALPHA = (2.0 * DEPTH) ** 0.25

LANES = 128
MOE_TILE = 256
VMEM_LIMIT = 56 * 1024 * 1024


def _slopes(n):
    return [float(v) for v in np.asarray(2.0 ** (-8.0 * np.arange(1, n + 1) / n), dtype=np.float32)]


def _params(sem):
    return pltpu.CompilerParams(dimension_semantics=sem, vmem_limit_bytes=VMEM_LIMIT)


def _mm_kernel(x_ref, w_ref, o_ref):
    o_ref[...] = jnp.dot(x_ref[...], w_ref[...], preferred_element_type=F32).astype(o_ref.dtype)


def _matmul(x, w, out_dtype, tm, tn, name):
    m, k = x.shape
    n = w.shape[1]
    return pl.pallas_call(
        _mm_kernel,
        grid=(n // tn, m // tm),
        in_specs=[pl.BlockSpec((tm, k), lambda j, i: (i, 0)),
                  pl.BlockSpec((k, tn), lambda j, i: (0, j))],
        out_specs=pl.BlockSpec((tm, tn), lambda j, i: (i, j)),
        out_shape=jax.ShapeDtypeStruct((m, n), out_dtype),
        compiler_params=_params(("arbitrary", "arbitrary")),
        name=name,
    )(x, w)


def _layer_norm_store(z, g_ref, b_ref, h_ref, hb_ref):
    mu = jnp.mean(z, axis=-1, keepdims=True)
    zc = z - mu
    var = jnp.mean(zc * zc, axis=-1, keepdims=True)
    y = zc * lax.rsqrt(var + LN_EPS) * g_ref[...] + b_ref[...]
    h_ref[...] = y
    hb_ref[...] = y.astype(BF16)


def _split_bf16(x):
    hi = x.astype(BF16)
    lo = (x - hi.astype(F32)).astype(BF16)
    return hi, lo


def _dil_attn_kernel(q_ref, kp_ref, kc_ref, vp_ref, vc_ref, o_ref, lse_ref, bias_ref, *, dil, slopes):
    r = pl.program_id(0)
    jb = pl.program_id(1)
    qb = Q_BLOCK

    @pl.when((r == 0) & (jb == 0))
    def _():
        key = lax.broadcasted_iota(I32, (2 * qb, qb), 0)
        qry = lax.broadcasted_iota(I32, (2 * qb, qb), 1)
        dist = qry + qb - key
        valid = (dist >= 0) & (dist <= qb)
        dist_tok = (dist * dil).astype(F32)
        for h in range(N_HEADS):
            b = jnp.where(valid, -(slopes[h] * LOG2E * dist_tok), NEG_INF)
            bias_ref[1, h] = b
            bias_ref[0, h] = jnp.where(key >= qb, b, NEG_INF)

    first = jnp.minimum(jb, 1)
    lane = lax.broadcasted_iota(I32, (1, LANES), 1)
    lo_half = lane < HEAD_DIM
    lo_rows = lax.broadcasted_iota(I32, (LANES, qb), 0) < HEAD_DIM
    lses = []
    for hp in range(N_HEADS // 2):
        sl = slice(hp * LANES, (hp + 1) * LANES)
        q2 = q_ref[:, sl]
        k2 = jnp.concatenate([kp_ref[:, sl], kc_ref[:, sl]], axis=0)
        v2t = jnp.concatenate([vp_ref[:, sl], vc_ref[:, sl]], axis=0).T
        outs = []
        for half in range(2):
            keep = lo_half if half == 0 else jnp.logical_not(lo_half)
            qm = jnp.where(keep, q2, jnp.zeros_like(q2))
            st = lax.dot_general(k2, qm, (((1,), (1,)), ((), ())), preferred_element_type=F32)
            st = st + bias_ref[first, 2 * hp + half]
            m = jnp.max(st, axis=0, keepdims=True)
            p = jnp.exp2(st - m)
            l = jnp.sum(p, axis=0, keepdims=True)
            ot = jnp.dot(v2t, p.astype(BF16), preferred_element_type=F32)
            outs.append(ot * (1.0 / l))
            lses.append(m + jnp.log2(l))
        o_ref[:, sl] = jnp.where(lo_rows, outs[0], outs[1]).T.astype(BF16)
    lse_t = jnp.concatenate(lses + [jnp.zeros((LANES - N_HEADS, qb), F32)], axis=0)
    lse_ref[...] = lse_t.T


def _qkv_kernel(x_ref, w_ref, o_ref, acc_ref, *, dil):
    acc = jnp.dot(x_ref[...], w_ref[...], preferred_element_type=F32)
    acc = acc * jnp.where(pl.program_id(0) == 0, HEAD_DIM ** -0.5 * LOG2E, 1.0)
    if dil == 1:
        o_ref[0] = acc.astype(BF16)
    else:
        n_chunks = acc.shape[1] // LANES
        for c in range(n_chunks):
            acc_ref[c] = acc[:, c * LANES:(c + 1) * LANES]
        n = acc.shape[0] // dil
        for r in range(dil):
            o_ref[r] = jnp.concatenate(
                [acc_ref[c, pl.ds(r, n, stride=dil), :] for c in range(n_chunks)], axis=1).astype(BF16)


def _qkv_group(hb, w_in, layer, g, tm=1024):
    seq = hb.shape[0]
    dil = A_DILATIONS[g]
    return pl.pallas_call(
        functools.partial(_qkv_kernel, dil=dil),
        grid=(3, seq // tm),
        in_specs=[pl.BlockSpec((tm, D_MODEL), lambda c, i: (i, 0)),
                  pl.BlockSpec((None, D_MODEL, D_MODEL), lambda c, i: (layer, 0, 3 * g + c))],
        out_specs=pl.BlockSpec((dil, tm // dil, D_MODEL), lambda c, i: (0, i, c)),
        out_shape=jax.ShapeDtypeStruct((dil, seq // dil, 3 * D_MODEL), BF16),
        scratch_shapes=[pltpu.VMEM((D_MODEL // LANES, tm, LANES), F32)],
        compiler_params=_params(("arbitrary", "arbitrary")),
        name=f"qkv_proj_g{g}",
    )(hb, w_in)


def _dilated_group(qkv_g, g, seq):
    dil = A_DILATIONS[g]
    rows = seq // dil
    nblk = rows // Q_BLOCK
    blk = (None, Q_BLOCK, D_MODEL)
    kern = functools.partial(_dil_attn_kernel, dil=dil, slopes=_slopes(N_HEADS))
    return pl.pallas_call(
        kern,
        grid=(dil, nblk),
        in_specs=[
            pl.BlockSpec(blk, lambda r, j: (r, j, 0)),
            pl.BlockSpec(blk, lambda r, j: (r, jnp.maximum(j - 1, 0), 1)),
            pl.BlockSpec(blk, lambda r, j: (r, j, 1)),
            pl.BlockSpec(blk, lambda r, j: (r, jnp.maximum(j - 1, 0), 2)),
            pl.BlockSpec(blk, lambda r, j: (r, j, 2)),
        ],
        out_specs=[pl.BlockSpec(blk, lambda r, j: (r, j, 0)),
                   pl.BlockSpec((None, Q_BLOCK, LANES), lambda r, j: (r, j, 0))],
        out_shape=[jax.ShapeDtypeStruct((dil, rows, D_MODEL), BF16),
                   jax.ShapeDtypeStruct((dil, rows, LANES), F32)],
        scratch_shapes=[pltpu.VMEM((2, N_HEADS, 2 * Q_BLOCK, Q_BLOCK), F32)],
        compiler_params=_params(("arbitrary", "arbitrary")),
        name=f"dilated_attn_g{g}",
    )(qkv_g, qkv_g, qkv_g, qkv_g, qkv_g)


def _head_expand_matrix():
    h = np.arange(LANES)[:, None]
    c = np.arange(D_MODEL)[None, :]
    return jnp.asarray((c // HEAD_DIM == h).astype(np.float32), BF16)


def _to_token_order(src_ref, dst_ref, dil):
    n = src_ref.shape[1]
    n_chunks = dst_ref.shape[0]
    if dil == 1:
        return src_ref[0].astype(F32)
    for r in range(dil):
        v = src_ref[r].astype(F32)
        for c in range(n_chunks):
            dst_ref[c, pl.ds(r, n, stride=dil), :] = v[:, c * LANES:(c + 1) * LANES]
    return jnp.concatenate([dst_ref[c] for c in range(n_chunks)], axis=1)


def _outproj_a_kernel(o0_ref, o1_ref, o2_ref, l0_ref, l1_ref, l2_ref, h_ref, w_ref, e_ref, g_ref, b_ref,
                      hn_ref, hb_ref, onat_ref, lnat_ref):
    l0, l1, l2 = [_to_token_order(l_ref, lnat_ref, A_DILATIONS[gi])
                  for gi, l_ref in enumerate((l0_ref, l1_ref, l2_ref))]
    m = jnp.maximum(jnp.maximum(l0, l1), l2)
    e0, e1, e2 = jnp.exp2(l0 - m), jnp.exp2(l1 - m), jnp.exp2(l2 - m)
    inv = 1.0 / (e0 + e1 + e2)
    o = jnp.zeros(h_ref.shape, F32)
    for gi, (e, o_ref) in enumerate(((e0, o0_ref), (e1, o1_ref), (e2, o2_ref))):
        hi, lo = _split_bf16(e * inv)
        wexp = (jnp.dot(hi, e_ref[...], preferred_element_type=F32)
                + jnp.dot(lo, e_ref[...], preferred_element_type=F32))
        o = o + wexp * _to_token_order(o_ref, onat_ref, A_DILATIONS[gi])
    mix = jnp.dot(o.astype(BF16), w_ref[...], preferred_element_type=F32)
    _layer_norm_store(DEEPNORM_ALPHA * h_ref[...] + mix, g_ref, b_ref, hn_ref, hb_ref)


def _outproj_a(outs, lses, h, w_out, ln_g, ln_b, tm=512):
    seq = h.shape[0]
    row = lambda i: (i, 0)
    fix = lambda i: (0, 0)
    cls = lambda i: (0, i, 0)
    return pl.pallas_call(
        _outproj_a_kernel,
        grid=(seq // tm,),
        in_specs=[pl.BlockSpec((d, tm // d, D_MODEL), cls) for d in A_DILATIONS]
        + [pl.BlockSpec((d, tm // d, LANES), cls) for d in A_DILATIONS] + [
            pl.BlockSpec((tm, D_MODEL), row),
            pl.BlockSpec((D_MODEL, D_MODEL), fix),
            pl.BlockSpec((LANES, D_MODEL), fix),
            pl.BlockSpec((1, D_MODEL), fix),
            pl.BlockSpec((1, D_MODEL), fix)],
        out_specs=[pl.BlockSpec((tm, D_MODEL), row)] * 2,
        out_shape=[jax.ShapeDtypeStruct((seq, D_MODEL), F32), jax.ShapeDtypeStruct((seq, D_MODEL), BF16)],
        scratch_shapes=[pltpu.VMEM((D_MODEL // LANES, tm, LANES), F32), pltpu.VMEM((1, tm, LANES), F32)],
        compiler_params=_params(("arbitrary",)),
        name="outproj_ln_a",
    )(*outs, *lses, h, w_out.astype(BF16), _head_expand_matrix(), ln_g.reshape(1, -1), ln_b.reshape(1, -1))


def _dilated_layer(h, hb, w_in, layer, w_out, ln_g, ln_b):
    seq = h.shape[0]
    outs, lses = [], []
    for g in range(A_N_GROUPS):
        qkv_g = _qkv_group(hb, w_in, layer, g)
        o, lse = _dilated_group(qkv_g, g, seq)
        outs.append(o)
        lses.append(lse)
    return _outproj_a(outs, lses, h, w_out, ln_g, ln_b)


ROUTER_ROWS = 24


def _router_kernel(h_ref, w_ref, b_ref, ids_ref, wts_ref, cnt_ref, base_ref):
    i = pl.program_id(0)
    tm = h_ref.shape[0]

    @pl.when(i == 0)
    def _():
        base_ref[...] = jnp.zeros_like(base_ref)

    lt = lax.dot_general(w_ref[...], h_ref[...], (((1,), (1,)), ((), ())),
                         precision=lax.Precision.HIGHEST, preferred_element_type=F32)
    lt = lt + b_ref[:, 0:1]
    gl = [lt[k:k + 1, :] for k in range(N_EXPERT_GROUPS)]
    best, gidx = gl[0], jnp.zeros((1, tm), I32)
    for k in range(1, N_EXPERT_GROUPS):
        take = gl[k] > best
        best = jnp.where(take, gl[k], best)
        gidx = jnp.where(take, k, gidx)
    denom = gl[0] * 0.0
    for k in range(N_EXPERT_GROUPS):
        denom = denom + jnp.exp(gl[k] - best)
    g_w = 1.0 / denom
    e_in = []
    for e in range(EXPERTS_PER_GROUP):
        v = lt[N_EXPERT_GROUPS + e:N_EXPERT_GROUPS + e + 1, :]
        for k in range(1, N_EXPERT_GROUPS):
            row = N_EXPERT_GROUPS + k * EXPERTS_PER_GROUP + e
            v = jnp.where(gidx == k, lt[row:row + 1, :], v)
        e_in.append(v)
    v1, i1 = e_in[0], jnp.zeros((1, tm), I32)
    for e in range(1, EXPERTS_PER_GROUP):
        take = e_in[e] > v1
        v1 = jnp.where(take, e_in[e], v1)
        i1 = jnp.where(take, e, i1)
    v2, i2 = jnp.full((1, tm), -jnp.inf, F32), jnp.zeros((1, tm), I32)
    for e in range(EXPERTS_PER_GROUP):
        take = (e_in[e] > v2) & (i1 != e)
        v2 = jnp.where(take, e_in[e], v2)
        i2 = jnp.where(take, e, i2)
    t = jnp.exp(v2 - v1)
    w1 = g_w / (1.0 + t)
    w2 = g_w * t / (1.0 + t)
    e1 = gidx * EXPERTS_PER_GROUP + i1
    e2 = gidx * EXPERTS_PER_GROUP + i2

    erow = lax.broadcasted_iota(I32, (N_EXPERTS, tm), 0)
    oh1 = (erow == e1).astype(F32)
    oh2 = (erow == e2).astype(F32)
    c = oh1 + oh2
    tri = (lax.broadcasted_iota(I32, (tm, tm), 0) < lax.broadcasted_iota(I32, (tm, tm), 1)).astype(BF16)
    excl = jnp.dot(c.astype(BF16), tri, preferred_element_type=F32) + base_ref[...]
    r1 = jnp.sum(oh1 * excl, axis=0, keepdims=True).astype(I32)
    r2 = jnp.sum(oh2 * excl, axis=0, keepdims=True).astype(I32)
    base_ref[...] = base_ref[...] + jnp.sum(c, axis=1, keepdims=True)

    zi = jnp.zeros((1, tm), I32)
    ids_ref[...] = jnp.concatenate([e1, e2, r1, r2, zi, zi, zi, zi], axis=0)
    zf = jnp.zeros((1, tm), F32)
    wts_ref[...] = jnp.concatenate([w1, w2, zf, zf, zf, zf, zf, zf], axis=0)
    cnt_ref[...] = jnp.broadcast_to(base_ref[...], cnt_ref.shape)


def _router(h, w_router, b_router, tm=512):
    seq = h.shape[0]
    n_log = N_EXPERT_GROUPS + N_EXPERTS
    w_t = jnp.zeros((ROUTER_ROWS, D_MODEL), F32).at[:n_log].set(w_router.T)
    b_t = jnp.zeros((ROUTER_ROWS, LANES), F32).at[:n_log].set(jnp.broadcast_to(b_router[:, None], (n_log, LANES)))
    return pl.pallas_call(
        _router_kernel,
        grid=(seq // tm,),
        in_specs=[pl.BlockSpec((tm, D_MODEL), lambda i: (i, 0)),
                  pl.BlockSpec((ROUTER_ROWS, D_MODEL), lambda i: (0, 0)),
                  pl.BlockSpec((ROUTER_ROWS, LANES), lambda i: (0, 0))],
        out_specs=[pl.BlockSpec((8, tm), lambda i: (0, i)),
                   pl.BlockSpec((8, tm), lambda i: (0, i)),
                   pl.BlockSpec((N_EXPERTS, LANES), lambda i: (0, 0))],
        out_shape=[jax.ShapeDtypeStruct((8, seq), I32), jax.ShapeDtypeStruct((8, seq), F32),
                   jax.ShapeDtypeStruct((N_EXPERTS, LANES), F32)],
        scratch_shapes=[pltpu.VMEM((N_EXPERTS, 1), F32)],
        compiler_params=_params(("arbitrary",)),
        name="moe_router",
    )(h, w_t, b_t)


ROW_TILE = D_MODEL // LANES


def _rows_to_tiles(x, dst_ref):
    n = x.shape[0]
    for c in range(ROW_TILE):
        dst_ref[pl.ds(c, n, stride=ROW_TILE), :] = x[:, c * LANES:(c + 1) * LANES]


def _tiles_to_rows(src_ref):
    n = src_ref.shape[0] // ROW_TILE
    return jnp.concatenate([src_ref[pl.ds(c, n, stride=ROW_TILE), :] for c in range(ROW_TILE)], axis=1)


def _row_tile(ref, p):
    return ref.at[pl.ds(pl.multiple_of(p * ROW_TILE, ROW_TILE), ROW_TILE)]


def _dispatch_kernel(pos_ref, h_ref, xs_in_ref, xs_ref, ht_ref, sem):
    del xs_in_ref
    tm = h_ref.shape[0]
    _rows_to_tiles(h_ref[...], ht_ref)

    def row_copy(i, p):
        return pltpu.make_async_copy(_row_tile(ht_ref, i), _row_tile(xs_ref, p), sem)

    def start(i, c):
        row_copy(i, pos_ref[0, 0, i]).start(priority=0)
        row_copy(i, pos_ref[0, 0, tm + i]).start(priority=1)
        return c

    def wait(i, c):
        row_copy(0, 0).wait()
        row_copy(0, 0).wait()
        return c

    lax.fori_loop(0, tm, start, 0, unroll=8)
    lax.fori_loop(0, tm, wait, 0, unroll=8)


def _dispatch(h, pos, buf, tm=256):
    seq = h.shape[0]
    nt = seq // tm
    pos3 = pos.reshape(2, nt, tm).transpose(1, 0, 2).reshape(nt, 1, 2 * tm)
    return pl.pallas_call(
        _dispatch_kernel,
        grid=(nt,),
        in_specs=[pl.BlockSpec((1, 1, 2 * tm), lambda i: (i, 0, 0), memory_space=pltpu.SMEM),
                  pl.BlockSpec((tm, D_MODEL), lambda i: (i, 0)),
                  pl.BlockSpec(memory_space=pl.ANY)],
        out_specs=pl.BlockSpec(memory_space=pl.ANY),
        out_shape=jax.ShapeDtypeStruct(buf.shape, F32),
        scratch_shapes=[pltpu.VMEM((tm * ROW_TILE, LANES), F32), pltpu.SemaphoreType.DMA(())],
        input_output_aliases={2: 0},
        compiler_params=_params(("arbitrary",)),
        name="moe_dispatch",
    )(pos3, h, buf)


def _expert_kernel(te_ref, x_ref, wgu_ref, wd_ref, y_ref):
    del te_ref
    gu = jnp.dot(_tiles_to_rows(x_ref).astype(BF16), wgu_ref[...], preferred_element_type=F32)
    gate, up = gu[:, :D_EXPERT], gu[:, D_EXPERT:]
    hid = gate * (1.0 / (1.0 + jnp.exp(-gate))) * up
    _rows_to_tiles(jnp.dot(hid.astype(BF16), wd_ref[...], preferred_element_type=F32), y_ref)


def _experts(xs, tile_expert, w_gate_up, w_down, layer):
    blk = MOE_TILE * ROW_TILE
    return pl.pallas_call(
        _expert_kernel,
        grid_spec=pltpu.PrefetchScalarGridSpec(
            num_scalar_prefetch=1,
            grid=(xs.shape[0] // blk,),
            in_specs=[pl.BlockSpec((blk, LANES), lambda i, te: (i, 0)),
                      pl.BlockSpec((None, None, D_MODEL, 2 * D_EXPERT), lambda i, te: (layer, te[i], 0, 0)),
                      pl.BlockSpec((None, None, D_EXPERT, D_MODEL), lambda i, te: (layer, te[i], 0, 0))],
            out_specs=pl.BlockSpec((blk, LANES), lambda i, te: (i, 0))),
        out_shape=jax.ShapeDtypeStruct(xs.shape, F32),
        compiler_params=_params(("arbitrary",)),
        name="moe_experts",
    )(tile_expert, xs, w_gate_up, w_down)


def _combine_kernel(pos_ref, pos_next_ref, ys_ref, h_ref, w_ref, g_ref, b_ref, hn_ref, hb_ref, buf_ref, sem):
    step = pl.program_id(0)
    tm = h_ref.shape[0]
    slot = lax.rem(step, 2)

    def row_copy(s, k, i, p):
        return pltpu.make_async_copy(_row_tile(ys_ref, p), _row_tile(buf_ref.at[s, k], i), sem.at[s])

    def gather(p_ref, s):
        def start(i, c):
            row_copy(s, 0, i, p_ref[0, 0, i]).start(priority=0)
            row_copy(s, 1, i, p_ref[0, 0, tm + i]).start(priority=1)
            return c
        lax.fori_loop(0, tm, start, 0, unroll=8)

    @pl.when(step == 0)
    def _():
        gather(pos_ref, 0)

    @pl.when(step + 1 < pl.num_programs(0))
    def _():
        gather(pos_next_ref, 1 - slot)

    def wait(i, c):
        row_copy(slot, 0, 0, 0).wait()
        row_copy(slot, 1, 0, 0).wait()
        return c

    lax.fori_loop(0, tm, wait, 0, unroll=8)
    w = w_ref[...]
    ffn = (w[:, 0:1] * _tiles_to_rows(buf_ref.at[slot, 0])
           + w[:, 1:2] * _tiles_to_rows(buf_ref.at[slot, 1]))
    _layer_norm_store(DEEPNORM_ALPHA * h_ref[...] + ffn, g_ref, b_ref, hn_ref, hb_ref)


def _combine(ys, pos, wts, h, ln_g, ln_b, tm=256):
    seq = h.shape[0]
    nt = seq // tm
    pos3 = pos.reshape(2, nt, tm).transpose(1, 0, 2).reshape(nt, 1, 2 * tm)
    row = lambda i: (i, 0)
    fix = lambda i: (0, 0)
    return pl.pallas_call(
        _combine_kernel,
        grid=(nt,),
        in_specs=[pl.BlockSpec((1, 1, 2 * tm), lambda i: (i, 0, 0), memory_space=pltpu.SMEM),
                  pl.BlockSpec((1, 1, 2 * tm), lambda i: (jnp.minimum(i + 1, nt - 1), 0, 0),
                               memory_space=pltpu.SMEM),
                  pl.BlockSpec(memory_space=pl.ANY),
                  pl.BlockSpec((tm, D_MODEL), row),
                  pl.BlockSpec((tm, 2), row),
                  pl.BlockSpec((1, D_MODEL), fix),
                  pl.BlockSpec((1, D_MODEL), fix)],
        out_specs=[pl.BlockSpec((tm, D_MODEL), row)] * 2,
        out_shape=[jax.ShapeDtypeStruct((seq, D_MODEL), F32), jax.ShapeDtypeStruct((seq, D_MODEL), BF16)],
        scratch_shapes=[pltpu.VMEM((2, 2, tm * ROW_TILE, LANES), F32), pltpu.SemaphoreType.DMA((2,))],
        compiler_params=_params(("arbitrary",)),
        name="moe_combine_ln",
    )(pos3, pos3, ys, h, wts, ln_g.reshape(1, -1), ln_b.reshape(1, -1))


def _moe_layer(h, w_router, b_router, w_gate_up, w_down, layer, ln_g, ln_b, buf):
    seq = h.shape[0]
    ids, wts, cnt = _router(h, w_router, b_router)
    counts = cnt[:, 0].astype(I32)
    padded = ((counts + MOE_TILE - 1) // MOE_TILE) * MOE_TILE
    ends = jnp.cumsum(padded)
    offs = ends - padded
    eids = jnp.arange(N_EXPERTS, dtype=I32)[:, None, None]
    pos = jnp.sum(jnp.where(ids[None, 0:2] == eids, offs[:, None, None], 0), axis=0) + ids[2:4]
    n_rows = 2 * seq + N_EXPERTS * MOE_TILE
    tile_start = jnp.arange(n_rows // MOE_TILE, dtype=I32) * MOE_TILE
    tile_expert = jnp.minimum(jnp.sum((tile_start[:, None] >= ends[None, :]).astype(I32), axis=1), N_EXPERTS - 1)
    if buf is None:
        buf = jnp.zeros((n_rows * ROW_TILE, LANES), F32)
    xs = _dispatch(h, pos, buf)
    ys = _experts(xs, tile_expert.astype(I32), w_gate_up, w_down, layer)
    return _combine(ys, pos, wts[0:2].T, h, ln_g, ln_b) + (ys,)


B_KV_GROUPS = 4
B_HPG = N_HEADS // B_KV_GROUPS
CMP_LEN = 32
CMP_STRIDE = 16
CMP_HIDDEN = 256
SLC_BLOCK = 64
N_SELECT = 16
WIN = 512
FORCE_SCORE = 1e4
N_FORCED = 3
LIST_W = 272
LOG2E = math.log2(math.e)
MASK_PENALTY = 1e30
M_FLOOR = -1e29
CMP_PARTS = 8
SEL_CHUNK = 4


def _group_slopes(g):
    sl = [v * LOG2E for v in _slopes(N_HEADS)]
    out = []
    for r in range(B_HPG):
        v = jnp.full((1, 1), sl[r], F32)
        for k in range(1, B_KV_GROUPS):
            v = jnp.where(g == k, sl[k * B_HPG + r], v)
        out.append(v)
    return out


def _qproj_kernel(hb_ref, w_ref, b_ref, qt_ref, gt_ref):
    pt = lax.dot_general(w_ref[...], hb_ref[...], (((1,), (1,)), ((), ())), preferred_element_type=F32)
    pt = pt + b_ref[:, 0:1]
    qt_ref[...] = (pt[:D_MODEL] * (HEAD_DIM ** -0.5 * LOG2E)).astype(BF16)
    gt_ref[...] = 1.0 / (1.0 + jnp.exp(-pt[D_MODEL:]))


def _qproj(hb, w_q, b_q, tm=512):
    seq = hb.shape[0]
    n = w_q.shape[1]
    npad = D_MODEL + LANES
    w_t = jnp.zeros((npad, D_MODEL), BF16).at[:n].set(w_q.T.astype(BF16))
    b_t = jnp.zeros((npad, LANES), F32).at[:n].set(jnp.broadcast_to(b_q[:, None], (n, LANES)))
    return pl.pallas_call(
        _qproj_kernel,
        grid=(seq // tm,),
        in_specs=[pl.BlockSpec((tm, D_MODEL), lambda i: (i, 0)),
                  pl.BlockSpec((npad, D_MODEL), lambda i: (0, 0)),
                  pl.BlockSpec((npad, LANES), lambda i: (0, 0))],
        out_specs=[pl.BlockSpec((D_MODEL, tm), lambda i: (0, i)),
                   pl.BlockSpec((LANES, tm), lambda i: (0, i))],
        out_shape=[jax.ShapeDtypeStruct((D_MODEL, seq), BF16), jax.ShapeDtypeStruct((LANES, seq), F32)],
        compiler_params=_params(("arbitrary",)),
        name="nsa_qproj",
    )(hb, w_t, b_t)


def _compress_kernel(x_ref, w1a_ref, w1b_ref, pa_ref, pb_ref, b1_ref, w2_ref, b2_ref, o_ref):
    x = x_ref[0, 0]
    n = x.shape[0]
    a = jnp.dot((x + pa_ref[0]).astype(BF16), w1a_ref[0], preferred_element_type=F32)
    b = jnp.dot((x + pb_ref[0]).astype(BF16), w1b_ref[0], preferred_element_type=F32)
    b = pltpu.roll(b, n - 1, 0)
    u = a + b + b1_ref[0]
    hid = 0.5 * u * (1.0 + jnp.tanh(math.sqrt(2.0 / math.pi) * (u + 0.044715 * (u * u * u))))
    o_ref[0, 0] = (jnp.dot(hid.astype(BF16), w2_ref[0], preferred_element_type=F32) + b2_ref[0]).astype(BF16)


def _compress(kvc, cmp_pos, cmp_w1, cmp_b1, cmp_w2, cmp_b2):
    seq = kvc.shape[0]
    nch = seq // CMP_STRIDE
    half = CMP_STRIDE * HEAD_DIM
    chunks = kvc.reshape(nch, CMP_STRIDE, 2, B_KV_GROUPS, HEAD_DIM).transpose(2, 3, 0, 1, 4).reshape(
        2, B_KV_GROUPS, nch, half)
    pos = cmp_pos.reshape(2, 1, CMP_LEN * HEAD_DIM)
    w1 = cmp_w1.astype(BF16)
    jfix = lambda j, g: (j, 0, 0)
    return pl.pallas_call(
        _compress_kernel,
        grid=(2, B_KV_GROUPS),
        in_specs=[pl.BlockSpec((1, 1, nch, half), lambda j, g: (j, g, 0, 0)),
                  pl.BlockSpec((1, half, CMP_HIDDEN), jfix),
                  pl.BlockSpec((1, half, CMP_HIDDEN), lambda j, g: (j, 1, 0)),
                  pl.BlockSpec((1, 1, half), jfix),
                  pl.BlockSpec((1, 1, half), lambda j, g: (j, 0, 1)),
                  pl.BlockSpec((1, 1, CMP_HIDDEN), jfix),
                  pl.BlockSpec((1, CMP_HIDDEN, HEAD_DIM), jfix),
                  pl.BlockSpec((1, 1, HEAD_DIM), jfix)],
        out_specs=pl.BlockSpec((1, 1, nch, HEAD_DIM), lambda j, g: (j, g, 0, 0)),
        out_shape=jax.ShapeDtypeStruct((2, B_KV_GROUPS, nch, HEAD_DIM), BF16),
        compiler_params=_params(("arbitrary", "arbitrary")),
        name="nsa_compress",
    )(chunks, w1, w1, pos, pos, cmp_b1.reshape(2, 1, CMP_HIDDEN), cmp_w2.astype(BF16),
      cmp_b2.reshape(2, 1, HEAD_DIM))


def _nsa_shared_kv(hb, w_kv, cmp_pos, cmp_w1, cmp_b1, cmp_w2, cmp_b2):
    seq = hb.shape[0]
    gw = B_KV_GROUPS * HEAD_DIM
    w = w_kv.astype(BF16)
    kvc = _matmul(hb, w[:, :2 * gw], F32, 512, 2 * gw, "kv_proj_cmp")
    kvr = _matmul(hb, w[:, 2 * gw:], BF16, 512, 4 * gw, "kv_proj_rest")
    cmp = _compress(kvc, cmp_pos, cmp_w1, cmp_b1, cmp_w2, cmp_b2)
    n_slc, n_win = seq // SLC_BLOCK, seq // Q_BLOCK

    def blocks(cols, nblk, blk, transpose):
        t = cols.reshape(nblk, blk, B_KV_GROUPS, HEAD_DIM)
        return t.transpose(2, 0, 3, 1) if transpose else t.transpose(2, 0, 1, 3)

    return dict(
        kc=cmp[0], vct=cmp[1].transpose(0, 2, 1),
        ks=blocks(kvr[:, 0 * gw:1 * gw], n_slc, SLC_BLOCK, False),
        vst=blocks(kvr[:, 1 * gw:2 * gw], n_slc, SLC_BLOCK, False),
        kw=blocks(kvr[:, 2 * gw:3 * gw], n_win, Q_BLOCK, False),
        vwt=blocks(kvr[:, 3 * gw:4 * gw], n_win, Q_BLOCK, True))


def _select_matrix(n_slc, nch):
    m = np.zeros((n_slc, nch), np.float32)
    r_s, r_c = SLC_BLOCK // CMP_STRIDE, CMP_LEN // CMP_STRIDE
    for a in range(r_s):
        for b in range(r_c):
            idx = r_s * np.arange(n_slc) + a - b
            ok = (idx >= 0) & (idx < nch - r_c + 1)
            m[np.arange(n_slc)[ok], idx[ok]] += 1.0
    return jnp.asarray(m, BF16)


def _q_heads(qt_ref):
    return jnp.concatenate([qt_ref[r * HEAD_DIM:(r + 1) * HEAD_DIM, :] for r in range(B_HPG)], axis=1)


def _nsa_cmp_kernel(qt_ref, kc_ref, vct_ref, gt_ref, mt_ref, oc_ref, sel_ref, flag_ref, bias_ref):
    g = pl.program_id(0)
    i = pl.program_id(1)
    nch = kc_ref.shape[1]
    n_slc = mt_ref.shape[0]
    t0 = i * Q_BLOCK

    @pl.when(i == 0)
    def _():
        slopes = _group_slopes(g)
        rel = (lax.broadcasted_iota(I32, (nch, Q_BLOCK), 1)
               - (CMP_STRIDE * lax.broadcasted_iota(I32, (nch, Q_BLOCK), 0) + (CMP_LEN - 1))).astype(F32)
        for r in range(B_HPG):
            bias_ref[r] = slopes[r] * rel

    def compressed_branch(n):
        cidx = lax.broadcasted_iota(I32, (n, Q_BLOCK), 0)
        qidx = lax.broadcasted_iota(I32, (n, Q_BLOCK), 1)
        valid = (t0 + qidx) - (CMP_STRIDE * cidx + (CMP_LEN - 1)) >= 0
        st = jnp.dot(kc_ref[0, :n, :], _q_heads(qt_ref), preferred_element_type=F32)
        imp = jnp.zeros((n, Q_BLOCK), F32)
        outs = []
        for r in range(B_HPG):
            s = st[:, r * Q_BLOCK:(r + 1) * Q_BLOCK] - bias_ref[r, :n, :]
            s = jnp.where(valid, s, NEG_INF)
            m = jnp.maximum(jnp.max(s, axis=0, keepdims=True), M_FLOOR)
            p = jnp.exp2(s - m)
            l = jnp.sum(p, axis=0, keepdims=True)
            pn = p * (1.0 / jnp.maximum(l, TINY))
            imp = imp + pn
            o = jnp.dot(vct_ref[0, :, :n], pn.astype(BF16), preferred_element_type=F32)
            outs.append(o * gt_ref[pl.ds(g * B_HPG + r, 1), :])
        oc_ref[...] = jnp.concatenate(outs, axis=0).T
        hi = imp.astype(BF16)
        r1 = imp - hi.astype(F32)
        mid = r1.astype(BF16)
        lo = (r1 - mid.astype(F32)).astype(BF16)
        ns = n * n_slc // nch
        mt = mt_ref[:ns, :n]
        p_slc = (jnp.dot(mt, hi, preferred_element_type=F32) + jnp.dot(mt, mid, preferred_element_type=F32)
                 + jnp.dot(mt, lo, preferred_element_type=F32))
        sid = lax.broadcasted_iota(I32, (ns, Q_BLOCK), 0)
        cur = (t0 + lax.broadcasted_iota(I32, (ns, Q_BLOCK), 1)) // SLC_BLOCK
        forced = (sid == 0) | (sid == cur) | (sid == cur - 1)
        score = jnp.where(forced, -jnp.inf, p_slc)
        score = jnp.where(sid <= cur, score, -FORCE_SCORE)
        for _ in range(N_SELECT - N_FORCED):
            mx = jnp.max(score, axis=0, keepdims=True)
            first = jnp.min(jnp.where(score == mx, sid, ns), axis=0, keepdims=True)
            score = jnp.where(sid == first, -jnp.inf, score)
        sel = jnp.where((score == -jnp.inf) & (sid <= cur), 1.0, 0.0)
        sel_ref[0, 0, :ns, :] = sel
        flag_ref[0, 0, :ns, :] = jnp.max(sel, axis=1, keepdims=True).astype(I32)
        if ns < n_slc:
            sel_ref[0, 0, ns:, :] = jnp.zeros((n_slc - ns, Q_BLOCK), F32)
            flag_ref[0, 0, ns:, :] = jnp.zeros((n_slc - ns, 1), I32)

    n_parts = min(CMP_PARTS, n_slc // N_SELECT)
    part = nch // n_parts
    lax.switch(lax.div(8 * i + 6, part),
               [functools.partial(compressed_branch, part * (k + 1)) for k in range(n_parts)])


def _nsa_cmp(qt, gt, kv, seq):
    nb = seq // Q_BLOCK
    nch, n_slc = seq // CMP_STRIDE, seq // SLC_BLOCK
    gw = B_HPG * HEAD_DIM
    return pl.pallas_call(
        _nsa_cmp_kernel,
        grid=(B_KV_GROUPS, nb),
        in_specs=[pl.BlockSpec((gw, Q_BLOCK), lambda g, i: (g, i)),
                  pl.BlockSpec((1, nch, HEAD_DIM), lambda g, i: (g, 0, 0)),
                  pl.BlockSpec((1, HEAD_DIM, nch), lambda g, i: (g, 0, 0)),
                  pl.BlockSpec((LANES, Q_BLOCK), lambda g, i: (0, i)),
                  pl.BlockSpec((n_slc, nch), lambda g, i: (0, 0))],
        out_specs=[pl.BlockSpec((Q_BLOCK, gw), lambda g, i: (i, g)),
                   pl.BlockSpec((1, 1, n_slc, Q_BLOCK), lambda g, i: (g, i, 0, 0)),
                   pl.BlockSpec((1, 1, n_slc, 1), lambda g, i: (g, i, 0, 0))],
        out_shape=[jax.ShapeDtypeStruct((seq, D_MODEL), F32),
                   jax.ShapeDtypeStruct((B_KV_GROUPS, nb, n_slc, Q_BLOCK), F32),
                   jax.ShapeDtypeStruct((B_KV_GROUPS, nb, n_slc, 1), I32)],
        scratch_shapes=[pltpu.VMEM((B_HPG, nch, Q_BLOCK), F32)],
        compiler_params=_params(("arbitrary", "arbitrary")),
        name="nsa_compressed_select",
    )(qt, kv["kc"], kv["vct"], gt, _select_matrix(n_slc, nch))


def _fold8(x, op):
    return functools.reduce(op, [x[8 * k:8 * k + 8] for k in range(x.shape[0] // 8)])


def _softmax_tiles(sts, tiles, rel_bias, slopes, m_old):
    ps = [[] for _ in tiles]
    ms, ls, alphas = [], [], []
    for r in range(B_HPG):
        cols = slice(r * Q_BLOCK, (r + 1) * Q_BLOCK)
        us, shifts, tops = [], [], []
        for st, (mask, penalty, off) in zip(sts, tiles):
            u = st[:, cols] - rel_bias[r]
            u = u if mask is None else jnp.where(mask, u, NEG_INF)
            shift = slopes[r] * jnp.asarray(off, F32)
            shift = shift if penalty is None else shift + penalty
            us.append(u)
            shifts.append(shift)
            tops.append(_fold8(u, jnp.maximum) - shift)
        m_new = jnp.max(functools.reduce(jnp.maximum, tops), axis=0, keepdims=True)
        m_new = jnp.maximum(m_new, M_FLOOR)
        if m_old is not None:
            m_new = jnp.maximum(m_old[:, cols], m_new)
            alphas.append(jnp.exp2(m_old[:, cols] - m_new))
        l8 = jnp.zeros((8, Q_BLOCK), F32)
        for t, (u, shift) in enumerate(zip(us, shifts)):
            p = jnp.exp2(u - (m_new + shift))
            l8 = l8 + _fold8(p, jnp.add)
            ps[t].append(p.astype(BF16))
        ms.append(m_new)
        ls.append(jnp.sum(l8, axis=0, keepdims=True))
    ps = [jnp.concatenate(p, axis=1) for p in ps]
    alpha = None if m_old is None else jnp.concatenate(alphas, axis=1)
    return ps, jnp.concatenate(ms, axis=1), jnp.concatenate(ls, axis=1), alpha


def _nsa_sparse_kernel(lst_ref, qt_ref, ks_ref, vst_ref, kw_ref, vwt_ref, sel_ref, gt_ref, oc_ref, o_ref,
                       m_ref, alpha_ref, l_ref, accw_ref, acc_ref, st_ref, p_ref):
    g = pl.program_id(0)
    i = pl.program_id(1)
    t0 = i * Q_BLOCK
    slopes = _group_slopes(g)
    q4 = _q_heads(qt_ref)

    def gate(branch):
        return jnp.concatenate(
            [gt_ref[pl.ds(branch * N_HEADS + g * B_HPG + r, 1), :] for r in range(B_HPG)], axis=1)

    rel_w = (lax.broadcasted_iota(I32, (Q_BLOCK, Q_BLOCK), 1)
             - lax.broadcasted_iota(I32, (Q_BLOCK, Q_BLOCK), 0))
    n_prev = WIN // Q_BLOCK
    rel_w_f = rel_w.astype(F32)
    bias_w = [slopes[r] * rel_w_f for r in range(B_HPG)]
    tiles, blks = [], []
    for c in range(n_prev + 1):
        blk = i - n_prev + c
        off = (n_prev - c) * Q_BLOCK
        mask = (rel_w >= 0) if c == n_prev else (rel_w < 0) if c == 0 else None
        penalty = None if c == n_prev else jnp.where(blk >= 0, 0.0, MASK_PENALTY).astype(F32)
        tiles.append((mask, penalty, off))
        blks.append(jnp.maximum(blk, 0))
    sts = [jnp.dot(kw_ref[0, blk], q4, preferred_element_type=F32) for blk in blks]
    ps, _, l_w, _ = _softmax_tiles(sts, tiles, bias_w, slopes, None)
    acc_w = functools.reduce(lambda a, b: a + b, [
        jnp.dot(vwt_ref[0, blk], p, preferred_element_type=F32) for blk, p in zip(blks, ps)])
    accw_ref[...] = acc_w * (gate(2) / jnp.maximum(l_w, TINY))

    rel_s = (lax.broadcasted_iota(I32, (SLC_BLOCK, Q_BLOCK), 1)
             - lax.broadcasted_iota(I32, (SLC_BLOCK, Q_BLOCK), 0))
    rel_s_f = rel_s.astype(F32)
    bias_s = [slopes[r] * rel_s_f for r in range(B_HPG)]

    def block_id(j, b):
        return lst_ref[0, 0, j * SEL_CHUNK + b]

    def rows(b):
        return slice(b * SLC_BLOCK, (b + 1) * SLC_BLOCK)

    def scores(j, slot):
        for b in range(SEL_CHUNK):
            st_ref[slot, rows(b), :] = jnp.dot(ks_ref[0, block_id(j, b)], q4, preferred_element_type=F32)

    def weighted_values(j):
        v = jnp.concatenate([vst_ref[0, block_id(j, b)] for b in range(SEL_CHUNK)], axis=0)
        return lax.dot_general(v, p_ref[...], (((0,), (0,)), ((), ())), preferred_element_type=F32)

    m_ref[...] = jnp.full(m_ref.shape, M_FLOOR, F32)
    alpha_ref[...] = jnp.ones(alpha_ref.shape, F32)
    l_ref[...] = jnp.zeros(l_ref.shape, F32)
    acc_ref[...] = jnp.zeros(acc_ref.shape, F32)
    p_ref[...] = jnp.zeros(p_ref.shape, BF16)
    scores(0, 0)

    def sel_chunk(j, c):
        slot = lax.rem(j, 2)
        acc_ref[...] = acc_ref[...] * alpha_ref[0:1, :] + weighted_values(jnp.maximum(j - 1, 0))
        tiles, sts = [], []
        for b in range(SEL_CHUNK):
            blk = block_id(j, b)
            off = t0 - blk * SLC_BLOCK
            penalty = jnp.where(sel_ref[0, 0, pl.ds(blk, 1), :] > 0.5, 0.0, MASK_PENALTY)
            mask = (rel_s >= -off) if b < 2 else None
            tiles.append((mask, penalty, off))
            sts.append(st_ref[slot, rows(b), :])
        ps, m, l, alpha = _softmax_tiles(sts, tiles, bias_s, slopes, m_ref[0:1, :])
        for b in range(SEL_CHUNK):
            p_ref[rows(b), :] = ps[b]
        m_ref[0:1, :] = m
        l_ref[0:1, :] = alpha * l_ref[0:1, :] + l
        alpha_ref[0:1, :] = alpha
        scores(j + 1, 1 - slot)
        return c

    n_chunks = lst_ref[0, 0, LIST_W - 1]
    lax.fori_loop(0, n_chunks, sel_chunk, 0)
    acc_s = acc_ref[...] * alpha_ref[0:1, :] + weighted_values(jnp.maximum(n_chunks - 1, 0))

    total = accw_ref[...] + acc_s * (gate(1) / jnp.maximum(l_ref[0:1, :], TINY))
    ot = jnp.concatenate([total[:, r * Q_BLOCK:(r + 1) * Q_BLOCK] for r in range(B_HPG)], axis=0)
    o_ref[...] = (ot.T + oc_ref[...]).astype(BF16)


def _nsa_sparse(lists, qt, gt, kv, sel, oc, seq):
    nb = seq // Q_BLOCK
    n_slc, n_win = seq // SLC_BLOCK, seq // Q_BLOCK
    gw = B_HPG * HEAD_DIM
    grp = lambda g, i: (g, 0, 0, 0)
    return pl.pallas_call(
        _nsa_sparse_kernel,
        grid=(B_KV_GROUPS, nb),
        in_specs=[pl.BlockSpec((1, 1, LIST_W), lambda g, i: (g * nb + i, 0, 0), memory_space=pltpu.SMEM),
                  pl.BlockSpec((gw, Q_BLOCK), lambda g, i: (g, i)),
                  pl.BlockSpec((1, n_slc, SLC_BLOCK, HEAD_DIM), grp),
                  pl.BlockSpec((1, n_slc, SLC_BLOCK, HEAD_DIM), grp),
                  pl.BlockSpec((1, n_win, Q_BLOCK, HEAD_DIM), grp),
                  pl.BlockSpec((1, n_win, HEAD_DIM, Q_BLOCK), grp),
                  pl.BlockSpec((1, 1, n_slc, Q_BLOCK), lambda g, i: (g, i, 0, 0)),
                  pl.BlockSpec((LANES, Q_BLOCK), lambda g, i: (0, i)),
                  pl.BlockSpec((Q_BLOCK, gw), lambda g, i: (i, g))],
        out_specs=pl.BlockSpec((Q_BLOCK, gw), lambda g, i: (i, g)),
        scratch_shapes=[pltpu.VMEM((8, B_HPG * Q_BLOCK), F32),
                        pltpu.VMEM((8, B_HPG * Q_BLOCK), F32),
                        pltpu.VMEM((8, B_HPG * Q_BLOCK), F32),
                        pltpu.VMEM((HEAD_DIM, B_HPG * Q_BLOCK), F32),
                        pltpu.VMEM((HEAD_DIM, B_HPG * Q_BLOCK), F32),
                        pltpu.VMEM((2, SEL_CHUNK * SLC_BLOCK, B_HPG * Q_BLOCK), F32),
                        pltpu.VMEM((SEL_CHUNK * SLC_BLOCK, B_HPG * Q_BLOCK), BF16)],
        out_shape=jax.ShapeDtypeStruct((seq, D_MODEL), BF16),
        compiler_params=_params(("arbitrary", "arbitrary")),
        name="nsa_selected_window",
    )(lists, qt, kv["ks"], kv["vst"], kv["kw"], kv["vwt"], sel, gt, oc)


def _outproj_b_kernel(o_ref, h_ref, w_ref, g_ref, b_ref, hn_ref, hb_ref):
    mix = jnp.dot(o_ref[...], w_ref[...], preferred_element_type=F32)
    _layer_norm_store(DEEPNORM_ALPHA * h_ref[...] + mix, g_ref, b_ref, hn_ref, hb_ref)


def _outproj_b(o, h, w_out, ln_g, ln_b, tm=512):
    seq = h.shape[0]
    row = lambda i: (i, 0)
    fix = lambda i: (0, 0)
    return pl.pallas_call(
        _outproj_b_kernel,
        grid=(seq // tm,),
        in_specs=[pl.BlockSpec((tm, D_MODEL), row), pl.BlockSpec((tm, D_MODEL), row),
                  pl.BlockSpec((D_MODEL, D_MODEL), fix), pl.BlockSpec((1, D_MODEL), fix),
                  pl.BlockSpec((1, D_MODEL), fix)],
        out_specs=[pl.BlockSpec((tm, D_MODEL), row)] * 2,
        out_shape=[jax.ShapeDtypeStruct((seq, D_MODEL), F32), jax.ShapeDtypeStruct((seq, D_MODEL), BF16)],
        compiler_params=_params(("arbitrary",)),
        name="outproj_ln_b",
    )(o, h, w_out.astype(BF16), ln_g.reshape(1, -1), ln_b.reshape(1, -1))


def _nsa_layer(h, hb, kv, w_q, b_q, w_out, ln_g, ln_b):
    seq = h.shape[0]
    nb, n_slc = seq // Q_BLOCK, seq // SLC_BLOCK
    qt, gt = _qproj(hb, w_q, b_q)
    oc, sel, flags = _nsa_cmp(qt, gt, kv, seq)
    f = flags.reshape(B_KV_GROUPS, nb, n_slc)
    newest = 2 * jnp.arange(nb, dtype=I32)[None, :, None] + 1
    age = jnp.mod(newest - jnp.arange(n_slc, dtype=I32)[None, None, :], n_slc)
    order = jnp.argsort((1 - f) * n_slc + age, axis=2).astype(I32).reshape(B_KV_GROUPS * nb, n_slc)
    n_chunks = (jnp.sum(f, axis=2).reshape(-1) + SEL_CHUNK - 1) // SEL_CHUNK
    lists = jnp.concatenate(
        [order, jnp.zeros((B_KV_GROUPS * nb, LIST_W - n_slc - 1), I32), n_chunks.astype(I32)[:, None]], axis=1)
    o = _nsa_sparse(lists.reshape(B_KV_GROUPS * nb, 1, LIST_W), qt, gt, kv, sel, oc, seq)
    return _outproj_b(o, h, w_out, ln_g, ln_b)


def kernel(x, a_w_in, a_w_out, b_w_kv, b_cmp_pos, b_cmp_w1, b_cmp_b1, b_cmp_w2, b_cmp_b2, b_w_q, b_b_q, b_w_out, moe_w_router, moe_b_router, moe_w_gate_up, moe_w_down, ln_mix_g, ln_mix_b, ln_ffn_g, ln_ffn_b):
    seq = x.shape[1]
    h = x.reshape(seq, D_MODEL)
    hb = h.astype(BF16)
    a_w_in_b, w_gate_up_b, w_down_b = a_w_in.astype(BF16), moe_w_gate_up.astype(BF16), moe_w_down.astype(BF16)
    buf = None
    for layer in range(DEPTH):
        if layer < N_A_LAYERS:
            h, hb = _dilated_layer(h, hb, a_w_in_b, layer, a_w_out[layer], ln_mix_g[layer], ln_mix_b[layer])
        else:
            if layer == N_A_LAYERS:
                kv = _nsa_shared_kv(hb, b_w_kv, b_cmp_pos, b_cmp_w1, b_cmp_b1, b_cmp_w2, b_cmp_b2)
            j = layer - N_A_LAYERS
            h, hb = _nsa_layer(h, hb, kv, b_w_q[j], b_b_q[j], b_w_out[j], ln_mix_g[layer], ln_mix_b[layer])
        h, hb, buf = _moe_layer(h, moe_w_router[layer], moe_b_router[layer], w_gate_up_b, w_down_b, layer,
                                ln_ffn_g[layer], ln_ffn_b[layer], buf)
    return h.reshape(1, seq, D_MODEL)
```

```python
import functools
import math

import numpy as np
import jax
import jax.numpy as jnp
from jax import lax
from jax.experimental import pallas as pl
from jax.experimental.pallas import tpu as pltpu

F32 = jnp.float32
BF16 = jnp.bfloat16
I32 = jnp.int32

D_MODEL = 1024
DEPTH = 4
N_A_LAYERS = DEPTH // 2
HEAD_DIM = 64
N_HEADS = D_MODEL // HEAD_DIM
Q_BLOCK = 128
LN_EPS = 1e-5
NEG_INF = -1e30
TINY = 1e-30

A_WINDOWS = (128, 512, 2048)
A_DILATIONS = (1, 4, 16)
A_N_GROUPS = 3

N_EXPERT_GROUPS = 4
EXPERTS_PER_GROUP = 4
N_EXPERTS = 16
D_EXPERT = 256

DEEPNORM_ALPHA = (2.0 * DEPTH) ** 0.25

LANES = 128
MOE_TILE = 256
VMEM_LIMIT = 56 * 1024 * 1024


def _slopes(n):
    return [float(v) for v in np.asarray(2.0 ** (-8.0 * np.arange(1, n + 1) / n), dtype=np.float32)]


def _params(sem):
    return pltpu.CompilerParams(dimension_semantics=sem, vmem_limit_bytes=VMEM_LIMIT)


def _mm_kernel(x_ref, w_ref, o_ref):
    o_ref[...] = jnp.dot(x_ref[...], w_ref[...], preferred_element_type=F32).astype(o_ref.dtype)


def _matmul(x, w, out_dtype, tm, tn, name):
    m, k = x.shape
    n = w.shape[1]
    return pl.pallas_call(
        _mm_kernel,
        grid=(n // tn, m // tm),
        in_specs=[pl.BlockSpec((tm, k), lambda j, i: (i, 0)),
                  pl.BlockSpec((k, tn), lambda j, i: (0, j))],
        out_specs=pl.BlockSpec((tm, tn), lambda j, i: (i, j)),
        out_shape=jax.ShapeDtypeStruct((m, n), out_dtype),
        compiler_params=_params(("arbitrary", "arbitrary")),
        name=name,
    )(x, w)


def _layer_norm_store(z, g_ref, b_ref, h_ref, hb_ref):
    mu = jnp.mean(z, axis=-1, keepdims=True)
    zc = z - mu
    var = jnp.mean(zc * zc, axis=-1, keepdims=True)
    y = zc * lax.rsqrt(var + LN_EPS) * g_ref[...] + b_ref[...]
    h_ref[...] = y
    hb_ref[...] = y.astype(BF16)


def _split_bf16(x):
    hi = x.astype(BF16)
    lo = (x - hi.astype(F32)).astype(BF16)
    return hi, lo


def _dil_attn_kernel(q_ref, kp_ref, kc_ref, vp_ref, vc_ref, o_ref, lse_ref, bias_ref, *, dil, slopes):
    r = pl.program_id(0)
    jb = pl.program_id(1)
    qb = Q_BLOCK

    @pl.when((r == 0) & (jb == 0))
    def _():
        key = lax.broadcasted_iota(I32, (2 * qb, qb), 0)
        qry = lax.broadcasted_iota(I32, (2 * qb, qb), 1)
        dist = qry + qb - key
        valid = (dist >= 0) & (dist <= qb)
        dist_tok = (dist * dil).astype(F32)
        for h in range(N_HEADS):
            b = jnp.where(valid, -(slopes[h] * LOG2E * dist_tok), NEG_INF)
            bias_ref[1, h] = b
            bias_ref[0, h] = jnp.where(key >= qb, b, NEG_INF)

    first = jnp.minimum(jb, 1)
    lane = lax.broadcasted_iota(I32, (1, LANES), 1)
    lo_half = lane < HEAD_DIM
    lo_rows = lax.broadcasted_iota(I32, (LANES, qb), 0) < HEAD_DIM
    lses = []
    for hp in range(N_HEADS // 2):
        sl = slice(hp * LANES, (hp + 1) * LANES)
        q2 = q_ref[:, sl]
        k2 = jnp.concatenate([kp_ref[:, sl], kc_ref[:, sl]], axis=0)
        v2t = jnp.concatenate([vp_ref[:, sl], vc_ref[:, sl]], axis=0).T
        outs = []
        for half in range(2):
            keep = lo_half if half == 0 else jnp.logical_not(lo_half)
            qm = jnp.where(keep, q2, jnp.zeros_like(q2))
            st = lax.dot_general(k2, qm, (((1,), (1,)), ((), ())), preferred_element_type=F32)
            st = st + bias_ref[first, 2 * hp + half]
            m = jnp.max(st, axis=0, keepdims=True)
            p = jnp.exp2(st - m)
            l = jnp.sum(p, axis=0, keepdims=True)
            ot = jnp.dot(v2t, p.astype(BF16), preferred_element_type=F32)
            outs.append(ot * (1.0 / l))
            lses.append(m + jnp.log2(l))
        o_ref[:, sl] = jnp.where(lo_rows, outs[0], outs[1]).T.astype(BF16)
    lse_t = jnp.concatenate(lses + [jnp.zeros((LANES - N_HEADS, qb), F32)], axis=0)
    lse_ref[...] = lse_t.T


def _qkv_kernel(x_ref, w_ref, o_ref, acc_ref, *, dil):
    acc = jnp.dot(x_ref[...], w_ref[...], preferred_element_type=F32)
    acc = acc * jnp.where(pl.program_id(0) == 0, HEAD_DIM ** -0.5 * LOG2E, 1.0)
    if dil == 1:
        o_ref[0] = acc.astype(BF16)
    else:
        n_chunks = acc.shape[1] // LANES
        for c in range(n_chunks):
            acc_ref[c] = acc[:, c * LANES:(c + 1) * LANES]
        n = acc.shape[0] // dil
        for r in range(dil):
            o_ref[r] = jnp.concatenate(
                [acc_ref[c, pl.ds(r, n, stride=dil), :] for c in range(n_chunks)], axis=1).astype(BF16)


def _qkv_group(hb, w_in, layer, g, tm=1024):
    seq = hb.shape[0]
    dil = A_DILATIONS[g]
    return pl.pallas_call(
        functools.partial(_qkv_kernel, dil=dil),
        grid=(3, seq // tm),
        in_specs=[pl.BlockSpec((tm, D_MODEL), lambda c, i: (i, 0)),
                  pl.BlockSpec((None, D_MODEL, D_MODEL), lambda c, i: (layer, 0, 3 * g + c))],
        out_specs=pl.BlockSpec((dil, tm // dil, D_MODEL), lambda c, i: (0, i, c)),
        out_shape=jax.ShapeDtypeStruct((dil, seq // dil, 3 * D_MODEL), BF16),
        scratch_shapes=[pltpu.VMEM((D_MODEL // LANES, tm, LANES), F32)],
        compiler_params=_params(("arbitrary", "arbitrary")),
        name=f"qkv_proj_g{g}",
    )(hb, w_in)


def _dilated_group(qkv_g, g, seq):
    dil = A_DILATIONS[g]
    rows = seq // dil
    nblk = rows // Q_BLOCK
    blk = (None, Q_BLOCK, D_MODEL)
    kern = functools.partial(_dil_attn_kernel, dil=dil, slopes=_slopes(N_HEADS))
    return pl.pallas_call(
        kern,
        grid=(dil, nblk),
        in_specs=[
            pl.BlockSpec(blk, lambda r, j: (r, j, 0)),
            pl.BlockSpec(blk, lambda r, j: (r, jnp.maximum(j - 1, 0), 1)),
            pl.BlockSpec(blk, lambda r, j: (r, j, 1)),
            pl.BlockSpec(blk, lambda r, j: (r, jnp.maximum(j - 1, 0), 2)),
            pl.BlockSpec(blk, lambda r, j: (r, j, 2)),
        ],
        out_specs=[pl.BlockSpec(blk, lambda r, j: (r, j, 0)),
                   pl.BlockSpec((None, Q_BLOCK, LANES), lambda r, j: (r, j, 0))],
        out_shape=[jax.ShapeDtypeStruct((dil, rows, D_MODEL), BF16),
                   jax.ShapeDtypeStruct((dil, rows, LANES), F32)],
        scratch_shapes=[pltpu.VMEM((2, N_HEADS, 2 * Q_BLOCK, Q_BLOCK), F32)],
        compiler_params=_params(("arbitrary", "arbitrary")),
        name=f"dilated_attn_g{g}",
    )(qkv_g, qkv_g, qkv_g, qkv_g, qkv_g)


def _head_expand_matrix():
    h = np.arange(LANES)[:, None]
    c = np.arange(D_MODEL)[None, :]
    return jnp.asarray((c // HEAD_DIM == h).astype(np.float32), BF16)


def _to_token_order(src_ref, dst_ref, dil):
    n = src_ref.shape[1]
    n_chunks = dst_ref.shape[0]
    if dil == 1:
        return src_ref[0].astype(F32)
    for r in range(dil):
        v = src_ref[r].astype(F32)
        for c in range(n_chunks):
            dst_ref[c, pl.ds(r, n, stride=dil), :] = v[:, c * LANES:(c + 1) * LANES]
    return jnp.concatenate([dst_ref[c] for c in range(n_chunks)], axis=1)


def _outproj_a_kernel(o0_ref, o1_ref, o2_ref, l0_ref, l1_ref, l2_ref, h_ref, w_ref, e_ref, g_ref, b_ref,
                      hn_ref, hb_ref, onat_ref, lnat_ref):
    l0, l1, l2 = [_to_token_order(l_ref, lnat_ref, A_DILATIONS[gi])
                  for gi, l_ref in enumerate((l0_ref, l1_ref, l2_ref))]
    m = jnp.maximum(jnp.maximum(l0, l1), l2)
    e0, e1, e2 = jnp.exp2(l0 - m), jnp.exp2(l1 - m), jnp.exp2(l2 - m)
    inv = 1.0 / (e0 + e1 + e2)
    o = jnp.zeros(h_ref.shape, F32)
    for gi, (e, o_ref) in enumerate(((e0, o0_ref), (e1, o1_ref), (e2, o2_ref))):
        hi, lo = _split_bf16(e * inv)
        wexp = (jnp.dot(hi, e_ref[...], preferred_element_type=F32)
                + jnp.dot(lo, e_ref[...], preferred_element_type=F32))
        o = o + wexp * _to_token_order(o_ref, onat_ref, A_DILATIONS[gi])
    mix = jnp.dot(o.astype(BF16), w_ref[...], preferred_element_type=F32)
    _layer_norm_store(DEEPNORM_ALPHA * h_ref[...] + mix, g_ref, b_ref, hn_ref, hb_ref)


def _outproj_a(outs, lses, h, w_out, ln_g, ln_b, tm=512):
    seq = h.shape[0]
    row = lambda i: (i, 0)
    fix = lambda i: (0, 0)
    cls = lambda i: (0, i, 0)
    return pl.pallas_call(
        _outproj_a_kernel,
        grid=(seq // tm,),
        in_specs=[pl.BlockSpec((d, tm // d, D_MODEL), cls) for d in A_DILATIONS]
        + [pl.BlockSpec((d, tm // d, LANES), cls) for d in A_DILATIONS] + [
            pl.BlockSpec((tm, D_MODEL), row),
            pl.BlockSpec((D_MODEL, D_MODEL), fix),
            pl.BlockSpec((LANES, D_MODEL), fix),
            pl.BlockSpec((1, D_MODEL), fix),
            pl.BlockSpec((1, D_MODEL), fix)],
        out_specs=[pl.BlockSpec((tm, D_MODEL), row)] * 2,
        out_shape=[jax.ShapeDtypeStruct((seq, D_MODEL), F32), jax.ShapeDtypeStruct((seq, D_MODEL), BF16)],
        scratch_shapes=[pltpu.VMEM((D_MODEL // LANES, tm, LANES), F32), pltpu.VMEM((1, tm, LANES), F32)],
        compiler_params=_params(("arbitrary",)),
        name="outproj_ln_a",
    )(*outs, *lses, h, w_out.astype(BF16), _head_expand_matrix(), ln_g.reshape(1, -1), ln_b.reshape(1, -1))


def _dilated_layer(h, hb, w_in, layer, w_out, ln_g, ln_b):
    seq = h.shape[0]
    outs, lses = [], []
    for g in range(A_N_GROUPS):
        qkv_g = _qkv_group(hb, w_in, layer, g)
        o, lse = _dilated_group(qkv_g, g, seq)
        outs.append(o)
        lses.append(lse)
    return _outproj_a(outs, lses, h, w_out, ln_g, ln_b)


ROUTER_ROWS = 24


def _router_kernel(h_ref, w_ref, b_ref, ids_ref, wts_ref, cnt_ref, base_ref):
    i = pl.program_id(0)
    tm = h_ref.shape[0]

    @pl.when(i == 0)
    def _():
        base_ref[...] = jnp.zeros_like(base_ref)

    lt = lax.dot_general(w_ref[...], h_ref[...], (((1,), (1,)), ((), ())),
                         precision=lax.Precision.HIGHEST, preferred_element_type=F32)
    lt = lt + b_ref[:, 0:1]
    gl = [lt[k:k + 1, :] for k in range(N_EXPERT_GROUPS)]
    best, gidx = gl[0], jnp.zeros((1, tm), I32)
    for k in range(1, N_EXPERT_GROUPS):
        take = gl[k] > best
        best = jnp.where(take, gl[k], best)
        gidx = jnp.where(take, k, gidx)
    denom = gl[0] * 0.0
    for k in range(N_EXPERT_GROUPS):
        denom = denom + jnp.exp(gl[k] - best)
    g_w = 1.0 / denom
    e_in = []
    for e in range(EXPERTS_PER_GROUP):
        v = lt[N_EXPERT_GROUPS + e:N_EXPERT_GROUPS + e + 1, :]
        for k in range(1, N_EXPERT_GROUPS):
            row = N_EXPERT_GROUPS + k * EXPERTS_PER_GROUP + e
            v = jnp.where(gidx == k, lt[row:row + 1, :], v)
        e_in.append(v)
    v1, i1 = e_in[0], jnp.zeros((1, tm), I32)
    for e in range(1, EXPERTS_PER_GROUP):
        take = e_in[e] > v1
        v1 = jnp.where(take, e_in[e], v1)
        i1 = jnp.where(take, e, i1)
    v2, i2 = jnp.full((1, tm), -jnp.inf, F32), jnp.zeros((1, tm), I32)
    for e in range(EXPERTS_PER_GROUP):
        take = (e_in[e] > v2) & (i1 != e)
        v2 = jnp.where(take, e_in[e], v2)
        i2 = jnp.where(take, e, i2)
    t = jnp.exp(v2 - v1)
    w1 = g_w / (1.0 + t)
    w2 = g_w * t / (1.0 + t)
    e1 = gidx * EXPERTS_PER_GROUP + i1
    e2 = gidx * EXPERTS_PER_GROUP + i2

    erow = lax.broadcasted_iota(I32, (N_EXPERTS, tm), 0)
    oh1 = (erow == e1).astype(F32)
    oh2 = (erow == e2).astype(F32)
    c = oh1 + oh2
    tri = (lax.broadcasted_iota(I32, (tm, tm), 0) < lax.broadcasted_iota(I32, (tm, tm), 1)).astype(BF16)
    excl = jnp.dot(c.astype(BF16), tri, preferred_element_type=F32) + base_ref[...]
    r1 = jnp.sum(oh1 * excl, axis=0, keepdims=True).astype(I32)
    r2 = jnp.sum(oh2 * excl, axis=0, keepdims=True).astype(I32)
    base_ref[...] = base_ref[...] + jnp.sum(c, axis=1, keepdims=True)

    zi = jnp.zeros((1, tm), I32)
    ids_ref[...] = jnp.concatenate([e1, e2, r1, r2, zi, zi, zi, zi], axis=0)
    zf = jnp.zeros((1, tm), F32)
    wts_ref[...] = jnp.concatenate([w1, w2, zf, zf, zf, zf, zf, zf], axis=0)
    cnt_ref[...] = jnp.broadcast_to(base_ref[...], cnt_ref.shape)


def _router(h, w_router, b_router, tm=512):
    seq = h.shape[0]
    n_log = N_EXPERT_GROUPS + N_EXPERTS
    w_t = jnp.zeros((ROUTER_ROWS, D_MODEL), F32).at[:n_log].set(w_router.T)
    b_t = jnp.zeros((ROUTER_ROWS, LANES), F32).at[:n_log].set(jnp.broadcast_to(b_router[:, None], (n_log, LANES)))
    return pl.pallas_call(
        _router_kernel,
        grid=(seq // tm,),
        in_specs=[pl.BlockSpec((tm, D_MODEL), lambda i: (i, 0)),
                  pl.BlockSpec((ROUTER_ROWS, D_MODEL), lambda i: (0, 0)),
                  pl.BlockSpec((ROUTER_ROWS, LANES), lambda i: (0, 0))],
        out_specs=[pl.BlockSpec((8, tm), lambda i: (0, i)),
                   pl.BlockSpec((8, tm), lambda i: (0, i)),
                   pl.BlockSpec((N_EXPERTS, LANES), lambda i: (0, 0))],
        out_shape=[jax.ShapeDtypeStruct((8, seq), I32), jax.ShapeDtypeStruct((8, seq), F32),
                   jax.ShapeDtypeStruct((N_EXPERTS, LANES), F32)],
        scratch_shapes=[pltpu.VMEM((N_EXPERTS, 1), F32)],
        compiler_params=_params(("arbitrary",)),
        name="moe_router",
    )(h, w_t, b_t)


ROW_TILE = D_MODEL // LANES


def _rows_to_tiles(x, dst_ref):
    n = x.shape[0]
    for c in range(ROW_TILE):
        dst_ref[pl.ds(c, n, stride=ROW_TILE), :] = x[:, c * LANES:(c + 1) * LANES]


def _tiles_to_rows(src_ref):
    n = src_ref.shape[0] // ROW_TILE
    return jnp.concatenate([src_ref[pl.ds(c, n, stride=ROW_TILE), :] for c in range(ROW_TILE)], axis=1)


def _row_tile(ref, p):
    return ref.at[pl.ds(pl.multiple_of(p * ROW_TILE, ROW_TILE), ROW_TILE)]


def _dispatch_kernel(pos_ref, h_ref, xs_in_ref, xs_ref, ht_ref, sem):
    del xs_in_ref
    tm = h_ref.shape[0]
    _rows_to_tiles(h_ref[...], ht_ref)

    def row_copy(i, p):
        return pltpu.make_async_copy(_row_tile(ht_ref, i), _row_tile(xs_ref, p), sem)

    def start(i, c):
        row_copy(i, pos_ref[0, 0, i]).start(priority=0)
        row_copy(i, pos_ref[0, 0, tm + i]).start(priority=1)
        return c

    def wait(i, c):
        row_copy(0, 0).wait()
        row_copy(0, 0).wait()
        return c

    lax.fori_loop(0, tm, start, 0, unroll=8)
    lax.fori_loop(0, tm, wait, 0, unroll=8)


def _dispatch(h, pos, buf, tm=256):
    seq = h.shape[0]
    nt = seq // tm
    pos3 = pos.reshape(2, nt, tm).transpose(1, 0, 2).reshape(nt, 1, 2 * tm)
    return pl.pallas_call(
        _dispatch_kernel,
        grid=(nt,),
        in_specs=[pl.BlockSpec((1, 1, 2 * tm), lambda i: (i, 0, 0), memory_space=pltpu.SMEM),
                  pl.BlockSpec((tm, D_MODEL), lambda i: (i, 0)),
                  pl.BlockSpec(memory_space=pl.ANY)],
        out_specs=pl.BlockSpec(memory_space=pl.ANY),
        out_shape=jax.ShapeDtypeStruct(buf.shape, F32),
        scratch_shapes=[pltpu.VMEM((tm * ROW_TILE, LANES), F32), pltpu.SemaphoreType.DMA(())],
        input_output_aliases={2: 0},
        compiler_params=_params(("arbitrary",)),
        name="moe_dispatch",
    )(pos3, h, buf)


def _expert_kernel(te_ref, x_ref, wgu_ref, wd_ref, y_ref):
    del te_ref
    gu = jnp.dot(_tiles_to_rows(x_ref).astype(BF16), wgu_ref[...], preferred_element_type=F32)
    gate, up = gu[:, :D_EXPERT], gu[:, D_EXPERT:]
    hid = gate * (1.0 / (1.0 + jnp.exp(-gate))) * up
    _rows_to_tiles(jnp.dot(hid.astype(BF16), wd_ref[...], preferred_element_type=F32), y_ref)


def _experts(xs, tile_expert, w_gate_up, w_down, layer):
    blk = MOE_TILE * ROW_TILE
    return pl.pallas_call(
        _expert_kernel,
        grid_spec=pltpu.PrefetchScalarGridSpec(
            num_scalar_prefetch=1,
            grid=(xs.shape[0] // blk,),
            in_specs=[pl.BlockSpec((blk, LANES), lambda i, te: (i, 0)),
                      pl.BlockSpec((None, None, D_MODEL, 2 * D_EXPERT), lambda i, te: (layer, te[i], 0, 0)),
                      pl.BlockSpec((None, None, D_EXPERT, D_MODEL), lambda i, te: (layer, te[i], 0, 0))],
            out_specs=pl.BlockSpec((blk, LANES), lambda i, te: (i, 0))),
        out_shape=jax.ShapeDtypeStruct(xs.shape, F32),
        compiler_params=_params(("arbitrary",)),
        name="moe_experts",
    )(tile_expert, xs, w_gate_up, w_down)


def _combine_kernel(pos_ref, pos_next_ref, ys_ref, h_ref, w_ref, g_ref, b_ref, hn_ref, hb_ref, buf_ref, sem):
    step = pl.program_id(0)
    tm = h_ref.shape[0]
    slot = lax.rem(step, 2)

    def row_copy(s, k, i, p):
        return pltpu.make_async_copy(_row_tile(ys_ref, p), _row_tile(buf_ref.at[s, k], i), sem.at[s])

    def gather(p_ref, s):
        def start(i, c):
            row_copy(s, 0, i, p_ref[0, 0, i]).start(priority=0)
            row_copy(s, 1, i, p_ref[0, 0, tm + i]).start(priority=1)
            return c
        lax.fori_loop(0, tm, start, 0, unroll=8)

    @pl.when(step == 0)
    def _():
        gather(pos_ref, 0)

    @pl.when(step + 1 < pl.num_programs(0))
    def _():
        gather(pos_next_ref, 1 - slot)

    def wait(i, c):
        row_copy(slot, 0, 0, 0).wait()
        row_copy(slot, 1, 0, 0).wait()
        return c

    lax.fori_loop(0, tm, wait, 0, unroll=8)
    w = w_ref[...]
    ffn = (w[:, 0:1] * _tiles_to_rows(buf_ref.at[slot, 0])
           + w[:, 1:2] * _tiles_to_rows(buf_ref.at[slot, 1]))
    _layer_norm_store(DEEPNORM_ALPHA * h_ref[...] + ffn, g_ref, b_ref, hn_ref, hb_ref)


def _combine(ys, pos, wts, h, ln_g, ln_b, tm=256):
    seq = h.shape[0]
    nt = seq // tm
    pos3 = pos.reshape(2, nt, tm).transpose(1, 0, 2).reshape(nt, 1, 2 * tm)
    row = lambda i: (i, 0)
    fix = lambda i: (0, 0)
    return pl.pallas_call(
        _combine_kernel,
        grid=(nt,),
        in_specs=[pl.BlockSpec((1, 1, 2 * tm), lambda i: (i, 0, 0), memory_space=pltpu.SMEM),
                  pl.BlockSpec((1, 1, 2 * tm), lambda i: (jnp.minimum(i + 1, nt - 1), 0, 0),
                               memory_space=pltpu.SMEM),
                  pl.BlockSpec(memory_space=pl.ANY),
                  pl.BlockSpec((tm, D_MODEL), row),
                  pl.BlockSpec((tm, 2), row),
                  pl.BlockSpec((1, D_MODEL), fix),
                  pl.BlockSpec((1, D_MODEL), fix)],
        out_specs=[pl.BlockSpec((tm, D_MODEL), row)] * 2,
        out_shape=[jax.ShapeDtypeStruct((seq, D_MODEL), F32), jax.ShapeDtypeStruct((seq, D_MODEL), BF16)],
        scratch_shapes=[pltpu.VMEM((2, 2, tm * ROW_TILE, LANES), F32), pltpu.SemaphoreType.DMA((2,))],
        compiler_params=_params(("arbitrary",)),
        name="moe_combine_ln",
    )(pos3, pos3, ys, h, wts, ln_g.reshape(1, -1), ln_b.reshape(1, -1))


def _moe_layer(h, w_router, b_router, w_gate_up, w_down, layer, ln_g, ln_b, buf):
    seq = h.shape[0]
    ids, wts, cnt = _router(h, w_router, b_router)
    counts = cnt[:, 0].astype(I32)
    padded = ((counts + MOE_TILE - 1) // MOE_TILE) * MOE_TILE
    ends = jnp.cumsum(padded)
    offs = ends - padded
    eids = jnp.arange(N_EXPERTS, dtype=I32)[:, None, None]
    pos = jnp.sum(jnp.where(ids[None, 0:2] == eids, offs[:, None, None], 0), axis=0) + ids[2:4]
    n_rows = 2 * seq + N_EXPERTS * MOE_TILE
    tile_start = jnp.arange(n_rows // MOE_TILE, dtype=I32) * MOE_TILE
    tile_expert = jnp.minimum(jnp.sum((tile_start[:, None] >= ends[None, :]).astype(I32), axis=1), N_EXPERTS - 1)
    if buf is None:
        buf = jnp.zeros((n_rows * ROW_TILE, LANES), F32)
    xs = _dispatch(h, pos, buf)
    ys = _experts(xs, tile_expert.astype(I32), w_gate_up, w_down, layer)
    return _combine(ys, pos, wts[0:2].T, h, ln_g, ln_b) + (ys,)


B_KV_GROUPS = 4
B_HPG = N_HEADS // B_KV_GROUPS
CMP_LEN = 32
CMP_STRIDE = 16
CMP_HIDDEN = 256
SLC_BLOCK = 64
N_SELECT = 16
WIN = 512
FORCE_SCORE = 1e4
N_FORCED = 3
LIST_W = 272
LOG2E = math.log2(math.e)
MASK_PENALTY = 1e30
M_FLOOR = -1e29
CMP_PARTS = 8
SEL_CHUNK = 4


def _group_slopes(g):
    sl = [v * LOG2E for v in _slopes(N_HEADS)]
    out = []
    for r in range(B_HPG):
        v = jnp.full((1, 1), sl[r], F32)
        for k in range(1, B_KV_GROUPS):
            v = jnp.where(g == k, sl[k * B_HPG + r], v)
        out.append(v)
    return out


def _qproj_kernel(hb_ref, w_ref, b_ref, qt_ref, gt_ref):
    pt = lax.dot_general(w_ref[...], hb_ref[...], (((1,), (1,)), ((), ())), preferred_element_type=F32)
    pt = pt + b_ref[:, 0:1]
    qt_ref[...] = (pt[:D_MODEL] * (HEAD_DIM ** -0.5 * LOG2E)).astype(BF16)
    gt_ref[...] = 1.0 / (1.0 + jnp.exp(-pt[D_MODEL:]))


def _qproj(hb, w_q, b_q, tm=512):
    seq = hb.shape[0]
    n = w_q.shape[1]
    npad = D_MODEL + LANES
    w_t = jnp.zeros((npad, D_MODEL), BF16).at[:n].set(w_q.T.astype(BF16))
    b_t = jnp.zeros((npad, LANES), F32).at[:n].set(jnp.broadcast_to(b_q[:, None], (n, LANES)))
    return pl.pallas_call(
        _qproj_kernel,
        grid=(seq // tm,),
        in_specs=[pl.BlockSpec((tm, D_MODEL), lambda i: (i, 0)),
                  pl.BlockSpec((npad, D_MODEL), lambda i: (0, 0)),
                  pl.BlockSpec((npad, LANES), lambda i: (0, 0))],
        out_specs=[pl.BlockSpec((D_MODEL, tm), lambda i: (0, i)),
                   pl.BlockSpec((LANES, tm), lambda i: (0, i))],
        out_shape=[jax.ShapeDtypeStruct((D_MODEL, seq), BF16), jax.ShapeDtypeStruct((LANES, seq), F32)],
        compiler_params=_params(("arbitrary",)),
        name="nsa_qproj",
    )(hb, w_t, b_t)


def _compress_kernel(x_ref, w1a_ref, w1b_ref, pa_ref, pb_ref, b1_ref, w2_ref, b2_ref, o_ref):
    x = x_ref[0, 0]
    n = x.shape[0]
    a = jnp.dot((x + pa_ref[0]).astype(BF16), w1a_ref[0], preferred_element_type=F32)
    b = jnp.dot((x + pb_ref[0]).astype(BF16), w1b_ref[0], preferred_element_type=F32)
    b = pltpu.roll(b, n - 1, 0)
    u = a + b + b1_ref[0]
    hid = 0.5 * u * (1.0 + jnp.tanh(math.sqrt(2.0 / math.pi) * (u + 0.044715 * (u * u * u))))
    o_ref[0, 0] = (jnp.dot(hid.astype(BF16), w2_ref[0], preferred_element_type=F32) + b2_ref[0]).astype(BF16)


def _compress(kvc, cmp_pos, cmp_w1, cmp_b1, cmp_w2, cmp_b2):
    seq = kvc.shape[0]
    nch = seq // CMP_STRIDE
    half = CMP_STRIDE * HEAD_DIM
    chunks = kvc.reshape(nch, CMP_STRIDE, 2, B_KV_GROUPS, HEAD_DIM).transpose(2, 3, 0, 1, 4).reshape(
        2, B_KV_GROUPS, nch, half)
    pos = cmp_pos.reshape(2, 1, CMP_LEN * HEAD_DIM)
    w1 = cmp_w1.astype(BF16)
    jfix = lambda j, g: (j, 0, 0)
    return pl.pallas_call(
        _compress_kernel,
        grid=(2, B_KV_GROUPS),
        in_specs=[pl.BlockSpec((1, 1, nch, half), lambda j, g: (j, g, 0, 0)),
                  pl.BlockSpec((1, half, CMP_HIDDEN), jfix),
                  pl.BlockSpec((1, half, CMP_HIDDEN), lambda j, g: (j, 1, 0)),
                  pl.BlockSpec((1, 1, half), jfix),
                  pl.BlockSpec((1, 1, half), lambda j, g: (j, 0, 1)),
                  pl.BlockSpec((1, 1, CMP_HIDDEN), jfix),
                  pl.BlockSpec((1, CMP_HIDDEN, HEAD_DIM), jfix),
                  pl.BlockSpec((1, 1, HEAD_DIM), jfix)],
        out_specs=pl.BlockSpec((1, 1, nch, HEAD_DIM), lambda j, g: (j, g, 0, 0)),
        out_shape=jax.ShapeDtypeStruct((2, B_KV_GROUPS, nch, HEAD_DIM), BF16),
        compiler_params=_params(("arbitrary", "arbitrary")),
        name="nsa_compress",
    )(chunks, w1, w1, pos, pos, cmp_b1.reshape(2, 1, CMP_HIDDEN), cmp_w2.astype(BF16),
      cmp_b2.reshape(2, 1, HEAD_DIM))


def _nsa_shared_kv(hb, w_kv, cmp_pos, cmp_w1, cmp_b1, cmp_w2, cmp_b2):
    seq = hb.shape[0]
    gw = B_KV_GROUPS * HEAD_DIM
    w = w_kv.astype(BF16)
    kvc = _matmul(hb, w[:, :2 * gw], F32, 512, 2 * gw, "kv_proj_cmp")
    kvr = _matmul(hb, w[:, 2 * gw:], BF16, 512, 4 * gw, "kv_proj_rest")
    cmp = _compress(kvc, cmp_pos, cmp_w1, cmp_b1, cmp_w2, cmp_b2)
    n_slc, n_win = seq // SLC_BLOCK, seq // Q_BLOCK

    def blocks(cols, nblk, blk, transpose):
        t = cols.reshape(nblk, blk, B_KV_GROUPS, HEAD_DIM)
        return t.transpose(2, 0, 3, 1) if transpose else t.transpose(2, 0, 1, 3)

    return dict(
        kc=cmp[0], vct=cmp[1].transpose(0, 2, 1),
        ks=blocks(kvr[:, 0 * gw:1 * gw], n_slc, SLC_BLOCK, False),
        vst=blocks(kvr[:, 1 * gw:2 * gw], n_slc, SLC_BLOCK, False),
        kw=blocks(kvr[:, 2 * gw:3 * gw], n_win, Q_BLOCK, False),
        vwt=blocks(kvr[:, 3 * gw:4 * gw], n_win, Q_BLOCK, True))


def _select_matrix(n_slc, nch):
    m = np.zeros((n_slc, nch), np.float32)
    r_s, r_c = SLC_BLOCK // CMP_STRIDE, CMP_LEN // CMP_STRIDE
    for a in range(r_s):
        for b in range(r_c):
            idx = r_s * np.arange(n_slc) + a - b
            ok = (idx >= 0) & (idx < nch - r_c + 1)
            m[np.arange(n_slc)[ok], idx[ok]] += 1.0
    return jnp.asarray(m, BF16)


def _q_heads(qt_ref):
    return jnp.concatenate([qt_ref[r * HEAD_DIM:(r + 1) * HEAD_DIM, :] for r in range(B_HPG)], axis=1)


def _nsa_cmp_kernel(qt_ref, kc_ref, vct_ref, gt_ref, mt_ref, oc_ref, sel_ref, flag_ref, bias_ref):
    g = pl.program_id(0)
    i = pl.program_id(1)
    nch = kc_ref.shape[1]
    n_slc = mt_ref.shape[0]
    t0 = i * Q_BLOCK

    @pl.when(i == 0)
    def _():
        slopes = _group_slopes(g)
        rel = (lax.broadcasted_iota(I32, (nch, Q_BLOCK), 1)
               - (CMP_STRIDE * lax.broadcasted_iota(I32, (nch, Q_BLOCK), 0) + (CMP_LEN - 1))).astype(F32)
        for r in range(B_HPG):
            bias_ref[r] = slopes[r] * rel

    def compressed_branch(n):
        cidx = lax.broadcasted_iota(I32, (n, Q_BLOCK), 0)
        qidx = lax.broadcasted_iota(I32, (n, Q_BLOCK), 1)
        valid = (t0 + qidx) - (CMP_STRIDE * cidx + (CMP_LEN - 1)) >= 0
        st = jnp.dot(kc_ref[0, :n, :], _q_heads(qt_ref), preferred_element_type=F32)
        imp = jnp.zeros((n, Q_BLOCK), F32)
        outs = []
        for r in range(B_HPG):
            s = st[:, r * Q_BLOCK:(r + 1) * Q_BLOCK] - bias_ref[r, :n, :]
            s = jnp.where(valid, s, NEG_INF)
            m = jnp.maximum(jnp.max(s, axis=0, keepdims=True), M_FLOOR)
            p = jnp.exp2(s - m)
            l = jnp.sum(p, axis=0, keepdims=True)
            pn = p * (1.0 / jnp.maximum(l, TINY))
            imp = imp + pn
            o = jnp.dot(vct_ref[0, :, :n], pn.astype(BF16), preferred_element_type=F32)
            outs.append(o * gt_ref[pl.ds(g * B_HPG + r, 1), :])
        oc_ref[...] = jnp.concatenate(outs, axis=0).T
        hi = imp.astype(BF16)
        r1 = imp - hi.astype(F32)
        mid = r1.astype(BF16)
        lo = (r1 - mid.astype(F32)).astype(BF16)
        ns = n * n_slc // nch
        mt = mt_ref[:ns, :n]
        p_slc = (jnp.dot(mt, hi, preferred_element_type=F32) + jnp.dot(mt, mid, preferred_element_type=F32)
                 + jnp.dot(mt, lo, preferred_element_type=F32))
        sid = lax.broadcasted_iota(I32, (ns, Q_BLOCK), 0)
        cur = (t0 + lax.broadcasted_iota(I32, (ns, Q_BLOCK), 1)) // SLC_BLOCK
        forced = (sid == 0) | (sid == cur) | (sid == cur - 1)
        score = jnp.where(forced, -jnp.inf, p_slc)
        score = jnp.where(sid <= cur, score, -FORCE_SCORE)
        for _ in range(N_SELECT - N_FORCED):
            mx = jnp.max(score, axis=0, keepdims=True)
            first = jnp.min(jnp.where(score == mx, sid, ns), axis=0, keepdims=True)
            score = jnp.where(sid == first, -jnp.inf, score)
        sel = jnp.where((score == -jnp.inf) & (sid <= cur), 1.0, 0.0)
        sel_ref[0, 0, :ns, :] = sel
        flag_ref[0, 0, :ns, :] = jnp.max(sel, axis=1, keepdims=True).astype(I32)
        if ns < n_slc:
            sel_ref[0, 0, ns:, :] = jnp.zeros((n_slc - ns, Q_BLOCK), F32)
            flag_ref[0, 0, ns:, :] = jnp.zeros((n_slc - ns, 1), I32)

    n_parts = min(CMP_PARTS, n_slc // N_SELECT)
    part = nch // n_parts
    lax.switch(lax.div(8 * i + 6, part),
               [functools.partial(compressed_branch, part * (k + 1)) for k in range(n_parts)])


def _nsa_cmp(qt, gt, kv, seq):
    nb = seq // Q_BLOCK
    nch, n_slc = seq // CMP_STRIDE, seq // SLC_BLOCK
    gw = B_HPG * HEAD_DIM
    return pl.pallas_call(
        _nsa_cmp_kernel,
        grid=(B_KV_GROUPS, nb),
        in_specs=[pl.BlockSpec((gw, Q_BLOCK), lambda g, i: (g, i)),
                  pl.BlockSpec((1, nch, HEAD_DIM), lambda g, i: (g, 0, 0)),
                  pl.BlockSpec((1, HEAD_DIM, nch), lambda g, i: (g, 0, 0)),
                  pl.BlockSpec((LANES, Q_BLOCK), lambda g, i: (0, i)),
                  pl.BlockSpec((n_slc, nch), lambda g, i: (0, 0))],
        out_specs=[pl.BlockSpec((Q_BLOCK, gw), lambda g, i: (i, g)),
                   pl.BlockSpec((1, 1, n_slc, Q_BLOCK), lambda g, i: (g, i, 0, 0)),
                   pl.BlockSpec((1, 1, n_slc, 1), lambda g, i: (g, i, 0, 0))],
        out_shape=[jax.ShapeDtypeStruct((seq, D_MODEL), F32),
                   jax.ShapeDtypeStruct((B_KV_GROUPS, nb, n_slc, Q_BLOCK), F32),
                   jax.ShapeDtypeStruct((B_KV_GROUPS, nb, n_slc, 1), I32)],
        scratch_shapes=[pltpu.VMEM((B_HPG, nch, Q_BLOCK), F32)],
        compiler_params=_params(("arbitrary", "arbitrary")),
        name="nsa_compressed_select",
    )(qt, kv["kc"], kv["vct"], gt, _select_matrix(n_slc, nch))


def _fold8(x, op):
    return functools.reduce(op, [x[8 * k:8 * k + 8] for k in range(x.shape[0] // 8)])


def _softmax_tiles(sts, tiles, rel_bias, slopes, m_old):
    ps = [[] for _ in tiles]
    ms, ls, alphas = [], [], []
    for r in range(B_HPG):
        cols = slice(r * Q_BLOCK, (r + 1) * Q_BLOCK)
        us, shifts, tops = [], [], []
        for st, (mask, penalty, off) in zip(sts, tiles):
            u = st[:, cols] - rel_bias[r]
            u = u if mask is None else jnp.where(mask, u, NEG_INF)
            shift = slopes[r] * jnp.asarray(off, F32)
            shift = shift if penalty is None else shift + penalty
            us.append(u)
            shifts.append(shift)
            tops.append(_fold8(u, jnp.maximum) - shift)
        m_new = jnp.max(functools.reduce(jnp.maximum, tops), axis=0, keepdims=True)
        m_new = jnp.maximum(m_new, M_FLOOR)
        if m_old is not None:
            m_new = jnp.maximum(m_old[:, cols], m_new)
            alphas.append(jnp.exp2(m_old[:, cols] - m_new))
        l8 = jnp.zeros((8, Q_BLOCK), F32)
        for t, (u, shift) in enumerate(zip(us, shifts)):
            p = jnp.exp2(u - (m_new + shift))
            l8 = l8 + _fold8(p, jnp.add)
            ps[t].append(p.astype(BF16))
        ms.append(m_new)
        ls.append(jnp.sum(l8, axis=0, keepdims=True))
    ps = [jnp.concatenate(p, axis=1) for p in ps]
    alpha = None if m_old is None else jnp.concatenate(alphas, axis=1)
    return ps, jnp.concatenate(ms, axis=1), jnp.concatenate(ls, axis=1), alpha


def _nsa_sparse_kernel(lst_ref, qt_ref, ks_ref, vst_ref, kw_ref, vwt_ref, sel_ref, gt_ref, oc_ref, o_ref,
                       m_ref, alpha_ref, l_ref, accw_ref, acc_ref, st_ref, p_ref):
    g = pl.program_id(0)
    i = pl.program_id(1)
    t0 = i * Q_BLOCK
    slopes = _group_slopes(g)
    q4 = _q_heads(qt_ref)

    def gate(branch):
        return jnp.concatenate(
            [gt_ref[pl.ds(branch * N_HEADS + g * B_HPG + r, 1), :] for r in range(B_HPG)], axis=1)

    rel_w = (lax.broadcasted_iota(I32, (Q_BLOCK, Q_BLOCK), 1)
             - lax.broadcasted_iota(I32, (Q_BLOCK, Q_BLOCK), 0))
    n_prev = WIN // Q_BLOCK
    rel_w_f = rel_w.astype(F32)
    bias_w = [slopes[r] * rel_w_f for r in range(B_HPG)]
    tiles, blks = [], []
    for c in range(n_prev + 1):
        blk = i - n_prev + c
        off = (n_prev - c) * Q_BLOCK
        mask = (rel_w >= 0) if c == n_prev else (rel_w < 0) if c == 0 else None
        penalty = None if c == n_prev else jnp.where(blk >= 0, 0.0, MASK_PENALTY).astype(F32)
        tiles.append((mask, penalty, off))
        blks.append(jnp.maximum(blk, 0))
    sts = [jnp.dot(kw_ref[0, blk], q4, preferred_element_type=F32) for blk in blks]
    ps, _, l_w, _ = _softmax_tiles(sts, tiles, bias_w, slopes, None)
    acc_w = functools.reduce(lambda a, b: a + b, [
        jnp.dot(vwt_ref[0, blk], p, preferred_element_type=F32) for blk, p in zip(blks, ps)])
    accw_ref[...] = acc_w * (gate(2) / jnp.maximum(l_w, TINY))

    rel_s = (lax.broadcasted_iota(I32, (SLC_BLOCK, Q_BLOCK), 1)
             - lax.broadcasted_iota(I32, (SLC_BLOCK, Q_BLOCK), 0))
    rel_s_f = rel_s.astype(F32)
    bias_s = [slopes[r] * rel_s_f for r in range(B_HPG)]

    def block_id(j, b):
        return lst_ref[0, 0, j * SEL_CHUNK + b]

    def rows(b):
        return slice(b * SLC_BLOCK, (b + 1) * SLC_BLOCK)

    def scores(j, slot):
        for b in range(SEL_CHUNK):
            st_ref[slot, rows(b), :] = jnp.dot(ks_ref[0, block_id(j, b)], q4, preferred_element_type=F32)

    def weighted_values(j):
        v = jnp.concatenate([vst_ref[0, block_id(j, b)] for b in range(SEL_CHUNK)], axis=0)
        return lax.dot_general(v, p_ref[...], (((0,), (0,)), ((), ())), preferred_element_type=F32)

    m_ref[...] = jnp.full(m_ref.shape, M_FLOOR, F32)
    alpha_ref[...] = jnp.ones(alpha_ref.shape, F32)
    l_ref[...] = jnp.zeros(l_ref.shape, F32)
    acc_ref[...] = jnp.zeros(acc_ref.shape, F32)
    p_ref[...] = jnp.zeros(p_ref.shape, BF16)
    scores(0, 0)

    def sel_chunk(j, slot):
        acc_ref[...] = acc_ref[...] * alpha_ref[0:1, :] + weighted_values(jnp.maximum(j - 1, 0))
        tiles, sts = [], []
        for b in range(SEL_CHUNK):
            blk = block_id(j, b)
            off = t0 - blk * SLC_BLOCK
            penalty = jnp.where(sel_ref[0, 0, pl.ds(blk, 1), :] > 0.5, 0.0, MASK_PENALTY)
            mask = (rel_s >= -off) if b < 2 else None
            tiles.append((mask, penalty, off))
            sts.append(st_ref[slot, rows(b), :])
        ps, m, l, alpha = _softmax_tiles(sts, tiles, bias_s, slopes, m_ref[0:1, :])
        for b in range(SEL_CHUNK):
            p_ref[rows(b), :] = ps[b]
        m_ref[0:1, :] = m
        l_ref[0:1, :] = alpha * l_ref[0:1, :] + l
        alpha_ref[0:1, :] = alpha
        scores(j + 1, 1 - slot)

    def sel_pair(jj, c):
        sel_chunk(2 * jj, 0)
        sel_chunk(2 * jj + 1, 1)
        return c

    n_pairs = lst_ref[0, 0, LIST_W - 1]
    lax.fori_loop(0, n_pairs, sel_pair, 0)
    acc_s = acc_ref[...] * alpha_ref[0:1, :] + weighted_values(jnp.maximum(2 * n_pairs - 1, 0))

    total = accw_ref[...] + acc_s * (gate(1) / jnp.maximum(l_ref[0:1, :], TINY))
    ot = jnp.concatenate([total[:, r * Q_BLOCK:(r + 1) * Q_BLOCK] for r in range(B_HPG)], axis=0)
    o_ref[...] = (ot.T + oc_ref[...]).astype(BF16)


def _nsa_sparse(lists, qt, gt, kv, sel, oc, seq):
    nb = seq // Q_BLOCK
    n_slc, n_win = seq // SLC_BLOCK, seq // Q_BLOCK
    gw = B_HPG * HEAD_DIM
    grp = lambda g, i: (g, 0, 0, 0)
    return pl.pallas_call(
        _nsa_sparse_kernel,
        grid=(B_KV_GROUPS, nb),
        in_specs=[pl.BlockSpec((1, 1, LIST_W), lambda g, i: (g * nb + i, 0, 0), memory_space=pltpu.SMEM),
                  pl.BlockSpec((gw, Q_BLOCK), lambda g, i: (g, i)),
                  pl.BlockSpec((1, n_slc, SLC_BLOCK, HEAD_DIM), grp),
                  pl.BlockSpec((1, n_slc, SLC_BLOCK, HEAD_DIM), grp),
                  pl.BlockSpec((1, n_win, Q_BLOCK, HEAD_DIM), grp),
                  pl.BlockSpec((1, n_win, HEAD_DIM, Q_BLOCK), grp),
                  pl.BlockSpec((1, 1, n_slc, Q_BLOCK), lambda g, i: (g, i, 0, 0)),
                  pl.BlockSpec((LANES, Q_BLOCK), lambda g, i: (0, i)),
                  pl.BlockSpec((Q_BLOCK, gw), lambda g, i: (i, g))],
        out_specs=pl.BlockSpec((Q_BLOCK, gw), lambda g, i: (i, g)),
        scratch_shapes=[pltpu.VMEM((8, B_HPG * Q_BLOCK), F32),
                        pltpu.VMEM((8, B_HPG * Q_BLOCK), F32),
                        pltpu.VMEM((8, B_HPG * Q_BLOCK), F32),
                        pltpu.VMEM((HEAD_DIM, B_HPG * Q_BLOCK), F32),
                        pltpu.VMEM((HEAD_DIM, B_HPG * Q_BLOCK), F32),
                        pltpu.VMEM((2, SEL_CHUNK * SLC_BLOCK, B_HPG * Q_BLOCK), F32),
                        pltpu.VMEM((SEL_CHUNK * SLC_BLOCK, B_HPG * Q_BLOCK), BF16)],
        out_shape=jax.ShapeDtypeStruct((seq, D_MODEL), BF16),
        compiler_params=_params(("arbitrary", "arbitrary")),
        name="nsa_selected_window",
    )(lists, qt, kv["ks"], kv["vst"], kv["kw"], kv["vwt"], sel, gt, oc)


def _outproj_b_kernel(o_ref, h_ref, w_ref, g_ref, b_ref, hn_ref, hb_ref):
    mix = jnp.dot(o_ref[...], w_ref[...], preferred_element_type=F32)
    _layer_norm_store(DEEPNORM_ALPHA * h_ref[...] + mix, g_ref, b_ref, hn_ref, hb_ref)


def _outproj_b(o, h, w_out, ln_g, ln_b, tm=512):
    seq = h.shape[0]
    row = lambda i: (i, 0)
    fix = lambda i: (0, 0)
    return pl.pallas_call(
        _outproj_b_kernel,
        grid=(seq // tm,),
        in_specs=[pl.BlockSpec((tm, D_MODEL), row), pl.BlockSpec((tm, D_MODEL), row),
                  pl.BlockSpec((D_MODEL, D_MODEL), fix), pl.BlockSpec((1, D_MODEL), fix),
                  pl.BlockSpec((1, D_MODEL), fix)],
        out_specs=[pl.BlockSpec((tm, D_MODEL), row)] * 2,
        out_shape=[jax.ShapeDtypeStruct((seq, D_MODEL), F32), jax.ShapeDtypeStruct((seq, D_MODEL), BF16)],
        compiler_params=_params(("arbitrary",)),
        name="outproj_ln_b",
    )(o, h, w_out.astype(BF16), ln_g.reshape(1, -1), ln_b.reshape(1, -1))


def _nsa_layer(h, hb, kv, w_q, b_q, w_out, ln_g, ln_b):
    seq = h.shape[0]
    nb, n_slc = seq // Q_BLOCK, seq // SLC_BLOCK
    qt, gt = _qproj(hb, w_q, b_q)
    oc, sel, flags = _nsa_cmp(qt, gt, kv, seq)
    f = flags.reshape(B_KV_GROUPS, nb, n_slc)
    newest = 2 * jnp.arange(nb, dtype=I32)[None, :, None] + 1
    age = jnp.mod(newest - jnp.arange(n_slc, dtype=I32)[None, None, :], n_slc)
    order = jnp.argsort((1 - f) * n_slc + age, axis=2).astype(I32).reshape(B_KV_GROUPS * nb, n_slc)
    n_pairs = (jnp.sum(f, axis=2).reshape(-1) + 2 * SEL_CHUNK - 1) // (2 * SEL_CHUNK)
    lists = jnp.concatenate(
        [order, jnp.zeros((B_KV_GROUPS * nb, LIST_W - n_slc - 1), I32), n_pairs.astype(I32)[:, None]], axis=1)
    o = _nsa_sparse(lists.reshape(B_KV_GROUPS * nb, 1, LIST_W), qt, gt, kv, sel, oc, seq)
    return _outproj_b(o, h, w_out, ln_g, ln_b)


def kernel(x, a_w_in, a_w_out, b_w_kv, b_cmp_pos, b_cmp_w1, b_cmp_b1, b_cmp_w2, b_cmp_b2, b_w_q, b_b_q, b_w_out, moe_w_router, moe_b_router, moe_w_gate_up, moe_w_down, ln_mix_g, ln_mix_b, ln_ffn_g, ln_ffn_b):
    seq = x.shape[1]
    h = x.reshape(seq, D_MODEL)
    hb = h.astype(BF16)
    a_w_in_b, w_gate_up_b, w_down_b = a_w_in.astype(BF16), moe_w_gate_up.astype(BF16), moe_w_down.astype(BF16)
    buf = None
    for layer in range(DEPTH):
        if layer < N_A_LAYERS:
            h, hb = _dilated_layer(h, hb, a_w_in_b, layer, a_w_out[layer], ln_mix_g[layer], ln_mix_b[layer])
        else:
            if layer == N_A_LAYERS:
                kv = _nsa_shared_kv(hb, b_w_kv, b_cmp_pos, b_cmp_w1, b_cmp_b1, b_cmp_w2, b_cmp_b2)
            j = layer - N_A_LAYERS
            h, hb = _nsa_layer(h, hb, kv, b_w_q[j], b_b_q[j], b_w_out[j], ln_mix_g[layer], ln_mix_b[layer])
        h, hb, buf = _moe_layer(h, moe_w_router[layer], moe_b_router[layer], w_gate_up_b, w_down_b, layer,
                                ln_ffn_g[layer], ln_ffn_b[layer], buf)
    return h.reshape(1, seq, D_MODEL)
```

```python
import functools
import math

import numpy as np
import jax
import jax.numpy as jnp
from jax import lax
from jax.experimental import pallas as pl
from jax.experimental.pallas import tpu as pltpu

F32 = jnp.float32
BF16 = jnp.bfloat16
I32 = jnp.int32

D_MODEL = 1024
DEPTH = 4
N_A_LAYERS = DEPTH // 2
HEAD_DIM = 64
N_HEADS = D_MODEL // HEAD_DIM
Q_BLOCK = 128
LN_EPS = 1e-5
NEG_INF = -1e30
TINY = 1e-30

A_WINDOWS = (128, 512, 2048)
A_DILATIONS = (1, 4, 16)
A_N_GROUPS = 3
DIL_STEP_BLOCKS = 4

N_EXPERT_GROUPS = 4
EXPERTS_PER_GROUP = 4
N_EXPERTS = 16
D_EXPERT = 256

DEEPNORM_ALPHA = (2.0 * DEPTH) ** 0.25

LANES = 128
MOE_TILE = 256
VMEM_LIMIT = 56 * 1024 * 1024


def _slopes(n):
    return [float(v) for v in np.asarray(2.0 ** (-8.0 * np.arange(1, n + 1) / n), dtype=np.float32)]


def _params(sem):
    return pltpu.CompilerParams(dimension_semantics=sem, vmem_limit_bytes=VMEM_LIMIT)


def _mm_kernel(x_ref, w_ref, o_ref):
    o_ref[...] = jnp.dot(x_ref[...], w_ref[...], preferred_element_type=F32).astype(o_ref.dtype)


def _matmul(x, w, out_dtype, tm, tn, name):
    m, k = x.shape
    n = w.shape[1]
    return pl.pallas_call(
        _mm_kernel,
        grid=(n // tn, m // tm),
        in_specs=[pl.BlockSpec((tm, k), lambda j, i: (i, 0)),
                  pl.BlockSpec((k, tn), lambda j, i: (0, j))],
        out_specs=pl.BlockSpec((tm, tn), lambda j, i: (i, j)),
        out_shape=jax.ShapeDtypeStruct((m, n), out_dtype),
        compiler_params=_params(("arbitrary", "arbitrary")),
        name=name,
    )(x, w)


def _layer_norm_store(z, g_ref, b_ref, h_ref, hb_ref):
    mu = jnp.mean(z, axis=-1, keepdims=True)
    zc = z - mu
    var = jnp.mean(zc * zc, axis=-1, keepdims=True)
    y = zc * lax.rsqrt(var + LN_EPS) * g_ref[...] + b_ref[...]
    h_ref[...] = y
    hb_ref[...] = y.astype(BF16)


def _split_bf16(x):
    hi = x.astype(BF16)
    lo = (x - hi.astype(F32)).astype(BF16)
    return hi, lo


def _dil_attn_kernel(q_ref, kp_ref, kc_ref, vp_ref, vc_ref, o_ref, lse_ref, bias_ref, *, dil, slopes):
    r = pl.program_id(0)
    jb = pl.program_id(1)
    qb = Q_BLOCK

    @pl.when((r == 0) & (jb == 0))
    def _():
        key = lax.broadcasted_iota(I32, (2 * qb, qb), 0)
        qry = lax.broadcasted_iota(I32, (2 * qb, qb), 1)
        dist = qry + qb - key
        valid = (dist >= 0) & (dist <= qb)
        dist_tok = (dist * dil).astype(F32)
        for h in range(N_HEADS):
            b = jnp.where(valid, -(slopes[h] * LOG2E * dist_tok), NEG_INF)
            bias_ref[1, h] = b
            bias_ref[0, h] = jnp.where(key >= qb, b, NEG_INF)

    lane = lax.broadcasted_iota(I32, (1, LANES), 1)
    lo_half = lane < HEAD_DIM
    lo_rows = lax.broadcasted_iota(I32, (LANES, qb), 0) < HEAD_DIM
    for sub in range(DIL_STEP_BLOCKS):
        rows = slice(sub * qb, (sub + 1) * qb)
        before = slice((sub - 1) * qb, sub * qb)
        first = jnp.minimum(jb, 1) if sub == 0 else 1
        lses = []
        for hp in range(N_HEADS // 2):
            sl = slice(hp * LANES, (hp + 1) * LANES)
            q2 = q_ref[rows, sl]
            k_prev = kp_ref[:, sl] if sub == 0 else kc_ref[before, sl]
            v_prev = vp_ref[:, sl] if sub == 0 else vc_ref[before, sl]
            k2 = jnp.concatenate([k_prev, kc_ref[rows, sl]], axis=0)
            v2t = jnp.concatenate([v_prev, vc_ref[rows, sl]], axis=0).T
            outs = []
            for half in range(2):
                keep = lo_half if half == 0 else jnp.logical_not(lo_half)
                qm = jnp.where(keep, q2, jnp.zeros_like(q2))
                st = lax.dot_general(k2, qm, (((1,), (1,)), ((), ())), preferred_element_type=F32)
                st = st + bias_ref[first, 2 * hp + half]
                m = jnp.max(st, axis=0, keepdims=True)
                p = jnp.exp2(st - m)
                l = jnp.sum(p, axis=0, keepdims=True)
                ot = jnp.dot(v2t, p.astype(BF16), preferred_element_type=F32)
                outs.append(ot * (1.0 / l))
                lses.append(m + jnp.log2(l))
            o_ref[rows, sl] = jnp.where(lo_rows, outs[0], outs[1]).T.astype(BF16)
        lse_t = jnp.concatenate(lses + [jnp.zeros((LANES - N_HEADS, qb), F32)], axis=0)
        lse_ref[rows, :] = lse_t.T


def _qkv_kernel(x_ref, w_ref, o_ref, acc_ref, *, dil):
    acc = jnp.dot(x_ref[...], w_ref[...], preferred_element_type=F32)
    acc = acc * jnp.where(pl.program_id(0) == 0, HEAD_DIM ** -0.5 * LOG2E, 1.0)
    if dil == 1:
        o_ref[0] = acc.astype(BF16)
    else:
        n_chunks = acc.shape[1] // LANES
        for c in range(n_chunks):
            acc_ref[c] = acc[:, c * LANES:(c + 1) * LANES]
        n = acc.shape[0] // dil
        for r in range(dil):
            o_ref[r] = jnp.concatenate(
                [acc_ref[c, pl.ds(r, n, stride=dil), :] for c in range(n_chunks)], axis=1).astype(BF16)


def _qkv_group(hb, w_in, layer, g, tm=1024):
    seq = hb.shape[0]
    dil = A_DILATIONS[g]
    return pl.pallas_call(
        functools.partial(_qkv_kernel, dil=dil),
        grid=(3, seq // tm),
        in_specs=[pl.BlockSpec((tm, D_MODEL), lambda c, i: (i, 0)),
                  pl.BlockSpec((None, D_MODEL, D_MODEL), lambda c, i: (layer, 0, 3 * g + c))],
        out_specs=pl.BlockSpec((dil, tm // dil, D_MODEL), lambda c, i: (0, i, c)),
        out_shape=jax.ShapeDtypeStruct((dil, seq // dil, 3 * D_MODEL), BF16),
        scratch_shapes=[pltpu.VMEM((D_MODEL // LANES, tm, LANES), F32)],
        compiler_params=_params(("arbitrary", "arbitrary")),
        name=f"qkv_proj_g{g}",
    )(hb, w_in)


def _dilated_group(qkv_g, g, seq):
    dil = A_DILATIONS[g]
    rows = seq // dil
    nblk = rows // Q_BLOCK
    nq = DIL_STEP_BLOCKS * Q_BLOCK
    prev = (None, Q_BLOCK, D_MODEL)
    blk = (None, nq, D_MODEL)
    kern = functools.partial(_dil_attn_kernel, dil=dil, slopes=_slopes(N_HEADS))
    return pl.pallas_call(
        kern,
        grid=(dil, nblk // DIL_STEP_BLOCKS),
        in_specs=[
            pl.BlockSpec(blk, lambda r, j: (r, j, 0)),
            pl.BlockSpec(prev, lambda r, j: (r, jnp.maximum(DIL_STEP_BLOCKS * j - 1, 0), 1)),
            pl.BlockSpec(blk, lambda r, j: (r, j, 1)),
            pl.BlockSpec(prev, lambda r, j: (r, jnp.maximum(DIL_STEP_BLOCKS * j - 1, 0), 2)),
            pl.BlockSpec(blk, lambda r, j: (r, j, 2)),
        ],
        out_specs=[pl.BlockSpec(blk, lambda r, j: (r, j, 0)),
                   pl.BlockSpec((None, nq, LANES), lambda r, j: (r, j, 0))],
        out_shape=[jax.ShapeDtypeStruct((dil, rows, D_MODEL), BF16),
                   jax.ShapeDtypeStruct((dil, rows, LANES), F32)],
        scratch_shapes=[pltpu.VMEM((2, N_HEADS, 2 * Q_BLOCK, Q_BLOCK), F32)],
        compiler_params=_params(("arbitrary", "arbitrary")),
        name=f"dilated_attn_g{g}",
    )(qkv_g, qkv_g, qkv_g, qkv_g, qkv_g)


def _head_expand_matrix():
    h = np.arange(LANES)[:, None]
    c = np.arange(D_MODEL)[None, :]
    return jnp.asarray((c // HEAD_DIM == h).astype(np.float32), BF16)


def _to_token_order(src_ref, dst_ref, dil):
    n = src_ref.shape[1]
    n_chunks = dst_ref.shape[0]
    if dil == 1:
        return src_ref[0].astype(F32)
    for r in range(dil):
        v = src_ref[r].astype(F32)
        for c in range(n_chunks):
            dst_ref[c, pl.ds(r, n, stride=dil), :] = v[:, c * LANES:(c + 1) * LANES]
    return jnp.concatenate([dst_ref[c] for c in range(n_chunks)], axis=1)


def _outproj_a_kernel(o0_ref, o1_ref, o2_ref, l0_ref, l1_ref, l2_ref, h_ref, w_ref, e_ref, g_ref, b_ref,
                      hn_ref, hb_ref, onat_ref, lnat_ref):
    l0, l1, l2 = [_to_token_order(l_ref, lnat_ref, A_DILATIONS[gi])
                  for gi, l_ref in enumerate((l0_ref, l1_ref, l2_ref))]
    m = jnp.maximum(jnp.maximum(l0, l1), l2)
    e0, e1, e2 = jnp.exp2(l0 - m), jnp.exp2(l1 - m), jnp.exp2(l2 - m)
    inv = 1.0 / (e0 + e1 + e2)
    o = jnp.zeros(h_ref.shape, F32)
    for gi, (e, o_ref) in enumerate(((e0, o0_ref), (e1, o1_ref), (e2, o2_ref))):
        hi, lo = _split_bf16(e * inv)
        wexp = (jnp.dot(hi, e_ref[...], preferred_element_type=F32)
                + jnp.dot(lo, e_ref[...], preferred_element_type=F32))
        o = o + wexp * _to_token_order(o_ref, onat_ref, A_DILATIONS[gi])
    mix = jnp.dot(o.astype(BF16), w_ref[...], preferred_element_type=F32)
    _layer_norm_store(DEEPNORM_ALPHA * h_ref[...] + mix, g_ref, b_ref, hn_ref, hb_ref)


def _outproj_a(outs, lses, h, w_out, ln_g, ln_b, tm=512):
    seq = h.shape[0]
    row = lambda i: (i, 0)
    fix = lambda i: (0, 0)
    cls = lambda i: (0, i, 0)
    return pl.pallas_call(
        _outproj_a_kernel,
        grid=(seq // tm,),
        in_specs=[pl.BlockSpec((d, tm // d, D_MODEL), cls) for d in A_DILATIONS]
        + [pl.BlockSpec((d, tm // d, LANES), cls) for d in A_DILATIONS] + [
            pl.BlockSpec((tm, D_MODEL), row),
            pl.BlockSpec((D_MODEL, D_MODEL), fix),
            pl.BlockSpec((LANES, D_MODEL), fix),
            pl.BlockSpec((1, D_MODEL), fix),
            pl.BlockSpec((1, D_MODEL), fix)],
        out_specs=[pl.BlockSpec((tm, D_MODEL), row)] * 2,
        out_shape=[jax.ShapeDtypeStruct((seq, D_MODEL), F32), jax.ShapeDtypeStruct((seq, D_MODEL), BF16)],
        scratch_shapes=[pltpu.VMEM((D_MODEL // LANES, tm, LANES), F32), pltpu.VMEM((1, tm, LANES), F32)],
        compiler_params=_params(("arbitrary",)),
        name="outproj_ln_a",
    )(*outs, *lses, h, w_out.astype(BF16), _head_expand_matrix(), ln_g.reshape(1, -1), ln_b.reshape(1, -1))


def _dilated_layer(h, hb, w_in, layer, w_out, ln_g, ln_b):
    seq = h.shape[0]
    outs, lses = [], []
    for g in range(A_N_GROUPS):
        qkv_g = _qkv_group(hb, w_in, layer, g)
        o, lse = _dilated_group(qkv_g, g, seq)
        outs.append(o)
        lses.append(lse)
    return _outproj_a(outs, lses, h, w_out, ln_g, ln_b)


ROUTER_ROWS = 24


def _router_kernel(h_ref, w_ref, b_ref, ids_ref, wts_ref, cnt_ref, base_ref):
    i = pl.program_id(0)
    tm = h_ref.shape[0]

    @pl.when(i == 0)
    def _():
        base_ref[...] = jnp.zeros_like(base_ref)

    lt = lax.dot_general(w_ref[...], h_ref[...], (((1,), (1,)), ((), ())),
                         precision=lax.Precision.HIGHEST, preferred_element_type=F32)
    lt = lt + b_ref[:, 0:1]
    gl = [lt[k:k + 1, :] for k in range(N_EXPERT_GROUPS)]
    best, gidx = gl[0], jnp.zeros((1, tm), I32)
    for k in range(1, N_EXPERT_GROUPS):
        take = gl[k] > best
        best = jnp.where(take, gl[k], best)
        gidx = jnp.where(take, k, gidx)
    denom = gl[0] * 0.0
    for k in range(N_EXPERT_GROUPS):
        denom = denom + jnp.exp(gl[k] - best)
    g_w = 1.0 / denom
    e_in = []
    for e in range(EXPERTS_PER_GROUP):
        v = lt[N_EXPERT_GROUPS + e:N_EXPERT_GROUPS + e + 1, :]
        for k in range(1, N_EXPERT_GROUPS):
            row = N_EXPERT_GROUPS + k * EXPERTS_PER_GROUP + e
            v = jnp.where(gidx == k, lt[row:row + 1, :], v)
        e_in.append(v)
    v1, i1 = e_in[0], jnp.zeros((1, tm), I32)
    for e in range(1, EXPERTS_PER_GROUP):
        take = e_in[e] > v1
        v1 = jnp.where(take, e_in[e], v1)
        i1 = jnp.where(take, e, i1)
    v2, i2 = jnp.full((1, tm), -jnp.inf, F32), jnp.zeros((1, tm), I32)
    for e in range(EXPERTS_PER_GROUP):
        take = (e_in[e] > v2) & (i1 != e)
        v2 = jnp.where(take, e_in[e], v2)
        i2 = jnp.where(take, e, i2)
    t = jnp.exp(v2 - v1)
    w1 = g_w / (1.0 + t)
    w2 = g_w * t / (1.0 + t)
    e1 = gidx * EXPERTS_PER_GROUP + i1
    e2 = gidx * EXPERTS_PER_GROUP + i2

    erow = lax.broadcasted_iota(I32, (N_EXPERTS, tm), 0)
    oh1 = (erow == e1).astype(F32)
    oh2 = (erow == e2).astype(F32)
    c = oh1 + oh2
    tri = (lax.broadcasted_iota(I32, (tm, tm), 0) < lax.broadcasted_iota(I32, (tm, tm), 1)).astype(BF16)
    excl = jnp.dot(c.astype(BF16), tri, preferred_element_type=F32) + base_ref[...]
    r1 = jnp.sum(oh1 * excl, axis=0, keepdims=True).astype(I32)
    r2 = jnp.sum(oh2 * excl, axis=0, keepdims=True).astype(I32)
    base_ref[...] = base_ref[...] + jnp.sum(c, axis=1, keepdims=True)

    zi = jnp.zeros((1, tm), I32)
    ids_ref[...] = jnp.concatenate([e1, e2, r1, r2, zi, zi, zi, zi], axis=0)
    zf = jnp.zeros((1, tm), F32)
    wts_ref[...] = jnp.concatenate([w1, w2, zf, zf, zf, zf, zf, zf], axis=0)
    cnt_ref[...] = jnp.broadcast_to(base_ref[...], cnt_ref.shape)


def _router(h, w_router, b_router, tm=512):
    seq = h.shape[0]
    n_log = N_EXPERT_GROUPS + N_EXPERTS
    w_t = jnp.zeros((ROUTER_ROWS, D_MODEL), F32).at[:n_log].set(w_router.T)
    b_t = jnp.zeros((ROUTER_ROWS, LANES), F32).at[:n_log].set(jnp.broadcast_to(b_router[:, None], (n_log, LANES)))
    return pl.pallas_call(
        _router_kernel,
        grid=(seq // tm,),
        in_specs=[pl.BlockSpec((tm, D_MODEL), lambda i: (i, 0)),
                  pl.BlockSpec((ROUTER_ROWS, D_MODEL), lambda i: (0, 0)),
                  pl.BlockSpec((ROUTER_ROWS, LANES), lambda i: (0, 0))],
        out_specs=[pl.BlockSpec((8, tm), lambda i: (0, i)),
                   pl.BlockSpec((8, tm), lambda i: (0, i)),
                   pl.BlockSpec((N_EXPERTS, LANES), lambda i: (0, 0))],
        out_shape=[jax.ShapeDtypeStruct((8, seq), I32), jax.ShapeDtypeStruct((8, seq), F32),
                   jax.ShapeDtypeStruct((N_EXPERTS, LANES), F32)],
        scratch_shapes=[pltpu.VMEM((N_EXPERTS, 1), F32)],
        compiler_params=_params(("arbitrary",)),
        name="moe_router",
    )(h, w_t, b_t)


ROW_TILE = D_MODEL // LANES


def _rows_to_tiles(x, dst_ref):
    n = x.shape[0]
    for c in range(ROW_TILE):
        dst_ref[pl.ds(c, n, stride=ROW_TILE), :] = x[:, c * LANES:(c + 1) * LANES]


def _tiles_to_rows(src_ref):
    n = src_ref.shape[0] // ROW_TILE
    return jnp.concatenate([src_ref[pl.ds(c, n, stride=ROW_TILE), :] for c in range(ROW_TILE)], axis=1)


def _row_tile(ref, p):
    return ref.at[pl.ds(pl.multiple_of(p * ROW_TILE, ROW_TILE), ROW_TILE)]


def _dispatch_kernel(pos_ref, h_ref, xs_in_ref, xs_ref, ht_ref, sem):
    del xs_in_ref
    tm = h_ref.shape[0]
    _rows_to_tiles(h_ref[...], ht_ref)

    def row_copy(i, p):
        return pltpu.make_async_copy(_row_tile(ht_ref, i), _row_tile(xs_ref, p), sem)

    def start(i, c):
        row_copy(i, pos_ref[0, 0, i]).start(priority=0)
        row_copy(i, pos_ref[0, 0, tm + i]).start(priority=1)
        return c

    def wait(i, c):
        row_copy(0, 0).wait()
        row_copy(0, 0).wait()
        return c

    lax.fori_loop(0, tm, start, 0, unroll=8)
    lax.fori_loop(0, tm, wait, 0, unroll=8)


def _dispatch(h, pos, buf, tm=256):
    seq = h.shape[0]
    nt = seq // tm
    pos3 = pos.reshape(2, nt, tm).transpose(1, 0, 2).reshape(nt, 1, 2 * tm)
    return pl.pallas_call(
        _dispatch_kernel,
        grid=(nt,),
        in_specs=[pl.BlockSpec((1, 1, 2 * tm), lambda i: (i, 0, 0), memory_space=pltpu.SMEM),
                  pl.BlockSpec((tm, D_MODEL), lambda i: (i, 0)),
                  pl.BlockSpec(memory_space=pl.ANY)],
        out_specs=pl.BlockSpec(memory_space=pl.ANY),
        out_shape=jax.ShapeDtypeStruct(buf.shape, F32),
        scratch_shapes=[pltpu.VMEM((tm * ROW_TILE, LANES), F32), pltpu.SemaphoreType.DMA(())],
        input_output_aliases={2: 0},
        compiler_params=_params(("arbitrary",)),
        name="moe_dispatch",
    )(pos3, h, buf)


def _expert_kernel(te_ref, x_ref, wgu_ref, wd_ref, y_ref):
    del te_ref
    gu = jnp.dot(_tiles_to_rows(x_ref).astype(BF16), wgu_ref[...], preferred_element_type=F32)
    gate, up = gu[:, :D_EXPERT], gu[:, D_EXPERT:]
    hid = gate * (1.0 / (1.0 + jnp.exp(-gate))) * up
    _rows_to_tiles(jnp.dot(hid.astype(BF16), wd_ref[...], preferred_element_type=F32), y_ref)


def _experts(xs, tile_expert, w_gate_up, w_down, layer):
    blk = MOE_TILE * ROW_TILE
    return pl.pallas_call(
        _expert_kernel,
        grid_spec=pltpu.PrefetchScalarGridSpec(
            num_scalar_prefetch=1,
            grid=(xs.shape[0] // blk,),
            in_specs=[pl.BlockSpec((blk, LANES), lambda i, te: (i, 0)),
                      pl.BlockSpec((None, None, D_MODEL, 2 * D_EXPERT), lambda i, te: (layer, te[i], 0, 0)),
                      pl.BlockSpec((None, None, D_EXPERT, D_MODEL), lambda i, te: (layer, te[i], 0, 0))],
            out_specs=pl.BlockSpec((blk, LANES), lambda i, te: (i, 0))),
        out_shape=jax.ShapeDtypeStruct(xs.shape, F32),
        compiler_params=_params(("arbitrary",)),
        name="moe_experts",
    )(tile_expert, xs, w_gate_up, w_down)


def _combine_kernel(pos_ref, pos_next_ref, ys_ref, h_ref, w_ref, g_ref, b_ref, hn_ref, hb_ref, buf_ref, sem):
    step = pl.program_id(0)
    tm = h_ref.shape[0]
    slot = lax.rem(step, 2)

    def row_copy(s, k, i, p):
        return pltpu.make_async_copy(_row_tile(ys_ref, p), _row_tile(buf_ref.at[s, k], i), sem.at[s])

    def gather(p_ref, s):
        def start(i, c):
            row_copy(s, 0, i, p_ref[0, 0, i]).start(priority=0)
            row_copy(s, 1, i, p_ref[0, 0, tm + i]).start(priority=1)
            return c
        lax.fori_loop(0, tm, start, 0, unroll=8)

    @pl.when(step == 0)
    def _():
        gather(pos_ref, 0)

    @pl.when(step + 1 < pl.num_programs(0))
    def _():
        gather(pos_next_ref, 1 - slot)

    def wait(i, c):
        row_copy(slot, 0, 0, 0).wait()
        row_copy(slot, 1, 0, 0).wait()
        return c

    lax.fori_loop(0, tm, wait, 0, unroll=8)
    w = w_ref[...]
    ffn = (w[:, 0:1] * _tiles_to_rows(buf_ref.at[slot, 0])
           + w[:, 1:2] * _tiles_to_rows(buf_ref.at[slot, 1]))
    _layer_norm_store(DEEPNORM_ALPHA * h_ref[...] + ffn, g_ref, b_ref, hn_ref, hb_ref)


def _combine(ys, pos, wts, h, ln_g, ln_b, tm=256):
    seq = h.shape[0]
    nt = seq // tm
    pos3 = pos.reshape(2, nt, tm).transpose(1, 0, 2).reshape(nt, 1, 2 * tm)
    row = lambda i: (i, 0)
    fix = lambda i: (0, 0)
    return pl.pallas_call(
        _combine_kernel,
        grid=(nt,),
        in_specs=[pl.BlockSpec((1, 1, 2 * tm), lambda i: (i, 0, 0), memory_space=pltpu.SMEM),
                  pl.BlockSpec((1, 1, 2 * tm), lambda i: (jnp.minimum(i + 1, nt - 1), 0, 0),
                               memory_space=pltpu.SMEM),
                  pl.BlockSpec(memory_space=pl.ANY),
                  pl.BlockSpec((tm, D_MODEL), row),
                  pl.BlockSpec((tm, 2), row),
                  pl.BlockSpec((1, D_MODEL), fix),
                  pl.BlockSpec((1, D_MODEL), fix)],
        out_specs=[pl.BlockSpec((tm, D_MODEL), row)] * 2,
        out_shape=[jax.ShapeDtypeStruct((seq, D_MODEL), F32), jax.ShapeDtypeStruct((seq, D_MODEL), BF16)],
        scratch_shapes=[pltpu.VMEM((2, 2, tm * ROW_TILE, LANES), F32), pltpu.SemaphoreType.DMA((2,))],
        compiler_params=_params(("arbitrary",)),
        name="moe_combine_ln",
    )(pos3, pos3, ys, h, wts, ln_g.reshape(1, -1), ln_b.reshape(1, -1))


def _moe_layer(h, w_router, b_router, w_gate_up, w_down, layer, ln_g, ln_b, buf):
    seq = h.shape[0]
    ids, wts, cnt = _router(h, w_router, b_router)
    counts = cnt[:, 0].astype(I32)
    padded = ((counts + MOE_TILE - 1) // MOE_TILE) * MOE_TILE
    ends = jnp.cumsum(padded)
    offs = ends - padded
    eids = jnp.arange(N_EXPERTS, dtype=I32)[:, None, None]
    pos = jnp.sum(jnp.where(ids[None, 0:2] == eids, offs[:, None, None], 0), axis=0) + ids[2:4]
    n_rows = 2 * seq + N_EXPERTS * MOE_TILE
    tile_start = jnp.arange(n_rows // MOE_TILE, dtype=I32) * MOE_TILE
    tile_expert = jnp.minimum(jnp.sum((tile_start[:, None] >= ends[None, :]).astype(I32), axis=1), N_EXPERTS - 1)
    if buf is None:
        buf = jnp.zeros((n_rows * ROW_TILE, LANES), F32)
    xs = _dispatch(h, pos, buf)
    ys = _experts(xs, tile_expert.astype(I32), w_gate_up, w_down, layer)
    return _combine(ys, pos, wts[0:2].T, h, ln_g, ln_b) + (ys,)


B_KV_GROUPS = 4
B_HPG = N_HEADS // B_KV_GROUPS
CMP_LEN = 32
CMP_STRIDE = 16
CMP_HIDDEN = 256
SLC_BLOCK = 64
N_SELECT = 16
WIN = 512
FORCE_SCORE = 1e4
N_FORCED = 3
LIST_W = 272
LOG2E = math.log2(math.e)
MASK_PENALTY = 1e30
M_FLOOR = -1e29
CMP_STEP_BLOCKS = 2
CMP_PARTS = 8
SEL_CHUNK = 4


def _group_slopes(g):
    sl = [v * LOG2E for v in _slopes(N_HEADS)]
    out = []
    for r in range(B_HPG):
        v = jnp.full((1, 1), sl[r], F32)
        for k in range(1, B_KV_GROUPS):
            v = jnp.where(g == k, sl[k * B_HPG + r], v)
        out.append(v)
    return out


def _qproj_kernel(hb_ref, w_ref, b_ref, qt_ref, gt_ref):
    pt = lax.dot_general(w_ref[...], hb_ref[...], (((1,), (1,)), ((), ())), preferred_element_type=F32)
    pt = pt + b_ref[:, 0:1]
    qt_ref[...] = (pt[:D_MODEL] * (HEAD_DIM ** -0.5 * LOG2E)).astype(BF16)
    gt_ref[...] = 1.0 / (1.0 + jnp.exp(-pt[D_MODEL:]))


def _qproj(hb, w_q, b_q, tm=512):
    seq = hb.shape[0]
    n = w_q.shape[1]
    npad = D_MODEL + LANES
    w_t = jnp.zeros((npad, D_MODEL), BF16).at[:n].set(w_q.T.astype(BF16))
    b_t = jnp.zeros((npad, LANES), F32).at[:n].set(jnp.broadcast_to(b_q[:, None], (n, LANES)))
    return pl.pallas_call(
        _qproj_kernel,
        grid=(seq // tm,),
        in_specs=[pl.BlockSpec((tm, D_MODEL), lambda i: (i, 0)),
                  pl.BlockSpec((npad, D_MODEL), lambda i: (0, 0)),
                  pl.BlockSpec((npad, LANES), lambda i: (0, 0))],
        out_specs=[pl.BlockSpec((D_MODEL, tm), lambda i: (0, i)),
                   pl.BlockSpec((LANES, tm), lambda i: (0, i))],
        out_shape=[jax.ShapeDtypeStruct((D_MODEL, seq), BF16), jax.ShapeDtypeStruct((LANES, seq), F32)],
        compiler_params=_params(("arbitrary",)),
        name="nsa_qproj",
    )(hb, w_t, b_t)


def _compress_kernel(x_ref, w1a_ref, w1b_ref, pa_ref, pb_ref, b1_ref, w2_ref, b2_ref, o_ref):
    x = x_ref[0, 0]
    n = x.shape[0]
    a = jnp.dot((x + pa_ref[0]).astype(BF16), w1a_ref[0], preferred_element_type=F32)
    b = jnp.dot((x + pb_ref[0]).astype(BF16), w1b_ref[0], preferred_element_type=F32)
    b = pltpu.roll(b, n - 1, 0)
    u = a + b + b1_ref[0]
    hid = 0.5 * u * (1.0 + jnp.tanh(math.sqrt(2.0 / math.pi) * (u + 0.044715 * (u * u * u))))
    o_ref[0, 0] = (jnp.dot(hid.astype(BF16), w2_ref[0], preferred_element_type=F32) + b2_ref[0]).astype(BF16)


def _compress(kvc, cmp_pos, cmp_w1, cmp_b1, cmp_w2, cmp_b2):
    seq = kvc.shape[0]
    nch = seq // CMP_STRIDE
    half = CMP_STRIDE * HEAD_DIM
    chunks = kvc.reshape(nch, CMP_STRIDE, 2, B_KV_GROUPS, HEAD_DIM).transpose(2, 3, 0, 1, 4).reshape(
        2, B_KV_GROUPS, nch, half)
    pos = cmp_pos.reshape(2, 1, CMP_LEN * HEAD_DIM)
    w1 = cmp_w1.astype(BF16)
    jfix = lambda j, g: (j, 0, 0)
    return pl.pallas_call(
        _compress_kernel,
        grid=(2, B_KV_GROUPS),
        in_specs=[pl.BlockSpec((1, 1, nch, half), lambda j, g: (j, g, 0, 0)),
                  pl.BlockSpec((1, half, CMP_HIDDEN), jfix),
                  pl.BlockSpec((1, half, CMP_HIDDEN), lambda j, g: (j, 1, 0)),
                  pl.BlockSpec((1, 1, half), jfix),
                  pl.BlockSpec((1, 1, half), lambda j, g: (j, 0, 1)),
                  pl.BlockSpec((1, 1, CMP_HIDDEN), jfix),
                  pl.BlockSpec((1, CMP_HIDDEN, HEAD_DIM), jfix),
                  pl.BlockSpec((1, 1, HEAD_DIM), jfix)],
        out_specs=pl.BlockSpec((1, 1, nch, HEAD_DIM), lambda j, g: (j, g, 0, 0)),
        out_shape=jax.ShapeDtypeStruct((2, B_KV_GROUPS, nch, HEAD_DIM), BF16),
        compiler_params=_params(("arbitrary", "arbitrary")),
        name="nsa_compress",
    )(chunks, w1, w1, pos, pos, cmp_b1.reshape(2, 1, CMP_HIDDEN), cmp_w2.astype(BF16),
      cmp_b2.reshape(2, 1, HEAD_DIM))


def _nsa_shared_kv(hb, w_kv, cmp_pos, cmp_w1, cmp_b1, cmp_w2, cmp_b2):
    seq = hb.shape[0]
    gw = B_KV_GROUPS * HEAD_DIM
    w = w_kv.astype(BF16)
    kvc = _matmul(hb, w[:, :2 * gw], F32, 512, 2 * gw, "kv_proj_cmp")
    kvr = _matmul(hb, w[:, 2 * gw:], BF16, 512, 4 * gw, "kv_proj_rest")
    cmp = _compress(kvc, cmp_pos, cmp_w1, cmp_b1, cmp_w2, cmp_b2)
    n_slc, n_win = seq // SLC_BLOCK, seq // Q_BLOCK

    def blocks(cols, nblk, blk, transpose):
        t = cols.reshape(nblk, blk, B_KV_GROUPS, HEAD_DIM)
        return t.transpose(2, 0, 3, 1) if transpose else t.transpose(2, 0, 1, 3)

    return dict(
        kc=cmp[0], vct=cmp[1].transpose(0, 2, 1),
        ks=blocks(kvr[:, 0 * gw:1 * gw], n_slc, SLC_BLOCK, False),
        vst=blocks(kvr[:, 1 * gw:2 * gw], n_slc, SLC_BLOCK, False),
        kw=blocks(kvr[:, 2 * gw:3 * gw], n_win, Q_BLOCK, False),
        vwt=blocks(kvr[:, 3 * gw:4 * gw], n_win, Q_BLOCK, True))


def _select_matrix(n_slc, nch):
    m = np.zeros((n_slc, nch), np.float32)
    r_s, r_c = SLC_BLOCK // CMP_STRIDE, CMP_LEN // CMP_STRIDE
    for a in range(r_s):
        for b in range(r_c):
            idx = r_s * np.arange(n_slc) + a - b
            ok = (idx >= 0) & (idx < nch - r_c + 1)
            m[np.arange(n_slc)[ok], idx[ok]] += 1.0
    return jnp.asarray(m, BF16)


def _q_heads(qt_ref, lanes=slice(None)):
    return jnp.concatenate([qt_ref[r * HEAD_DIM:(r + 1) * HEAD_DIM, lanes] for r in range(B_HPG)], axis=1)


def _nsa_cmp_kernel(qt_ref, kc_ref, vct_ref, gt_ref, mt_ref, oc_ref, sel_ref, flag_ref, bias_ref):
    g = pl.program_id(0)
    i = pl.program_id(1)
    nch = kc_ref.shape[1]
    n_slc = mt_ref.shape[0]
    first_blk = i * CMP_STEP_BLOCKS

    @pl.when(i == 0)
    def _():
        slopes = _group_slopes(g)
        rel = (lax.broadcasted_iota(I32, (nch, Q_BLOCK), 1)
               - (CMP_STRIDE * lax.broadcasted_iota(I32, (nch, Q_BLOCK), 0) + (CMP_LEN - 1))).astype(F32)
        for r in range(B_HPG):
            bias_ref[r] = slopes[r] * rel

    def compressed_branch(n):
        for sub in range(CMP_STEP_BLOCKS):
            one_block(n, sub)

    def one_block(n, sub):
        t0 = (first_blk + sub) * Q_BLOCK
        lanes = slice(sub * Q_BLOCK, (sub + 1) * Q_BLOCK)
        cidx = lax.broadcasted_iota(I32, (n, Q_BLOCK), 0)
        qidx = lax.broadcasted_iota(I32, (n, Q_BLOCK), 1)
        valid = (t0 + qidx) - (CMP_STRIDE * cidx + (CMP_LEN - 1)) >= 0
        st = jnp.dot(kc_ref[0, :n, :], _q_heads(qt_ref, lanes), preferred_element_type=F32)
        imp = jnp.zeros((n, Q_BLOCK), F32)
        outs = []
        for r in range(B_HPG):
            s = st[:, r * Q_BLOCK:(r + 1) * Q_BLOCK] - bias_ref[r, :n, :]
            s = jnp.where(valid, s, NEG_INF)
            m = jnp.maximum(jnp.max(s, axis=0, keepdims=True), M_FLOOR)
            p = jnp.exp2(s - m)
            l = jnp.sum(p, axis=0, keepdims=True)
            pn = p * (1.0 / jnp.maximum(l, TINY))
            imp = imp + pn
            o = jnp.dot(vct_ref[0, :, :n], pn.astype(BF16), preferred_element_type=F32)
            outs.append(o * gt_ref[pl.ds(g * B_HPG + r, 1), :][:, lanes])
        oc_ref[lanes, :] = jnp.concatenate(outs, axis=0).T
        hi = imp.astype(BF16)
        r1 = imp - hi.astype(F32)
        mid = r1.astype(BF16)
        lo = (r1 - mid.astype(F32)).astype(BF16)
        ns = n * n_slc // nch
        mt = mt_ref[:ns, :n]
        p_slc = (jnp.dot(mt, hi, preferred_element_type=F32) + jnp.dot(mt, mid, preferred_element_type=F32)
                 + jnp.dot(mt, lo, preferred_element_type=F32))
        sid = lax.broadcasted_iota(I32, (ns, Q_BLOCK), 0)
        cur = (t0 + lax.broadcasted_iota(I32, (ns, Q_BLOCK), 1)) // SLC_BLOCK
        forced = (sid == 0) | (sid == cur) | (sid == cur - 1)
        score = jnp.where(forced, -jnp.inf, p_slc)
        score = jnp.where(sid <= cur, score, -FORCE_SCORE)
        for _ in range(N_SELECT - N_FORCED):
            mx = jnp.max(score, axis=0, keepdims=True)
            first = jnp.min(jnp.where(score == mx, sid, ns), axis=0, keepdims=True)
            score = jnp.where(sid == first, -jnp.inf, score)
        sel = jnp.where((score == -jnp.inf) & (sid <= cur), 1.0, 0.0)
        sel_ref[0, sub, :ns, :] = sel
        flag_ref[0, sub, :ns, :] = jnp.max(sel, axis=1, keepdims=True).astype(I32)
        if ns < n_slc:
            sel_ref[0, sub, ns:, :] = jnp.zeros((n_slc - ns, Q_BLOCK), F32)
            flag_ref[0, sub, ns:, :] = jnp.zeros((n_slc - ns, 1), I32)

    n_parts = min(CMP_PARTS, n_slc // N_SELECT)
    part = nch // n_parts
    lax.switch(lax.div(8 * (first_blk + CMP_STEP_BLOCKS - 1) + 6, part),
               [functools.partial(compressed_branch, part * (k + 1)) for k in range(n_parts)])


def _nsa_cmp(qt, gt, kv, seq):
    nb = seq // Q_BLOCK
    nch, n_slc = seq // CMP_STRIDE, seq // SLC_BLOCK
    gw = B_HPG * HEAD_DIM
    nq = CMP_STEP_BLOCKS * Q_BLOCK
    return pl.pallas_call(
        _nsa_cmp_kernel,
        grid=(B_KV_GROUPS, nb // CMP_STEP_BLOCKS),
        in_specs=[pl.BlockSpec((gw, nq), lambda g, i: (g, i)),
                  pl.BlockSpec((1, nch, HEAD_DIM), lambda g, i: (g, 0, 0)),
                  pl.BlockSpec((1, HEAD_DIM, nch), lambda g, i: (g, 0, 0)),
                  pl.BlockSpec((LANES, nq), lambda g, i: (0, i)),
                  pl.BlockSpec((n_slc, nch), lambda g, i: (0, 0))],
        out_specs=[pl.BlockSpec((nq, gw), lambda g, i: (i, g)),
                   pl.BlockSpec((1, CMP_STEP_BLOCKS, n_slc, Q_BLOCK), lambda g, i: (g, i, 0, 0)),
                   pl.BlockSpec((1, CMP_STEP_BLOCKS, n_slc, 1), lambda g, i: (g, i, 0, 0))],
        out_shape=[jax.ShapeDtypeStruct((seq, D_MODEL), F32),
                   jax.ShapeDtypeStruct((B_KV_GROUPS, nb, n_slc, Q_BLOCK), F32),
                   jax.ShapeDtypeStruct((B_KV_GROUPS, nb, n_slc, 1), I32)],
        scratch_shapes=[pltpu.VMEM((B_HPG, nch, Q_BLOCK), F32)],
        compiler_params=_params(("arbitrary", "arbitrary")),
        name="nsa_compressed_select",
    )(qt, kv["kc"], kv["vct"], gt, _select_matrix(n_slc, nch))


def _fold8(x, op):
    return functools.reduce(op, [x[8 * k:8 * k + 8] for k in range(x.shape[0] // 8)])


def _softmax_tiles(sts, tiles, rel_bias, slopes, m_old):
    ps = [[] for _ in tiles]
    ms, ls, alphas = [], [], []
    for r in range(B_HPG):
        cols = slice(r * Q_BLOCK, (r + 1) * Q_BLOCK)
        us, shifts, tops = [], [], []
        for st, (mask, penalty, off) in zip(sts, tiles):
            u = st[:, cols] - rel_bias[r]
            u = u if mask is None else jnp.where(mask, u, NEG_INF)
            shift = slopes[r] * jnp.asarray(off, F32)
            shift = shift if penalty is None else shift + penalty
            us.append(u)
            shifts.append(shift)
            tops.append(_fold8(u, jnp.maximum) - shift)
        m_new = jnp.max(functools.reduce(jnp.maximum, tops), axis=0, keepdims=True)
        m_new = jnp.maximum(m_new, M_FLOOR)
        if m_old is not None:
            m_new = jnp.maximum(m_old[:, cols], m_new)
            alphas.append(jnp.exp2(m_old[:, cols] - m_new))
        l8 = jnp.zeros((8, Q_BLOCK), F32)
        for t, (u, shift) in enumerate(zip(us, shifts)):
            p = jnp.exp2(u - (m_new + shift))
            l8 = l8 + _fold8(p, jnp.add)
            ps[t].append(p.astype(BF16))
        ms.append(m_new)
        ls.append(jnp.sum(l8, axis=0, keepdims=True))
    ps = [jnp.concatenate(p, axis=1) for p in ps]
    alpha = None if m_old is None else jnp.concatenate(alphas, axis=1)
    return ps, jnp.concatenate(ms, axis=1), jnp.concatenate(ls, axis=1), alpha


def _nsa_sparse_kernel(lst_ref, qt_ref, ks_ref, vst_ref, kw_ref, vwt_ref, sel_ref, gt_ref, oc_ref, o_ref,
                       m_ref, alpha_ref, l_ref, accw_ref, acc_ref, st_ref, p_ref):
    g = pl.program_id(0)
    i = pl.program_id(1)
    t0 = i * Q_BLOCK
    slopes = _group_slopes(g)
    q4 = _q_heads(qt_ref)

    def gate(branch):
        return jnp.concatenate(
            [gt_ref[pl.ds(branch * N_HEADS + g * B_HPG + r, 1), :] for r in range(B_HPG)], axis=1)

    rel_w = (lax.broadcasted_iota(I32, (Q_BLOCK, Q_BLOCK), 1)
             - lax.broadcasted_iota(I32, (Q_BLOCK, Q_BLOCK), 0))
    n_prev = WIN // Q_BLOCK
    rel_w_f = rel_w.astype(F32)
    bias_w = [slopes[r] * rel_w_f for r in range(B_HPG)]
    tiles, blks = [], []
    for c in range(n_prev + 1):
        blk = i - n_prev + c
        off = (n_prev - c) * Q_BLOCK
        mask = (rel_w >= 0) if c == n_prev else (rel_w < 0) if c == 0 else None
        penalty = None if c == n_prev else jnp.where(blk >= 0, 0.0, MASK_PENALTY).astype(F32)
        tiles.append((mask, penalty, off))
        blks.append(jnp.maximum(blk, 0))
    sts = [jnp.dot(kw_ref[0, blk], q4, preferred_element_type=F32) for blk in blks]
    ps, _, l_w, _ = _softmax_tiles(sts, tiles, bias_w, slopes, None)
    acc_w = functools.reduce(lambda a, b: a + b, [
        jnp.dot(vwt_ref[0, blk], p, preferred_element_type=F32) for blk, p in zip(blks, ps)])
    accw_ref[...] = acc_w * (gate(2) / jnp.maximum(l_w, TINY))

    rel_s = (lax.broadcasted_iota(I32, (SLC_BLOCK, Q_BLOCK), 1)
             - lax.broadcasted_iota(I32, (SLC_BLOCK, Q_BLOCK), 0))
    rel_s_f = rel_s.astype(F32)
    bias_s = [slopes[r] * rel_s_f for r in range(B_HPG)]

    def block_id(j, b):
        return lst_ref[0, 0, j * SEL_CHUNK + b]

    def rows(b):
        return slice(b * SLC_BLOCK, (b + 1) * SLC_BLOCK)

    def scores(j, slot):
        for b in range(SEL_CHUNK):
            st_ref[slot, rows(b), :] = jnp.dot(ks_ref[0, block_id(j, b)], q4, preferred_element_type=F32)

    def weighted_values(j):
        v = jnp.concatenate([vst_ref[0, block_id(j, b)] for b in range(SEL_CHUNK)], axis=0)
        return lax.dot_general(v, p_ref[...], (((0,), (0,)), ((), ())), preferred_element_type=F32)

    m_ref[...] = jnp.full(m_ref.shape, M_FLOOR, F32)
    alpha_ref[...] = jnp.ones(alpha_ref.shape, F32)
    l_ref[...] = jnp.zeros(l_ref.shape, F32)
    acc_ref[...] = jnp.zeros(acc_ref.shape, F32)
    p_ref[...] = jnp.zeros(p_ref.shape, BF16)
    scores(0, 0)

    def sel_chunk(j, slot):
        acc_ref[...] = acc_ref[...] * alpha_ref[0:1, :] + weighted_values(jnp.maximum(j - 1, 0))
        tiles, sts = [], []
        for b in range(SEL_CHUNK):
            blk = block_id(j, b)
            off = t0 - blk * SLC_BLOCK
            penalty = jnp.where(sel_ref[0, 0, pl.ds(blk, 1), :] > 0.5, 0.0, MASK_PENALTY)
            mask = (rel_s >= -off) if b < 2 else None
            tiles.append((mask, penalty, off))
            sts.append(st_ref[slot, rows(b), :])
        ps, m, l, alpha = _softmax_tiles(sts, tiles, bias_s, slopes, m_ref[0:1, :])
        for b in range(SEL_CHUNK):
            p_ref[rows(b), :] = ps[b]
        m_ref[0:1, :] = m
        l_ref[0:1, :] = alpha * l_ref[0:1, :] + l
        alpha_ref[0:1, :] = alpha
        scores(j + 1, 1 - slot)

    def sel_pair(jj, c):
        sel_chunk(2 * jj, 0)
        sel_chunk(2 * jj + 1, 1)
        return c

    n_pairs = lst_ref[0, 0, LIST_W - 1]
    lax.fori_loop(0, n_pairs, sel_pair, 0)
    acc_s = acc_ref[...] * alpha_ref[0:1, :] + weighted_values(jnp.maximum(2 * n_pairs - 1, 0))

    total = accw_ref[...] + acc_s * (gate(1) / jnp.maximum(l_ref[0:1, :], TINY))
    ot = jnp.concatenate([total[:, r * Q_BLOCK:(r + 1) * Q_BLOCK] for r in range(B_HPG)], axis=0)
    o_ref[...] = (ot.T + oc_ref[...]).astype(BF16)


def _nsa_sparse(lists, qt, gt, kv, sel, oc, seq):
    nb = seq // Q_BLOCK
    n_slc, n_win = seq // SLC_BLOCK, seq // Q_BLOCK
    gw = B_HPG * HEAD_DIM
    grp = lambda g, i: (g, 0, 0, 0)
    return pl.pallas_call(
        _nsa_sparse_kernel,
        grid=(B_KV_GROUPS, nb),
        in_specs=[pl.BlockSpec((1, 1, LIST_W), lambda g, i: (g * nb + i, 0, 0), memory_space=pltpu.SMEM),
                  pl.BlockSpec((gw, Q_BLOCK), lambda g, i: (g, i)),
                  pl.BlockSpec((1, n_slc, SLC_BLOCK, HEAD_DIM), grp),
                  pl.BlockSpec((1, n_slc, SLC_BLOCK, HEAD_DIM), grp),
                  pl.BlockSpec((1, n_win, Q_BLOCK, HEAD_DIM), grp),
                  pl.BlockSpec((1, n_win, HEAD_DIM, Q_BLOCK), grp),
                  pl.BlockSpec((1, 1, n_slc, Q_BLOCK), lambda g, i: (g, i, 0, 0)),
                  pl.BlockSpec((LANES, Q_BLOCK), lambda g, i: (0, i)),
                  pl.BlockSpec((Q_BLOCK, gw), lambda g, i: (i, g))],
        out_specs=pl.BlockSpec((Q_BLOCK, gw), lambda g, i: (i, g)),
        scratch_shapes=[pltpu.VMEM((8, B_HPG * Q_BLOCK), F32),
                        pltpu.VMEM((8, B_HPG * Q_BLOCK), F32),
                        pltpu.VMEM((8, B_HPG * Q_BLOCK), F32),
                        pltpu.VMEM((HEAD_DIM, B_HPG * Q_BLOCK), F32),
                        pltpu.VMEM((HEAD_DIM, B_HPG * Q_BLOCK), F32),
                        pltpu.VMEM((2, SEL_CHUNK * SLC_BLOCK, B_HPG * Q_BLOCK), F32),
                        pltpu.VMEM((SEL_CHUNK * SLC_BLOCK, B_HPG * Q_BLOCK), BF16)],
        out_shape=jax.ShapeDtypeStruct((seq, D_MODEL), BF16),
        compiler_params=_params(("arbitrary", "arbitrary")),
        name="nsa_selected_window",
    )(lists, qt, kv["ks"], kv["vst"], kv["kw"], kv["vwt"], sel, gt, oc)


def _outproj_b_kernel(o_ref, h_ref, w_ref, g_ref, b_ref, hn_ref, hb_ref):
    mix = jnp.dot(o_ref[...], w_ref[...], preferred_element_type=F32)
    _layer_norm_store(DEEPNORM_ALPHA * h_ref[...] + mix, g_ref, b_ref, hn_ref, hb_ref)


def _outproj_b(o, h, w_out, ln_g, ln_b, tm=512):
    seq = h.shape[0]
    row = lambda i: (i, 0)
    fix = lambda i: (0, 0)
    return pl.pallas_call(
        _outproj_b_kernel,
        grid=(seq // tm,),
        in_specs=[pl.BlockSpec((tm, D_MODEL), row), pl.BlockSpec((tm, D_MODEL), row),
                  pl.BlockSpec((D_MODEL, D_MODEL), fix), pl.BlockSpec((1, D_MODEL), fix),
                  pl.BlockSpec((1, D_MODEL), fix)],
        out_specs=[pl.BlockSpec((tm, D_MODEL), row)] * 2,
        out_shape=[jax.ShapeDtypeStruct((seq, D_MODEL), F32), jax.ShapeDtypeStruct((seq, D_MODEL), BF16)],
        compiler_params=_params(("arbitrary",)),
        name="outproj_ln_b",
    )(o, h, w_out.astype(BF16), ln_g.reshape(1, -1), ln_b.reshape(1, -1))


def _nsa_layer(h, hb, kv, w_q, b_q, w_out, ln_g, ln_b):
    seq = h.shape[0]
    nb, n_slc = seq // Q_BLOCK, seq // SLC_BLOCK
    qt, gt = _qproj(hb, w_q, b_q)
    oc, sel, flags = _nsa_cmp(qt, gt, kv, seq)
    f = flags.reshape(B_KV_GROUPS, nb, n_slc)
    newest = 2 * jnp.arange(nb, dtype=I32)[None, :, None] + 1
    age = jnp.mod(newest - jnp.arange(n_slc, dtype=I32)[None, None, :], n_slc)
    order = jnp.argsort((1 - f) * n_slc + age, axis=2).astype(I32).reshape(B_KV_GROUPS * nb, n_slc)
    n_pairs = (jnp.sum(f, axis=2).reshape(-1) + 2 * SEL_CHUNK - 1) // (2 * SEL_CHUNK)
    lists = jnp.concatenate(
        [order, jnp.zeros((B_KV_GROUPS * nb, LIST_W - n_slc - 1), I32), n_pairs.astype(I32)[:, None]], axis=1)
    o = _nsa_sparse(lists.reshape(B_KV_GROUPS * nb, 1, LIST_W), qt, gt, kv, sel, oc, seq)
    return _outproj_b(o, h, w_out, ln_g, ln_b)


def kernel(x, a_w_in, a_w_out, b_w_kv, b_cmp_pos, b_cmp_w1, b_cmp_b1, b_cmp_w2, b_cmp_b2, b_w_q, b_b_q, b_w_out, moe_w_router, moe_b_router, moe_w_gate_up, moe_w_down, ln_mix_g, ln_mix_b, ln_ffn_g, ln_ffn_b):
    seq = x.shape[1]
    h = x.reshape(seq, D_MODEL)
    hb = h.astype(BF16)
    a_w_in_b, w_gate_up_b, w_down_b = a_w_in.astype(BF16), moe_w_gate_up.astype(BF16), moe_w_down.astype(BF16)
    buf = None
    for layer in range(DEPTH):
        if layer < N_A_LAYERS:
            h, hb = _dilated_layer(h, hb, a_w_in_b, layer, a_w_out[layer], ln_mix_g[layer], ln_mix_b[layer])
        else:
            if layer == N_A_LAYERS:
                kv = _nsa_shared_kv(hb, b_w_kv, b_cmp_pos, b_cmp_w1, b_cmp_b1, b_cmp_w2, b_cmp_b2)
            j = layer - N_A_LAYERS
            h, hb = _nsa_layer(h, hb, kv, b_w_q[j], b_b_q[j], b_w_out[j], ln_mix_g[layer], ln_mix_b[layer])
        h, hb, buf = _moe_layer(h, moe_w_router[layer], moe_b_router[layer], w_gate_up_b, w_down_b, layer,
                                ln_ffn_g[layer], ln_ffn_b[layer], buf)
    return h.reshape(1, seq, D_MODEL)
```

```python
import functools
import math

import numpy as np
import jax
import jax.numpy as jnp
from jax import lax
from jax.experimental import pallas as pl
from jax.experimental.pallas import tpu as pltpu

F32 = jnp.float32
BF16 = jnp.bfloat16
I32 = jnp.int32

D_MODEL = 1024
DEPTH = 4
N_A_LAYERS = DEPTH // 2
HEAD_DIM = 64
N_HEADS = D_MODEL // HEAD_DIM
Q_BLOCK = 128
LN_EPS = 1e-5
NEG_INF = -1e30
TINY = 1e-30

A_WINDOWS = (128, 512, 2048)
A_DILATIONS = (1, 4, 16)
A_N_GROUPS = 3
DIL_STEP_BLOCKS = 8

N_EXPERT_GROUPS = 4
EXPERTS_PER_GROUP = 4
N_EXPERTS = 16
D_EXPERT = 256

DEEPNORM_ALPHA = (2.0 * DEPTH) ** 0.25

LANES = 128
MOE_TILE = 256
VMEM_LIMIT = 56 * 1024 * 1024


def _slopes(n):
    return [float(v) for v in np.asarray(2.0 ** (-8.0 * np.arange(1, n + 1) / n), dtype=np.float32)]


def _params(sem):
    return pltpu.CompilerParams(dimension_semantics=sem, vmem_limit_bytes=VMEM_LIMIT)


def _mm_kernel(x_ref, w_ref, o_ref):
    o_ref[...] = jnp.dot(x_ref[...], w_ref[...], preferred_element_type=F32).astype(o_ref.dtype)


def _matmul(x, w, out_dtype, tm, tn, name):
    m, k = x.shape
    n = w.shape[1]
    return pl.pallas_call(
        _mm_kernel,
        grid=(n // tn, m // tm),
        in_specs=[pl.BlockSpec((tm, k), lambda j, i: (i, 0)),
                  pl.BlockSpec((k, tn), lambda j, i: (0, j))],
        out_specs=pl.BlockSpec((tm, tn), lambda j, i: (i, j)),
        out_shape=jax.ShapeDtypeStruct((m, n), out_dtype),
        compiler_params=_params(("arbitrary", "arbitrary")),
        name=name,
    )(x, w)


def _layer_norm_store(z, g_ref, b_ref, h_ref, hb_ref):
    mu = jnp.mean(z, axis=-1, keepdims=True)
    zc = z - mu
    var = jnp.mean(zc * zc, axis=-1, keepdims=True)
    y = zc * lax.rsqrt(var + LN_EPS) * g_ref[...] + b_ref[...]
    h_ref[...] = y
    hb_ref[...] = y.astype(BF16)


def _split_bf16(x):
    hi = x.astype(BF16)
    lo = (x - hi.astype(F32)).astype(BF16)
    return hi, lo


def _dil_attn_kernel(q_ref, kp_ref, kc_ref, vp_ref, vc_ref, o_ref, lse_ref, bias_ref, *, dil, slopes):
    r = pl.program_id(0)
    jb = pl.program_id(1)
    qb = Q_BLOCK

    @pl.when((r == 0) & (jb == 0))
    def _():
        key = lax.broadcasted_iota(I32, (2 * qb, qb), 0)
        qry = lax.broadcasted_iota(I32, (2 * qb, qb), 1)
        dist = qry + qb - key
        valid = (dist >= 0) & (dist <= qb)
        dist_tok = (dist * dil).astype(F32)
        for h in range(N_HEADS):
            b = jnp.where(valid, -(slopes[h] * LOG2E * dist_tok), NEG_INF)
            bias_ref[1, h] = b
            bias_ref[0, h] = jnp.where(key >= qb, b, NEG_INF)

    lane = lax.broadcasted_iota(I32, (1, LANES), 1)
    lo_half = lane < HEAD_DIM
    lo_rows = lax.broadcasted_iota(I32, (LANES, qb), 0) < HEAD_DIM
    for sub in range(DIL_STEP_BLOCKS):
        rows = slice(sub * qb, (sub + 1) * qb)
        before = slice((sub - 1) * qb, sub * qb)
        first = jnp.minimum(jb, 1) if sub == 0 else 1
        lses = []
        for hp in range(N_HEADS // 2):
            sl = slice(hp * LANES, (hp + 1) * LANES)
            q2 = q_ref[rows, sl]
            k_prev = kp_ref[:, sl] if sub == 0 else kc_ref[before, sl]
            v_prev = vp_ref[:, sl] if sub == 0 else vc_ref[before, sl]
            k2 = jnp.concatenate([k_prev, kc_ref[rows, sl]], axis=0)
            v2t = jnp.concatenate([v_prev, vc_ref[rows, sl]], axis=0).T
            outs = []
            for half in range(2):
                keep = lo_half if half == 0 else jnp.logical_not(lo_half)
                qm = jnp.where(keep, q2, jnp.zeros_like(q2))
                st = lax.dot_general(k2, qm, (((1,), (1,)), ((), ())), preferred_element_type=F32)
                st = st + bias_ref[first, 2 * hp + half]
                m = jnp.max(st, axis=0, keepdims=True)
                p = jnp.exp2(st - m)
                l = jnp.sum(p, axis=0, keepdims=True)
                ot = jnp.dot(v2t, p.astype(BF16), preferred_element_type=F32)
                outs.append(ot * (1.0 / l))
                lses.append(m + jnp.log2(l))
            o_ref[rows, sl] = jnp.where(lo_rows, outs[0], outs[1]).T.astype(BF16)
        lse_t = jnp.concatenate(lses + [jnp.zeros((LANES - N_HEADS, qb), F32)], axis=0)
        lse_ref[rows, :] = lse_t.T


def _qkv_kernel(x_ref, w_ref, o_ref, acc_ref, *, dil):
    acc = jnp.dot(x_ref[...], w_ref[...], preferred_element_type=F32)
    acc = acc * jnp.where(pl.program_id(0) == 0, HEAD_DIM ** -0.5 * LOG2E, 1.0)
    if dil == 1:
        o_ref[0] = acc.astype(BF16)
    else:
        n_chunks = acc.shape[1] // LANES
        for c in range(n_chunks):
            acc_ref[c] = acc[:, c * LANES:(c + 1) * LANES]
        n = acc.shape[0] // dil
        for r in range(dil):
            o_ref[r] = jnp.concatenate(
                [acc_ref[c, pl.ds(r, n, stride=dil), :] for c in range(n_chunks)], axis=1).astype(BF16)


def _qkv_group(hb, w_in, layer, g, tm=1024):
    seq = hb.shape[0]
    dil = A_DILATIONS[g]
    return pl.pallas_call(
        functools.partial(_qkv_kernel, dil=dil),
        grid=(3, seq // tm),
        in_specs=[pl.BlockSpec((tm, D_MODEL), lambda c, i: (i, 0)),
                  pl.BlockSpec((None, D_MODEL, D_MODEL), lambda c, i: (layer, 0, 3 * g + c))],
        out_specs=pl.BlockSpec((dil, tm // dil, D_MODEL), lambda c, i: (0, i, c)),
        out_shape=jax.ShapeDtypeStruct((dil, seq // dil, 3 * D_MODEL), BF16),
        scratch_shapes=[pltpu.VMEM((D_MODEL // LANES, tm, LANES), F32)],
        compiler_params=_params(("arbitrary", "arbitrary")),
        name=f"qkv_proj_g{g}",
    )(hb, w_in)


def _dilated_group(qkv_g, g, seq):
    dil = A_DILATIONS[g]
    rows = seq // dil
    nblk = rows // Q_BLOCK
    nq = DIL_STEP_BLOCKS * Q_BLOCK
    prev = (None, Q_BLOCK, D_MODEL)
    blk = (None, nq, D_MODEL)
    kern = functools.partial(_dil_attn_kernel, dil=dil, slopes=_slopes(N_HEADS))
    return pl.pallas_call(
        kern,
        grid=(dil, nblk // DIL_STEP_BLOCKS),
        in_specs=[
            pl.BlockSpec(blk, lambda r, j: (r, j, 0)),
            pl.BlockSpec(prev, lambda r, j: (r, jnp.maximum(DIL_STEP_BLOCKS * j - 1, 0), 1)),
            pl.BlockSpec(blk, lambda r, j: (r, j, 1)),
            pl.BlockSpec(prev, lambda r, j: (r, jnp.maximum(DIL_STEP_BLOCKS * j - 1, 0), 2)),
            pl.BlockSpec(blk, lambda r, j: (r, j, 2)),
        ],
        out_specs=[pl.BlockSpec(blk, lambda r, j: (r, j, 0)),
                   pl.BlockSpec((None, nq, LANES), lambda r, j: (r, j, 0))],
        out_shape=[jax.ShapeDtypeStruct((dil, rows, D_MODEL), BF16),
                   jax.ShapeDtypeStruct((dil, rows, LANES), F32)],
        scratch_shapes=[pltpu.VMEM((2, N_HEADS, 2 * Q_BLOCK, Q_BLOCK), F32)],
        compiler_params=_params(("arbitrary", "arbitrary")),
        name=f"dilated_attn_g{g}",
    )(qkv_g, qkv_g, qkv_g, qkv_g, qkv_g)


def _head_expand_matrix():
    h = np.arange(LANES)[:, None]
    c = np.arange(D_MODEL)[None, :]
    return jnp.asarray((c // HEAD_DIM == h).astype(np.float32), BF16)


def _to_token_order(src_ref, dst_ref, dil):
    n = src_ref.shape[1]
    n_chunks = dst_ref.shape[0]
    if dil == 1:
        return src_ref[0].astype(F32)
    for r in range(dil):
        v = src_ref[r].astype(F32)
        for c in range(n_chunks):
            dst_ref[c, pl.ds(r, n, stride=dil), :] = v[:, c * LANES:(c + 1) * LANES]
    return jnp.concatenate([dst_ref[c] for c in range(n_chunks)], axis=1)


def _outproj_a_kernel(o0_ref, o1_ref, o2_ref, l0_ref, l1_ref, l2_ref, h_ref, w_ref, e_ref, g_ref, b_ref,
                      hn_ref, hb_ref, onat_ref, lnat_ref):
    l0, l1, l2 = [_to_token_order(l_ref, lnat_ref, A_DILATIONS[gi])
                  for gi, l_ref in enumerate((l0_ref, l1_ref, l2_ref))]
    m = jnp.maximum(jnp.maximum(l0, l1), l2)
    e0, e1, e2 = jnp.exp2(l0 - m), jnp.exp2(l1 - m), jnp.exp2(l2 - m)
    inv = 1.0 / (e0 + e1 + e2)
    o = jnp.zeros(h_ref.shape, F32)
    for gi, (e, o_ref) in enumerate(((e0, o0_ref), (e1, o1_ref), (e2, o2_ref))):
        hi, lo = _split_bf16(e * inv)
        wexp = (jnp.dot(hi, e_ref[...], preferred_element_type=F32)
                + jnp.dot(lo, e_ref[...], preferred_element_type=F32))
        o = o + wexp * _to_token_order(o_ref, onat_ref, A_DILATIONS[gi])
    mix = jnp.dot(o.astype(BF16), w_ref[...], preferred_element_type=F32)
    _layer_norm_store(DEEPNORM_ALPHA * h_ref[...] + mix, g_ref, b_ref, hn_ref, hb_ref)


def _outproj_a(outs, lses, h, w_out, ln_g, ln_b, tm=512):
    seq = h.shape[0]
    row = lambda i: (i, 0)
    fix = lambda i: (0, 0)
    cls = lambda i: (0, i, 0)
    return pl.pallas_call(
        _outproj_a_kernel,
        grid=(seq // tm,),
        in_specs=[pl.BlockSpec((d, tm // d, D_MODEL), cls) for d in A_DILATIONS]
        + [pl.BlockSpec((d, tm // d, LANES), cls) for d in A_DILATIONS] + [
            pl.BlockSpec((tm, D_MODEL), row),
            pl.BlockSpec((D_MODEL, D_MODEL), fix),
            pl.BlockSpec((LANES, D_MODEL), fix),
            pl.BlockSpec((1, D_MODEL), fix),
            pl.BlockSpec((1, D_MODEL), fix)],
        out_specs=[pl.BlockSpec((tm, D_MODEL), row)] * 2,
        out_shape=[jax.ShapeDtypeStruct((seq, D_MODEL), F32), jax.ShapeDtypeStruct((seq, D_MODEL), BF16)],
        scratch_shapes=[pltpu.VMEM((D_MODEL // LANES, tm, LANES), F32), pltpu.VMEM((1, tm, LANES), F32)],
        compiler_params=_params(("arbitrary",)),
        name="outproj_ln_a",
    )(*outs, *lses, h, w_out.astype(BF16), _head_expand_matrix(), ln_g.reshape(1, -1), ln_b.reshape(1, -1))


def _dilated_layer(h, hb, w_in, layer, w_out, ln_g, ln_b):
    seq = h.shape[0]
    outs, lses = [], []
    for g in range(A_N_GROUPS):
        qkv_g = _qkv_group(hb, w_in, layer, g)
        o, lse = _dilated_group(qkv_g, g, seq)
        outs.append(o)
        lses.append(lse)
    return _outproj_a(outs, lses, h, w_out, ln_g, ln_b)


ROUTER_ROWS = 24


def _router_kernel(h_ref, w_ref, b_ref, ids_ref, wts_ref, cnt_ref, base_ref):
    i = pl.program_id(0)
    tm = h_ref.shape[0]

    @pl.when(i == 0)
    def _():
        base_ref[...] = jnp.zeros_like(base_ref)

    lt = lax.dot_general(w_ref[...], h_ref[...], (((1,), (1,)), ((), ())),
                         precision=lax.Precision.HIGHEST, preferred_element_type=F32)
    lt = lt + b_ref[:, 0:1]
    gl = [lt[k:k + 1, :] for k in range(N_EXPERT_GROUPS)]
    best, gidx = gl[0], jnp.zeros((1, tm), I32)
    for k in range(1, N_EXPERT_GROUPS):
        take = gl[k] > best
        best = jnp.where(take, gl[k], best)
        gidx = jnp.where(take, k, gidx)
    denom = gl[0] * 0.0
    for k in range(N_EXPERT_GROUPS):
        denom = denom + jnp.exp(gl[k] - best)
    g_w = 1.0 / denom
    e_in = []
    for e in range(EXPERTS_PER_GROUP):
        v = lt[N_EXPERT_GROUPS + e:N_EXPERT_GROUPS + e + 1, :]
        for k in range(1, N_EXPERT_GROUPS):
            row = N_EXPERT_GROUPS + k * EXPERTS_PER_GROUP + e
            v = jnp.where(gidx == k, lt[row:row + 1, :], v)
        e_in.append(v)
    v1, i1 = e_in[0], jnp.zeros((1, tm), I32)
    for e in range(1, EXPERTS_PER_GROUP):
        take = e_in[e] > v1
        v1 = jnp.where(take, e_in[e], v1)
        i1 = jnp.where(take, e, i1)
    v2, i2 = jnp.full((1, tm), -jnp.inf, F32), jnp.zeros((1, tm), I32)
    for e in range(EXPERTS_PER_GROUP):
        take = (e_in[e] > v2) & (i1 != e)
        v2 = jnp.where(take, e_in[e], v2)
        i2 = jnp.where(take, e, i2)
    t = jnp.exp(v2 - v1)
    w1 = g_w / (1.0 + t)
    w2 = g_w * t / (1.0 + t)
    e1 = gidx * EXPERTS_PER_GROUP + i1
    e2 = gidx * EXPERTS_PER_GROUP + i2

    erow = lax.broadcasted_iota(I32, (N_EXPERTS, tm), 0)
    oh1 = (erow == e1).astype(F32)
    oh2 = (erow == e2).astype(F32)
    c = oh1 + oh2
    tri = (lax.broadcasted_iota(I32, (tm, tm), 0) < lax.broadcasted_iota(I32, (tm, tm), 1)).astype(BF16)
    excl = jnp.dot(c.astype(BF16), tri, preferred_element_type=F32) + base_ref[...]
    r1 = jnp.sum(oh1 * excl, axis=0, keepdims=True).astype(I32)
    r2 = jnp.sum(oh2 * excl, axis=0, keepdims=True).astype(I32)
    base_ref[...] = base_ref[...] + jnp.sum(c, axis=1, keepdims=True)

    zi = jnp.zeros((1, tm), I32)
    ids_ref[...] = jnp.concatenate([e1, e2, r1, r2, zi, zi, zi, zi], axis=0)
    zf = jnp.zeros((1, tm), F32)
    wts_ref[...] = jnp.concatenate([w1, w2, zf, zf, zf, zf, zf, zf], axis=0)
    cnt_ref[...] = jnp.broadcast_to(base_ref[...], cnt_ref.shape)


def _router(h, w_router, b_router, tm=512):
    seq = h.shape[0]
    n_log = N_EXPERT_GROUPS + N_EXPERTS
    w_t = jnp.zeros((ROUTER_ROWS, D_MODEL), F32).at[:n_log].set(w_router.T)
    b_t = jnp.zeros((ROUTER_ROWS, LANES), F32).at[:n_log].set(jnp.broadcast_to(b_router[:, None], (n_log, LANES)))
    return pl.pallas_call(
        _router_kernel,
        grid=(seq // tm,),
        in_specs=[pl.BlockSpec((tm, D_MODEL), lambda i: (i, 0)),
                  pl.BlockSpec((ROUTER_ROWS, D_MODEL), lambda i: (0, 0)),
                  pl.BlockSpec((ROUTER_ROWS, LANES), lambda i: (0, 0))],
        out_specs=[pl.BlockSpec((8, tm), lambda i: (0, i)),
                   pl.BlockSpec((8, tm), lambda i: (0, i)),
                   pl.BlockSpec((N_EXPERTS, LANES), lambda i: (0, 0))],
        out_shape=[jax.ShapeDtypeStruct((8, seq), I32), jax.ShapeDtypeStruct((8, seq), F32),
                   jax.ShapeDtypeStruct((N_EXPERTS, LANES), F32)],
        scratch_shapes=[pltpu.VMEM((N_EXPERTS, 1), F32)],
        compiler_params=_params(("arbitrary",)),
        name="moe_router",
    )(h, w_t, b_t)


ROW_TILE = D_MODEL // LANES


def _rows_to_tiles(x, dst_ref):
    n = x.shape[0]
    for c in range(ROW_TILE):
        dst_ref[pl.ds(c, n, stride=ROW_TILE), :] = x[:, c * LANES:(c + 1) * LANES]


def _tiles_to_rows(src_ref):
    n = src_ref.shape[0] // ROW_TILE
    return jnp.concatenate([src_ref[pl.ds(c, n, stride=ROW_TILE), :] for c in range(ROW_TILE)], axis=1)


def _row_tile(ref, p):
    return ref.at[pl.ds(pl.multiple_of(p * ROW_TILE, ROW_TILE), ROW_TILE)]


def _dispatch_kernel(pos_ref, h_ref, xs_in_ref, xs_ref, ht_ref, sem):
    del xs_in_ref
    tm = h_ref.shape[0]
    _rows_to_tiles(h_ref[...], ht_ref)

    def row_copy(i, p):
        return pltpu.make_async_copy(_row_tile(ht_ref, i), _row_tile(xs_ref, p), sem)

    def start(i, c):
        row_copy(i, pos_ref[0, 0, i]).start(priority=0)
        row_copy(i, pos_ref[0, 0, tm + i]).start(priority=1)
        return c

    def wait(i, c):
        row_copy(0, 0).wait()
        row_copy(0, 0).wait()
        return c

    lax.fori_loop(0, tm, start, 0, unroll=8)
    lax.fori_loop(0, tm, wait, 0, unroll=8)


def _dispatch(h, pos, buf, tm=256):
    seq = h.shape[0]
    nt = seq // tm
    pos3 = pos.reshape(2, nt, tm).transpose(1, 0, 2).reshape(nt, 1, 2 * tm)
    return pl.pallas_call(
        _dispatch_kernel,
        grid=(nt,),
        in_specs=[pl.BlockSpec((1, 1, 2 * tm), lambda i: (i, 0, 0), memory_space=pltpu.SMEM),
                  pl.BlockSpec((tm, D_MODEL), lambda i: (i, 0)),
                  pl.BlockSpec(memory_space=pl.ANY)],
        out_specs=pl.BlockSpec(memory_space=pl.ANY),
        out_shape=jax.ShapeDtypeStruct(buf.shape, F32),
        scratch_shapes=[pltpu.VMEM((tm * ROW_TILE, LANES), F32), pltpu.SemaphoreType.DMA(())],
        input_output_aliases={2: 0},
        compiler_params=_params(("arbitrary",)),
        name="moe_dispatch",
    )(pos3, h, buf)


def _expert_kernel(te_ref, x_ref, wgu_ref, wd_ref, y_ref):
    del te_ref
    gu = jnp.dot(_tiles_to_rows(x_ref).astype(BF16), wgu_ref[...], preferred_element_type=F32)
    gate, up = gu[:, :D_EXPERT], gu[:, D_EXPERT:]
    hid = gate * (1.0 / (1.0 + jnp.exp(-gate))) * up
    _rows_to_tiles(jnp.dot(hid.astype(BF16), wd_ref[...], preferred_element_type=F32), y_ref)


def _experts(xs, tile_expert, w_gate_up, w_down, layer):
    blk = MOE_TILE * ROW_TILE
    return pl.pallas_call(
        _expert_kernel,
        grid_spec=pltpu.PrefetchScalarGridSpec(
            num_scalar_prefetch=1,
            grid=(xs.shape[0] // blk,),
            in_specs=[pl.BlockSpec((blk, LANES), lambda i, te: (i, 0)),
                      pl.BlockSpec((None, None, D_MODEL, 2 * D_EXPERT), lambda i, te: (layer, te[i], 0, 0)),
                      pl.BlockSpec((None, None, D_EXPERT, D_MODEL), lambda i, te: (layer, te[i], 0, 0))],
            out_specs=pl.BlockSpec((blk, LANES), lambda i, te: (i, 0))),
        out_shape=jax.ShapeDtypeStruct(xs.shape, F32),
        compiler_params=_params(("arbitrary",)),
        name="moe_experts",
    )(tile_expert, xs, w_gate_up, w_down)


def _combine_kernel(pos_ref, pos_next_ref, ys_ref, h_ref, w_ref, g_ref, b_ref, hn_ref, hb_ref, buf_ref, sem):
    step = pl.program_id(0)
    tm = h_ref.shape[0]
    slot = lax.rem(step, 2)

    def row_copy(s, k, i, p):
        return pltpu.make_async_copy(_row_tile(ys_ref, p), _row_tile(buf_ref.at[s, k], i), sem.at[s])

    def gather(p_ref, s):
        def start(i, c):
            row_copy(s, 0, i, p_ref[0, 0, i]).start(priority=0)
            row_copy(s, 1, i, p_ref[0, 0, tm + i]).start(priority=1)
            return c
        lax.fori_loop(0, tm, start, 0, unroll=8)

    @pl.when(step == 0)
    def _():
        gather(pos_ref, 0)

    @pl.when(step + 1 < pl.num_programs(0))
    def _():
        gather(pos_next_ref, 1 - slot)

    def wait(i, c):
        row_copy(slot, 0, 0, 0).wait()
        row_copy(slot, 1, 0, 0).wait()
        return c

    lax.fori_loop(0, tm, wait, 0, unroll=8)
    w = w_ref[...]
    ffn = (w[:, 0:1] * _tiles_to_rows(buf_ref.at[slot, 0])
           + w[:, 1:2] * _tiles_to_rows(buf_ref.at[slot, 1]))
    _layer_norm_store(DEEPNORM_ALPHA * h_ref[...] + ffn, g_ref, b_ref, hn_ref, hb_ref)


def _combine(ys, pos, wts, h, ln_g, ln_b, tm=256):
    seq = h.shape[0]
    nt = seq // tm
    pos3 = pos.reshape(2, nt, tm).transpose(1, 0, 2).reshape(nt, 1, 2 * tm)
    row = lambda i: (i, 0)
    fix = lambda i: (0, 0)
    return pl.pallas_call(
        _combine_kernel,
        grid=(nt,),
        in_specs=[pl.BlockSpec((1, 1, 2 * tm), lambda i: (i, 0, 0), memory_space=pltpu.SMEM),
                  pl.BlockSpec((1, 1, 2 * tm), lambda i: (jnp.minimum(i + 1, nt - 1), 0, 0),
                               memory_space=pltpu.SMEM),
                  pl.BlockSpec(memory_space=pl.ANY),
                  pl.BlockSpec((tm, D_MODEL), row),
                  pl.BlockSpec((tm, 2), row),
                  pl.BlockSpec((1, D_MODEL), fix),
                  pl.BlockSpec((1, D_MODEL), fix)],
        out_specs=[pl.BlockSpec((tm, D_MODEL), row)] * 2,
        out_shape=[jax.ShapeDtypeStruct((seq, D_MODEL), F32), jax.ShapeDtypeStruct((seq, D_MODEL), BF16)],
        scratch_shapes=[pltpu.VMEM((2, 2, tm * ROW_TILE, LANES), F32), pltpu.SemaphoreType.DMA((2,))],
        compiler_params=_params(("arbitrary",)),
        name="moe_combine_ln",
    )(pos3, pos3, ys, h, wts, ln_g.reshape(1, -1), ln_b.reshape(1, -1))


def _moe_layer(h, w_router, b_router, w_gate_up, w_down, layer, ln_g, ln_b, buf):
    seq = h.shape[0]
    ids, wts, cnt = _router(h, w_router, b_router)
    counts = cnt[:, 0].astype(I32)
    padded = ((counts + MOE_TILE - 1) // MOE_TILE) * MOE_TILE
    ends = jnp.cumsum(padded)
    offs = ends - padded
    eids = jnp.arange(N_EXPERTS, dtype=I32)[:, None, None]
    pos = jnp.sum(jnp.where(ids[None, 0:2] == eids, offs[:, None, None], 0), axis=0) + ids[2:4]
    n_rows = 2 * seq + N_EXPERTS * MOE_TILE
    tile_start = jnp.arange(n_rows // MOE_TILE, dtype=I32) * MOE_TILE
    tile_expert = jnp.minimum(jnp.sum((tile_start[:, None] >= ends[None, :]).astype(I32), axis=1), N_EXPERTS - 1)
    if buf is None:
        buf = jnp.zeros((n_rows * ROW_TILE, LANES), F32)
    xs = _dispatch(h, pos, buf)
    ys = _experts(xs, tile_expert.astype(I32), w_gate_up, w_down, layer)
    return _combine(ys, pos, wts[0:2].T, h, ln_g, ln_b) + (ys,)


B_KV_GROUPS = 4
B_HPG = N_HEADS // B_KV_GROUPS
CMP_LEN = 32
CMP_STRIDE = 16
CMP_HIDDEN = 256
SLC_BLOCK = 64
N_SELECT = 16
WIN = 512
FORCE_SCORE = 1e4
N_FORCED = 3
LIST_W = 272
LOG2E = math.log2(math.e)
MASK_PENALTY = 1e30
M_FLOOR = -1e29
CMP_STEP_BLOCKS = 2
CMP_PARTS = 8
SEL_CHUNK = 4


def _group_slopes(g):
    sl = [v * LOG2E for v in _slopes(N_HEADS)]
    out = []
    for r in range(B_HPG):
        v = jnp.full((1, 1), sl[r], F32)
        for k in range(1, B_KV_GROUPS):
            v = jnp.where(g == k, sl[k * B_HPG + r], v)
        out.append(v)
    return out


def _qproj_kernel(hb_ref, w_ref, b_ref, qt_ref, gt_ref):
    pt = lax.dot_general(w_ref[...], hb_ref[...], (((1,), (1,)), ((), ())), preferred_element_type=F32)
    pt = pt + b_ref[:, 0:1]
    qt_ref[...] = (pt[:D_MODEL] * (HEAD_DIM ** -0.5 * LOG2E)).astype(BF16)
    gt_ref[...] = 1.0 / (1.0 + jnp.exp(-pt[D_MODEL:]))


def _qproj(hb, w_q, b_q, tm=512):
    seq = hb.shape[0]
    n = w_q.shape[1]
    npad = D_MODEL + LANES
    w_t = jnp.zeros((npad, D_MODEL), BF16).at[:n].set(w_q.T.astype(BF16))
    b_t = jnp.zeros((npad, LANES), F32).at[:n].set(jnp.broadcast_to(b_q[:, None], (n, LANES)))
    return pl.pallas_call(
        _qproj_kernel,
        grid=(seq // tm,),
        in_specs=[pl.BlockSpec((tm, D_MODEL), lambda i: (i, 0)),
                  pl.BlockSpec((npad, D_MODEL), lambda i: (0, 0)),
                  pl.BlockSpec((npad, LANES), lambda i: (0, 0))],
        out_specs=[pl.BlockSpec((D_MODEL, tm), lambda i: (0, i)),
                   pl.BlockSpec((LANES, tm), lambda i: (0, i))],
        out_shape=[jax.ShapeDtypeStruct((D_MODEL, seq), BF16), jax.ShapeDtypeStruct((LANES, seq), F32)],
        compiler_params=_params(("arbitrary",)),
        name="nsa_qproj",
    )(hb, w_t, b_t)


def _compress_kernel(x_ref, w1a_ref, w1b_ref, pa_ref, pb_ref, b1_ref, w2_ref, b2_ref, o_ref):
    x = x_ref[0, 0]
    n = x.shape[0]
    a = jnp.dot((x + pa_ref[0]).astype(BF16), w1a_ref[0], preferred_element_type=F32)
    b = jnp.dot((x + pb_ref[0]).astype(BF16), w1b_ref[0], preferred_element_type=F32)
    b = pltpu.roll(b, n - 1, 0)
    u = a + b + b1_ref[0]
    hid = 0.5 * u * (1.0 + jnp.tanh(math.sqrt(2.0 / math.pi) * (u + 0.044715 * (u * u * u))))
    o_ref[0, 0] = (jnp.dot(hid.astype(BF16), w2_ref[0], preferred_element_type=F32) + b2_ref[0]).astype(BF16)


def _compress(kvc, cmp_pos, cmp_w1, cmp_b1, cmp_w2, cmp_b2):
    seq = kvc.shape[0]
    nch = seq // CMP_STRIDE
    half = CMP_STRIDE * HEAD_DIM
    chunks = kvc.reshape(nch, CMP_STRIDE, 2, B_KV_GROUPS, HEAD_DIM).transpose(2, 3, 0, 1, 4).reshape(
        2, B_KV_GROUPS, nch, half)
    pos = cmp_pos.reshape(2, 1, CMP_LEN * HEAD_DIM)
    w1 = cmp_w1.astype(BF16)
    jfix = lambda j, g: (j, 0, 0)
    return pl.pallas_call(
        _compress_kernel,
        grid=(2, B_KV_GROUPS),
        in_specs=[pl.BlockSpec((1, 1, nch, half), lambda j, g: (j, g, 0, 0)),
                  pl.BlockSpec((1, half, CMP_HIDDEN), jfix),
                  pl.BlockSpec((1, half, CMP_HIDDEN), lambda j, g: (j, 1, 0)),
                  pl.BlockSpec((1, 1, half), jfix),
                  pl.BlockSpec((1, 1, half), lambda j, g: (j, 0, 1)),
                  pl.BlockSpec((1, 1, CMP_HIDDEN), jfix),
                  pl.BlockSpec((1, CMP_HIDDEN, HEAD_DIM), jfix),
                  pl.BlockSpec((1, 1, HEAD_DIM), jfix)],
        out_specs=pl.BlockSpec((1, 1, nch, HEAD_DIM), lambda j, g: (j, g, 0, 0)),
        out_shape=jax.ShapeDtypeStruct((2, B_KV_GROUPS, nch, HEAD_DIM), BF16),
        compiler_params=_params(("arbitrary", "arbitrary")),
        name="nsa_compress",
    )(chunks, w1, w1, pos, pos, cmp_b1.reshape(2, 1, CMP_HIDDEN), cmp_w2.astype(BF16),
      cmp_b2.reshape(2, 1, HEAD_DIM))


def _nsa_shared_kv(hb, w_kv, cmp_pos, cmp_w1, cmp_b1, cmp_w2, cmp_b2):
    seq = hb.shape[0]
    gw = B_KV_GROUPS * HEAD_DIM
    w = w_kv.astype(BF16)
    kvc = _matmul(hb, w[:, :2 * gw], F32, 512, 2 * gw, "kv_proj_cmp")
    kvr = _matmul(hb, w[:, 2 * gw:], BF16, 512, 4 * gw, "kv_proj_rest")
    cmp = _compress(kvc, cmp_pos, cmp_w1, cmp_b1, cmp_w2, cmp_b2)
    n_slc, n_win = seq // SLC_BLOCK, seq // Q_BLOCK

    def blocks(cols, nblk, blk, transpose):
        t = cols.reshape(nblk, blk, B_KV_GROUPS, HEAD_DIM)
        return t.transpose(2, 0, 3, 1) if transpose else t.transpose(2, 0, 1, 3)

    return dict(
        kc=cmp[0], vct=cmp[1].transpose(0, 2, 1),
        ks=blocks(kvr[:, 0 * gw:1 * gw], n_slc, SLC_BLOCK, False),
        vst=blocks(kvr[:, 1 * gw:2 * gw], n_slc, SLC_BLOCK, False),
        kw=blocks(kvr[:, 2 * gw:3 * gw], n_win, Q_BLOCK, False),
        vwt=blocks(kvr[:, 3 * gw:4 * gw], n_win, Q_BLOCK, True))


def _select_matrix(n_slc, nch):
    m = np.zeros((n_slc, nch), np.float32)
    r_s, r_c = SLC_BLOCK // CMP_STRIDE, CMP_LEN // CMP_STRIDE
    for a in range(r_s):
        for b in range(r_c):
            idx = r_s * np.arange(n_slc) + a - b
            ok = (idx >= 0) & (idx < nch - r_c + 1)
            m[np.arange(n_slc)[ok], idx[ok]] += 1.0
    return jnp.asarray(m, BF16)


def _q_heads(qt_ref, lanes=slice(None)):
    return jnp.concatenate([qt_ref[r * HEAD_DIM:(r + 1) * HEAD_DIM, lanes] for r in range(B_HPG)], axis=1)


def _nsa_cmp_kernel(qt_ref, kc_ref, vct_ref, gt_ref, mt_ref, oc_ref, sel_ref, flag_ref, bias_ref):
    g = pl.program_id(0)
    i = pl.program_id(1)
    nch = kc_ref.shape[1]
    n_slc = mt_ref.shape[0]
    first_blk = i * CMP_STEP_BLOCKS

    @pl.when(i == 0)
    def _():
        slopes = _group_slopes(g)
        rel = (lax.broadcasted_iota(I32, (nch, Q_BLOCK), 1)
               - (CMP_STRIDE * lax.broadcasted_iota(I32, (nch, Q_BLOCK), 0) + (CMP_LEN - 1))).astype(F32)
        for r in range(B_HPG):
            bias_ref[r] = slopes[r] * rel

    def compressed_branch(n):
        for sub in range(CMP_STEP_BLOCKS):
            one_block(n, sub)

    def one_block(n, sub):
        t0 = (first_blk + sub) * Q_BLOCK
        lanes = slice(sub * Q_BLOCK, (sub + 1) * Q_BLOCK)
        cidx = lax.broadcasted_iota(I32, (n, Q_BLOCK), 0)
        qidx = lax.broadcasted_iota(I32, (n, Q_BLOCK), 1)
        valid = (t0 + qidx) - (CMP_STRIDE * cidx + (CMP_LEN - 1)) >= 0
        st = jnp.dot(kc_ref[0, :n, :], _q_heads(qt_ref, lanes), preferred_element_type=F32)
        imp = jnp.zeros((n, Q_BLOCK), F32)
        outs = []
        for r in range(B_HPG):
            s = st[:, r * Q_BLOCK:(r + 1) * Q_BLOCK] - bias_ref[r, :n, :]
            s = jnp.where(valid, s, NEG_INF)
            m = jnp.maximum(jnp.max(s, axis=0, keepdims=True), M_FLOOR)
            p = jnp.exp2(s - m)
            l = jnp.sum(p, axis=0, keepdims=True)
            pn = p * (1.0 / jnp.maximum(l, TINY))
            imp = imp + pn
            o = jnp.dot(vct_ref[0, :, :n], pn.astype(BF16), preferred_element_type=F32)
            outs.append(o * gt_ref[pl.ds(g * B_HPG + r, 1), :][:, lanes])
        oc_ref[lanes, :] = jnp.concatenate(outs, axis=0).T
        hi = imp.astype(BF16)
        r1 = imp - hi.astype(F32)
        mid = r1.astype(BF16)
        lo = (r1 - mid.astype(F32)).astype(BF16)
        ns = n * n_slc // nch
        mt = mt_ref[:ns, :n]
        p_slc = (jnp.dot(mt, hi, preferred_element_type=F32) + jnp.dot(mt, mid, preferred_element_type=F32)
                 + jnp.dot(mt, lo, preferred_element_type=F32))
        sid = lax.broadcasted_iota(I32, (ns, Q_BLOCK), 0)
        cur = (t0 + lax.broadcasted_iota(I32, (ns, Q_BLOCK), 1)) // SLC_BLOCK
        forced = (sid == 0) | (sid == cur) | (sid == cur - 1)
        score = jnp.where(forced, -jnp.inf, p_slc)
        score = jnp.where(sid <= cur, score, -FORCE_SCORE)
        for _ in range(N_SELECT - N_FORCED):
            mx = jnp.max(score, axis=0, keepdims=True)
            first = jnp.min(jnp.where(score == mx, sid, ns), axis=0, keepdims=True)
            score = jnp.where(sid == first, -jnp.inf, score)
        sel = jnp.where((score == -jnp.inf) & (sid <= cur), 1.0, 0.0)
        sel_ref[0, sub, :ns, :] = sel
        flag_ref[0, sub, :ns, :] = jnp.max(sel, axis=1, keepdims=True).astype(I32)
        if ns < n_slc:
            sel_ref[0, sub, ns:, :] = jnp.zeros((n_slc - ns, Q_BLOCK), F32)
            flag_ref[0, sub, ns:, :] = jnp.zeros((n_slc - ns, 1), I32)

    n_parts = min(CMP_PARTS, n_slc // N_SELECT)
    part = nch // n_parts
    lax.switch(lax.div(8 * (first_blk + CMP_STEP_BLOCKS - 1) + 6, part),
               [functools.partial(compressed_branch, part * (k + 1)) for k in range(n_parts)])


def _nsa_cmp(qt, gt, kv, seq):
    nb = seq // Q_BLOCK
    nch, n_slc = seq // CMP_STRIDE, seq // SLC_BLOCK
    gw = B_HPG * HEAD_DIM
    nq = CMP_STEP_BLOCKS * Q_BLOCK
    return pl.pallas_call(
        _nsa_cmp_kernel,
        grid=(B_KV_GROUPS, nb // CMP_STEP_BLOCKS),
        in_specs=[pl.BlockSpec((gw, nq), lambda g, i: (g, i)),
                  pl.BlockSpec((1, nch, HEAD_DIM), lambda g, i: (g, 0, 0)),
                  pl.BlockSpec((1, HEAD_DIM, nch), lambda g, i: (g, 0, 0)),
                  pl.BlockSpec((LANES, nq), lambda g, i: (0, i)),
                  pl.BlockSpec((n_slc, nch), lambda g, i: (0, 0))],
        out_specs=[pl.BlockSpec((nq, gw), lambda g, i: (i, g)),
                   pl.BlockSpec((1, CMP_STEP_BLOCKS, n_slc, Q_BLOCK), lambda g, i: (g, i, 0, 0)),
                   pl.BlockSpec((1, CMP_STEP_BLOCKS, n_slc, 1), lambda g, i: (g, i, 0, 0))],
        out_shape=[jax.ShapeDtypeStruct((seq, D_MODEL), F32),
                   jax.ShapeDtypeStruct((B_KV_GROUPS, nb, n_slc, Q_BLOCK), F32),
                   jax.ShapeDtypeStruct((B_KV_GROUPS, nb, n_slc, 1), I32)],
        scratch_shapes=[pltpu.VMEM((B_HPG, nch, Q_BLOCK), F32)],
        compiler_params=_params(("arbitrary", "arbitrary")),
        name="nsa_compressed_select",
    )(qt, kv["kc"], kv["vct"], gt, _select_matrix(n_slc, nch))


def _fold8(x, op):
    return functools.reduce(op, [x[8 * k:8 * k + 8] for k in range(x.shape[0] // 8)])


def _softmax_tiles(sts, tiles, rel_bias, slopes, m_old):
    ps = [[] for _ in tiles]
    ms, ls, alphas = [], [], []
    for r in range(B_HPG):
        cols = slice(r * Q_BLOCK, (r + 1) * Q_BLOCK)
        us, shifts, tops = [], [], []
        for st, (mask, penalty, off) in zip(sts, tiles):
            u = st[:, cols] - rel_bias[r]
            u = u if mask is None else jnp.where(mask, u, NEG_INF)
            shift = slopes[r] * jnp.asarray(off, F32)
            shift = shift if penalty is None else shift + penalty
            us.append(u)
            shifts.append(shift)
            tops.append(_fold8(u, jnp.maximum) - shift)
        m_new = jnp.max(functools.reduce(jnp.maximum, tops), axis=0, keepdims=True)
        m_new = jnp.maximum(m_new, M_FLOOR)
        if m_old is not None:
            m_new = jnp.maximum(m_old[:, cols], m_new)
            alphas.append(jnp.exp2(m_old[:, cols] - m_new))
        l8 = jnp.zeros((8, Q_BLOCK), F32)
        for t, (u, shift) in enumerate(zip(us, shifts)):
            p = jnp.exp2(u - (m_new + shift))
            l8 = l8 + _fold8(p, jnp.add)
            ps[t].append(p.astype(BF16))
        ms.append(m_new)
        ls.append(jnp.sum(l8, axis=0, keepdims=True))
    ps = [jnp.concatenate(p, axis=1) for p in ps]
    alpha = None if m_old is None else jnp.concatenate(alphas, axis=1)
    return ps, jnp.concatenate(ms, axis=1), jnp.concatenate(ls, axis=1), alpha


def _nsa_sparse_kernel(lst_ref, qt_ref, ks_ref, vst_ref, kw_ref, vwt_ref, sel_ref, gt_ref, oc_ref, o_ref,
                       m_ref, alpha_ref, l_ref, accw_ref, acc_ref, st_ref, p_ref):
    g = pl.program_id(0)
    i = pl.program_id(1)
    t0 = i * Q_BLOCK
    slopes = _group_slopes(g)
    q4 = _q_heads(qt_ref)

    def gate(branch):
        return jnp.concatenate(
            [gt_ref[pl.ds(branch * N_HEADS + g * B_HPG + r, 1), :] for r in range(B_HPG)], axis=1)

    rel_w = (lax.broadcasted_iota(I32, (Q_BLOCK, Q_BLOCK), 1)
             - lax.broadcasted_iota(I32, (Q_BLOCK, Q_BLOCK), 0))
    n_prev = WIN // Q_BLOCK
    rel_w_f = rel_w.astype(F32)
    bias_w = [slopes[r] * rel_w_f for r in range(B_HPG)]
    tiles, blks = [], []
    for c in range(n_prev + 1):
        blk = i - n_prev + c
        off = (n_prev - c) * Q_BLOCK
        mask = (rel_w >= 0) if c == n_prev else (rel_w < 0) if c == 0 else None
        penalty = None if c == n_prev else jnp.where(blk >= 0, 0.0, MASK_PENALTY).astype(F32)
        tiles.append((mask, penalty, off))
        blks.append(jnp.maximum(blk, 0))
    sts = [jnp.dot(kw_ref[0, blk], q4, preferred_element_type=F32) for blk in blks]
    ps, _, l_w, _ = _softmax_tiles(sts, tiles, bias_w, slopes, None)
    acc_w = functools.reduce(lambda a, b: a + b, [
        jnp.dot(vwt_ref[0, blk], p, preferred_element_type=F32) for blk, p in zip(blks, ps)])
    accw_ref[...] = acc_w * (gate(2) / jnp.maximum(l_w, TINY))

    rel_s = (lax.broadcasted_iota(I32, (SLC_BLOCK, Q_BLOCK), 1)
             - lax.broadcasted_iota(I32, (SLC_BLOCK, Q_BLOCK), 0))
    rel_s_f = rel_s.astype(F32)
    bias_s = [slopes[r] * rel_s_f for r in range(B_HPG)]

    def block_id(j, b):
        return lst_ref[0, 0, j * SEL_CHUNK + b]

    def rows(b):
        return slice(b * SLC_BLOCK, (b + 1) * SLC_BLOCK)

    def scores(j, slot):
        for b in range(SEL_CHUNK):
            st_ref[slot, rows(b), :] = jnp.dot(ks_ref[0, block_id(j, b)], q4, preferred_element_type=F32)

    def weighted_values(j):
        v = jnp.concatenate([vst_ref[0, block_id(j, b)] for b in range(SEL_CHUNK)], axis=0)
        return lax.dot_general(v, p_ref[...], (((0,), (0,)), ((), ())), preferred_element_type=F32)

    m_ref[...] = jnp.full(m_ref.shape, M_FLOOR, F32)
    alpha_ref[...] = jnp.ones(alpha_ref.shape, F32)
    l_ref[...] = jnp.zeros(l_ref.shape, F32)
    acc_ref[...] = jnp.zeros(acc_ref.shape, F32)
    p_ref[...] = jnp.zeros(p_ref.shape, BF16)
    scores(0, 0)

    def sel_chunk(j, slot):
        acc_ref[...] = acc_ref[...] * alpha_ref[0:1, :] + weighted_values(jnp.maximum(j - 1, 0))
        tiles, sts = [], []
        for b in range(SEL_CHUNK):
            blk = block_id(j, b)
            off = t0 - blk * SLC_BLOCK
            penalty = jnp.where(sel_ref[0, 0, pl.ds(blk, 1), :] > 0.5, 0.0, MASK_PENALTY)
            mask = (rel_s >= -off) if b < 2 else None
            tiles.append((mask, penalty, off))
            sts.append(st_ref[slot, rows(b), :])
        ps, m, l, alpha = _softmax_tiles(sts, tiles, bias_s, slopes, m_ref[0:1, :])
        for b in range(SEL_CHUNK):
            p_ref[rows(b), :] = ps[b]
        m_ref[0:1, :] = m
        l_ref[0:1, :] = alpha * l_ref[0:1, :] + l
        alpha_ref[0:1, :] = alpha
        scores(j + 1, 1 - slot)

    def sel_pair(jj, c):
        sel_chunk(2 * jj, 0)
        sel_chunk(2 * jj + 1, 1)
        return c

    n_pairs = lst_ref[0, 0, LIST_W - 1]
    lax.fori_loop(0, n_pairs, sel_pair, 0)
    acc_s = acc_ref[...] * alpha_ref[0:1, :] + weighted_values(jnp.maximum(2 * n_pairs - 1, 0))

    total = accw_ref[...] + acc_s * (gate(1) / jnp.maximum(l_ref[0:1, :], TINY))
    ot = jnp.concatenate([total[:, r * Q_BLOCK:(r + 1) * Q_BLOCK] for r in range(B_HPG)], axis=0)
    o_ref[...] = (ot.T + oc_ref[...]).astype(BF16)


def _nsa_sparse(lists, qt, gt, kv, sel, oc, seq):
    nb = seq // Q_BLOCK
    n_slc, n_win = seq // SLC_BLOCK, seq // Q_BLOCK
    gw = B_HPG * HEAD_DIM
    grp = lambda g, i: (g, 0, 0, 0)
    return pl.pallas_call(
        _nsa_sparse_kernel,
        grid=(B_KV_GROUPS, nb),
        in_specs=[pl.BlockSpec((1, 1, LIST_W), lambda g, i: (g * nb + i, 0, 0), memory_space=pltpu.SMEM),
                  pl.BlockSpec((gw, Q_BLOCK), lambda g, i: (g, i)),
                  pl.BlockSpec((1, n_slc, SLC_BLOCK, HEAD_DIM), grp),
                  pl.BlockSpec((1, n_slc, SLC_BLOCK, HEAD_DIM), grp),
                  pl.BlockSpec((1, n_win, Q_BLOCK, HEAD_DIM), grp),
                  pl.BlockSpec((1, n_win, HEAD_DIM, Q_BLOCK), grp),
                  pl.BlockSpec((1, 1, n_slc, Q_BLOCK), lambda g, i: (g, i, 0, 0)),
                  pl.BlockSpec((LANES, Q_BLOCK), lambda g, i: (0, i)),
                  pl.BlockSpec((Q_BLOCK, gw), lambda g, i: (i, g))],
        out_specs=pl.BlockSpec((Q_BLOCK, gw), lambda g, i: (i, g)),
        scratch_shapes=[pltpu.VMEM((8, B_HPG * Q_BLOCK), F32),
                        pltpu.VMEM((8, B_HPG * Q_BLOCK), F32),
                        pltpu.VMEM((8, B_HPG * Q_BLOCK), F32),
                        pltpu.VMEM((HEAD_DIM, B_HPG * Q_BLOCK), F32),
                        pltpu.VMEM((HEAD_DIM, B_HPG * Q_BLOCK), F32),
                        pltpu.VMEM((2, SEL_CHUNK * SLC_BLOCK, B_HPG * Q_BLOCK), F32),
                        pltpu.VMEM((SEL_CHUNK * SLC_BLOCK, B_HPG * Q_BLOCK), BF16)],
        out_shape=jax.ShapeDtypeStruct((seq, D_MODEL), BF16),
        compiler_params=_params(("arbitrary", "arbitrary")),
        name="nsa_selected_window",
    )(lists, qt, kv["ks"], kv["vst"], kv["kw"], kv["vwt"], sel, gt, oc)


def _outproj_b_kernel(o_ref, h_ref, w_ref, g_ref, b_ref, hn_ref, hb_ref):
    mix = jnp.dot(o_ref[...], w_ref[...], preferred_element_type=F32)
    _layer_norm_store(DEEPNORM_ALPHA * h_ref[...] + mix, g_ref, b_ref, hn_ref, hb_ref)


def _outproj_b(o, h, w_out, ln_g, ln_b, tm=512):
    seq = h.shape[0]
    row = lambda i: (i, 0)
    fix = lambda i: (0, 0)
    return pl.pallas_call(
        _outproj_b_kernel,
        grid=(seq // tm,),
        in_specs=[pl.BlockSpec((tm, D_MODEL), row), pl.BlockSpec((tm, D_MODEL), row),
                  pl.BlockSpec((D_MODEL, D_MODEL), fix), pl.BlockSpec((1, D_MODEL), fix),
                  pl.BlockSpec((1, D_MODEL), fix)],
        out_specs=[pl.BlockSpec((tm, D_MODEL), row)] * 2,
        out_shape=[jax.ShapeDtypeStruct((seq, D_MODEL), F32), jax.ShapeDtypeStruct((seq, D_MODEL), BF16)],
        compiler_params=_params(("arbitrary",)),
        name="outproj_ln_b",
    )(o, h, w_out.astype(BF16), ln_g.reshape(1, -1), ln_b.reshape(1, -1))


def _nsa_layer(h, hb, kv, w_q, b_q, w_out, ln_g, ln_b):
    seq = h.shape[0]
    nb, n_slc = seq // Q_BLOCK, seq // SLC_BLOCK
    qt, gt = _qproj(hb, w_q, b_q)
    oc, sel, flags = _nsa_cmp(qt, gt, kv, seq)
    f = flags.reshape(B_KV_GROUPS, nb, n_slc)
    newest = 2 * jnp.arange(nb, dtype=I32)[None, :, None] + 1
    age = jnp.mod(newest - jnp.arange(n_slc, dtype=I32)[None, None, :], n_slc)
    order = jnp.argsort((1 - f) * n_slc + age, axis=2).astype(I32).reshape(B_KV_GROUPS * nb, n_slc)
    n_pairs = (jnp.sum(f, axis=2).reshape(-1) + 2 * SEL_CHUNK - 1) // (2 * SEL_CHUNK)
    lists = jnp.concatenate(
        [order, jnp.zeros((B_KV_GROUPS * nb, LIST_W - n_slc - 1), I32), n_pairs.astype(I32)[:, None]], axis=1)
    o = _nsa_sparse(lists.reshape(B_KV_GROUPS * nb, 1, LIST_W), qt, gt, kv, sel, oc, seq)
    return _outproj_b(o, h, w_out, ln_g, ln_b)


def kernel(x, a_w_in, a_w_out, b_w_kv, b_cmp_pos, b_cmp_w1, b_cmp_b1, b_cmp_w2, b_cmp_b2, b_w_q, b_b_q, b_w_out, moe_w_router, moe_b_router, moe_w_gate_up, moe_w_down, ln_mix_g, ln_mix_b, ln_ffn_g, ln_ffn_b):
    seq = x.shape[1]
    h = x.reshape(seq, D_MODEL)
    hb = h.astype(BF16)
    a_w_in_b, w_gate_up_b, w_down_b = a_w_in.astype(BF16), moe_w_gate_up.astype(BF16), moe_w_down.astype(BF16)
    buf = None
    for layer in range(DEPTH):
        if layer < N_A_LAYERS:
            h, hb = _dilated_layer(h, hb, a_w_in_b, layer, a_w_out[layer], ln_mix_g[layer], ln_mix_b[layer])
        else:
            if layer == N_A_LAYERS:
                kv = _nsa_shared_kv(hb, b_w_kv, b_cmp_pos, b_cmp_w1, b_cmp_b1, b_cmp_w2, b_cmp_b2)
            j = layer - N_A_LAYERS
            h, hb = _nsa_layer(h, hb, kv, b_w_q[j], b_b_q[j], b_w_out[j], ln_mix_g[layer], ln_mix_b[layer])
        h, hb, buf = _moe_layer(h, moe_w_router[layer], moe_b_router[layer], w_gate_up_b, w_down_b, layer,
                                ln_ffn_g[layer], ln_ffn_b[layer], buf)
    return h.reshape(1, seq, D_MODEL)
```

```python
import functools
import math

import numpy as np
import jax
import jax.numpy as jnp
from jax import lax
from jax.experimental import pallas as pl
from jax.experimental.pallas import tpu as pltpu

F32 = jnp.float32
BF16 = jnp.bfloat16
I32 = jnp.int32

D_MODEL = 1024
DEPTH = 4
N_A_LAYERS = DEPTH // 2
HEAD_DIM = 64
N_HEADS = D_MODEL // HEAD_DIM
Q_BLOCK = 128
LN_EPS = 1e-5
NEG_INF = -1e30
TINY = 1e-30

A_WINDOWS = (128, 512, 2048)
A_DILATIONS = (1, 4, 16)
A_N_GROUPS = 3
DIL_STEP_BLOCKS = 8

N_EXPERT_GROUPS = 4
EXPERTS_PER_GROUP = 4
N_EXPERTS = 16
D_EXPERT = 256

DEEPNORM_ALPHA = (2.0 * DEPTH) ** 0.25

LANES = 128
MOE_TILE = 256
VMEM_LIMIT = 56 * 1024 * 1024


def _slopes(n):
    return [float(v) for v in np.asarray(2.0 ** (-8.0 * np.arange(1, n + 1) / n), dtype=np.float32)]


def _params(sem):
    return pltpu.CompilerParams(dimension_semantics=sem, vmem_limit_bytes=VMEM_LIMIT)


def _mm_kernel(x_ref, w_ref, o_ref):
    o_ref[...] = jnp.dot(x_ref[...], w_ref[...], preferred_element_type=F32).astype(o_ref.dtype)


def _matmul(x, w, out_dtype, tm, tn, name):
    m, k = x.shape
    n = w.shape[1]
    return pl.pallas_call(
        _mm_kernel,
        grid=(n // tn, m // tm),
        in_specs=[pl.BlockSpec((tm, k), lambda j, i: (i, 0)),
                  pl.BlockSpec((k, tn), lambda j, i: (0, j))],
        out_specs=pl.BlockSpec((tm, tn), lambda j, i: (i, j)),
        out_shape=jax.ShapeDtypeStruct((m, n), out_dtype),
        compiler_params=_params(("arbitrary", "arbitrary")),
        name=name,
    )(x, w)


def _layer_norm_store(z, g_ref, b_ref, h_ref, hb_ref):
    mu = jnp.mean(z, axis=-1, keepdims=True)
    zc = z - mu
    var = jnp.mean(zc * zc, axis=-1, keepdims=True)
    y = zc * lax.rsqrt(var + LN_EPS) * g_ref[...] + b_ref[...]
    h_ref[...] = y
    hb_ref[...] = y.astype(BF16)


def _split_bf16(x):
    hi = x.astype(BF16)
    lo = (x - hi.astype(F32)).astype(BF16)
    return hi, lo


def _dil_attn_kernel(q_ref, kp_ref, kc_ref, vp_ref, vc_ref, o_ref, lse_ref, bias_ref, *, dil, slopes):
    r = pl.program_id(0)
    jb = pl.program_id(1)
    qb = Q_BLOCK

    @pl.when((r == 0) & (jb == 0))
    def _():
        key = lax.broadcasted_iota(I32, (2 * qb, qb), 0)
        qry = lax.broadcasted_iota(I32, (2 * qb, qb), 1)
        dist = qry + qb - key
        valid = (dist >= 0) & (dist <= qb)
        dist_tok = (dist * dil).astype(F32)
        for h in range(N_HEADS):
            b = jnp.where(valid, -(slopes[h] * LOG2E * dist_tok), NEG_INF)
            bias_ref[1, h] = b
            bias_ref[0, h] = jnp.where(key >= qb, b, NEG_INF)

    lane = lax.broadcasted_iota(I32, (1, LANES), 1)
    lo_half = lane < HEAD_DIM
    lo_rows = lax.broadcasted_iota(I32, (LANES, qb), 0) < HEAD_DIM
    for sub in range(DIL_STEP_BLOCKS):
        rows = slice(sub * qb, (sub + 1) * qb)
        before = slice((sub - 1) * qb, sub * qb)
        first = jnp.minimum(jb, 1) if sub == 0 else 1
        lses = []
        for hp in range(N_HEADS // 2):
            sl = slice(hp * LANES, (hp + 1) * LANES)
            q2 = q_ref[rows, sl]
            k_prev = kp_ref[:, sl] if sub == 0 else kc_ref[before, sl]
            v_prev = vp_ref[:, sl] if sub == 0 else vc_ref[before, sl]
            k2 = jnp.concatenate([k_prev, kc_ref[rows, sl]], axis=0)
            v2t = jnp.concatenate([v_prev, vc_ref[rows, sl]], axis=0).T
            outs = []
            for half in range(2):
                keep = lo_half if half == 0 else jnp.logical_not(lo_half)
                qm = jnp.where(keep, q2, jnp.zeros_like(q2))
                st = lax.dot_general(k2, qm, (((1,), (1,)), ((), ())), preferred_element_type=F32)
                st = st + bias_ref[first, 2 * hp + half]
                m = jnp.max(st, axis=0, keepdims=True)
                p = jnp.exp2(st - m)
                l = jnp.sum(p, axis=0, keepdims=True)
                ot = jnp.dot(v2t, p.astype(BF16), preferred_element_type=F32)
                outs.append(ot * (1.0 / l))
                lses.append(m + jnp.log2(l))
            o_ref[rows, sl] = jnp.where(lo_rows, outs[0], outs[1]).T.astype(BF16)
        lse_t = jnp.concatenate(lses + [jnp.zeros((LANES - N_HEADS, qb), F32)], axis=0)
        lse_ref[rows, :] = lse_t.T


def _qkv_kernel(x_ref, w_ref, o_ref, acc_ref, *, dil):
    acc = jnp.dot(x_ref[...], w_ref[...], preferred_element_type=F32)
    acc = acc * jnp.where(pl.program_id(0) == 0, HEAD_DIM ** -0.5 * LOG2E, 1.0)
    if dil == 1:
        o_ref[0] = acc.astype(BF16)
    else:
        n_chunks = acc.shape[1] // LANES
        for c in range(n_chunks):
            acc_ref[c] = acc[:, c * LANES:(c + 1) * LANES]
        n = acc.shape[0] // dil
        for r in range(dil):
            o_ref[r] = jnp.concatenate(
                [acc_ref[c, pl.ds(r, n, stride=dil), :] for c in range(n_chunks)], axis=1).astype(BF16)


def _qkv_group(hb, w_in, layer, g, tm=1024):
    seq = hb.shape[0]
    dil = A_DILATIONS[g]
    return pl.pallas_call(
        functools.partial(_qkv_kernel, dil=dil),
        grid=(3, seq // tm),
        in_specs=[pl.BlockSpec((tm, D_MODEL), lambda c, i: (i, 0)),
                  pl.BlockSpec((None, D_MODEL, D_MODEL), lambda c, i: (layer, 0, 3 * g + c))],
        out_specs=pl.BlockSpec((dil, tm // dil, D_MODEL), lambda c, i: (0, i, c)),
        out_shape=jax.ShapeDtypeStruct((dil, seq // dil, 3 * D_MODEL), BF16),
        scratch_shapes=[pltpu.VMEM((D_MODEL // LANES, tm, LANES), F32)],
        compiler_params=_params(("arbitrary", "arbitrary")),
        name=f"qkv_proj_g{g}",
    )(hb, w_in)


def _dilated_group(qkv_g, g, seq):
    dil = A_DILATIONS[g]
    rows = seq // dil
    nblk = rows // Q_BLOCK
    nq = DIL_STEP_BLOCKS * Q_BLOCK
    prev = (None, Q_BLOCK, D_MODEL)
    blk = (None, nq, D_MODEL)
    kern = functools.partial(_dil_attn_kernel, dil=dil, slopes=_slopes(N_HEADS))
    return pl.pallas_call(
        kern,
        grid=(dil, nblk // DIL_STEP_BLOCKS),
        in_specs=[
            pl.BlockSpec(blk, lambda r, j: (r, j, 0)),
            pl.BlockSpec(prev, lambda r, j: (r, jnp.maximum(DIL_STEP_BLOCKS * j - 1, 0), 1)),
            pl.BlockSpec(blk, lambda r, j: (r, j, 1)),
            pl.BlockSpec(prev, lambda r, j: (r, jnp.maximum(DIL_STEP_BLOCKS * j - 1, 0), 2)),
            pl.BlockSpec(blk, lambda r, j: (r, j, 2)),
        ],
        out_specs=[pl.BlockSpec(blk, lambda r, j: (r, j, 0)),
                   pl.BlockSpec((None, nq, LANES), lambda r, j: (r, j, 0))],
        out_shape=[jax.ShapeDtypeStruct((dil, rows, D_MODEL), BF16),
                   jax.ShapeDtypeStruct((dil, rows, LANES), F32)],
        scratch_shapes=[pltpu.VMEM((2, N_HEADS, 2 * Q_BLOCK, Q_BLOCK), F32)],
        compiler_params=_params(("arbitrary", "arbitrary")),
        name=f"dilated_attn_g{g}",
    )(qkv_g, qkv_g, qkv_g, qkv_g, qkv_g)


def _head_expand_matrix():
    h = np.arange(LANES)[:, None]
    c = np.arange(D_MODEL)[None, :]
    return jnp.asarray((c // HEAD_DIM == h).astype(np.float32), BF16)


def _to_token_order(src_ref, dst_ref, dil):
    n = src_ref.shape[1]
    n_chunks = dst_ref.shape[0]
    if dil == 1:
        return src_ref[0].astype(F32)
    for r in range(dil):
        v = src_ref[r].astype(F32)
        for c in range(n_chunks):
            dst_ref[c, pl.ds(r, n, stride=dil), :] = v[:, c * LANES:(c + 1) * LANES]
    return jnp.concatenate([dst_ref[c] for c in range(n_chunks)], axis=1)


def _outproj_a_kernel(o0_ref, o1_ref, o2_ref, l0_ref, l1_ref, l2_ref, h_ref, w_ref, e_ref, g_ref, b_ref,
                      hn_ref, hb_ref, onat_ref, lnat_ref):
    l0, l1, l2 = [_to_token_order(l_ref, lnat_ref, A_DILATIONS[gi])
                  for gi, l_ref in enumerate((l0_ref, l1_ref, l2_ref))]
    m = jnp.maximum(jnp.maximum(l0, l1), l2)
    e0, e1, e2 = jnp.exp2(l0 - m), jnp.exp2(l1 - m), jnp.exp2(l2 - m)
    inv = 1.0 / (e0 + e1 + e2)
    o = jnp.zeros(h_ref.shape, F32)
    for gi, (e, o_ref) in enumerate(((e0, o0_ref), (e1, o1_ref), (e2, o2_ref))):
        hi, lo = _split_bf16(e * inv)
        wexp = (jnp.dot(hi, e_ref[...], preferred_element_type=F32)
                + jnp.dot(lo, e_ref[...], preferred_element_type=F32))
        o = o + wexp * _to_token_order(o_ref, onat_ref, A_DILATIONS[gi])
    mix = jnp.dot(o.astype(BF16), w_ref[...], preferred_element_type=F32)
    _layer_norm_store(DEEPNORM_ALPHA * h_ref[...] + mix, g_ref, b_ref, hn_ref, hb_ref)


def _outproj_a(outs, lses, h, w_out, ln_g, ln_b, tm=512):
    seq = h.shape[0]
    row = lambda i: (i, 0)
    fix = lambda i: (0, 0)
    cls = lambda i: (0, i, 0)
    return pl.pallas_call(
        _outproj_a_kernel,
        grid=(seq // tm,),
        in_specs=[pl.BlockSpec((d, tm // d, D_MODEL), cls) for d in A_DILATIONS]
        + [pl.BlockSpec((d, tm // d, LANES), cls) for d in A_DILATIONS] + [
            pl.BlockSpec((tm, D_MODEL), row),
            pl.BlockSpec((D_MODEL, D_MODEL), fix),
            pl.BlockSpec((LANES, D_MODEL), fix),
            pl.BlockSpec((1, D_MODEL), fix),
            pl.BlockSpec((1, D_MODEL), fix)],
        out_specs=[pl.BlockSpec((tm, D_MODEL), row)] * 2,
        out_shape=[jax.ShapeDtypeStruct((seq, D_MODEL), F32), jax.ShapeDtypeStruct((seq, D_MODEL), BF16)],
        scratch_shapes=[pltpu.VMEM((D_MODEL // LANES, tm, LANES), F32), pltpu.VMEM((1, tm, LANES), F32)],
        compiler_params=_params(("arbitrary",)),
        name="outproj_ln_a",
    )(*outs, *lses, h, w_out.astype(BF16), _head_expand_matrix(), ln_g.reshape(1, -1), ln_b.reshape(1, -1))


def _dilated_layer(h, hb, w_in, layer, w_out, ln_g, ln_b):
    seq = h.shape[0]
    outs, lses = [], []
    for g in range(A_N_GROUPS):
        qkv_g = _qkv_group(hb, w_in, layer, g)
        o, lse = _dilated_group(qkv_g, g, seq)
        outs.append(o)
        lses.append(lse)
    return _outproj_a(outs, lses, h, w_out, ln_g, ln_b)


ROUTER_ROWS = 24


def _router_kernel(h_ref, w_ref, b_ref, ids_ref, wts_ref, cnt_ref, base_ref):
    i = pl.program_id(0)
    tm = h_ref.shape[0]

    @pl.when(i == 0)
    def _():
        base_ref[...] = jnp.zeros_like(base_ref)

    lt = lax.dot_general(w_ref[...], h_ref[...], (((1,), (1,)), ((), ())),
                         precision=lax.Precision.HIGHEST, preferred_element_type=F32)
    lt = lt + b_ref[:, 0:1]
    gl = [lt[k:k + 1, :] for k in range(N_EXPERT_GROUPS)]
    best, gidx = gl[0], jnp.zeros((1, tm), I32)
    for k in range(1, N_EXPERT_GROUPS):
        take = gl[k] > best
        best = jnp.where(take, gl[k], best)
        gidx = jnp.where(take, k, gidx)
    denom = gl[0] * 0.0
    for k in range(N_EXPERT_GROUPS):
        denom = denom + jnp.exp(gl[k] - best)
    g_w = 1.0 / denom
    e_in = []
    for e in range(EXPERTS_PER_GROUP):
        v = lt[N_EXPERT_GROUPS + e:N_EXPERT_GROUPS + e + 1, :]
        for k in range(1, N_EXPERT_GROUPS):
            row = N_EXPERT_GROUPS + k * EXPERTS_PER_GROUP + e
            v = jnp.where(gidx == k, lt[row:row + 1, :], v)
        e_in.append(v)
    v1, i1 = e_in[0], jnp.zeros((1, tm), I32)
    for e in range(1, EXPERTS_PER_GROUP):
        take = e_in[e] > v1
        v1 = jnp.where(take, e_in[e], v1)
        i1 = jnp.where(take, e, i1)
    v2, i2 = jnp.full((1, tm), -jnp.inf, F32), jnp.zeros((1, tm), I32)
    for e in range(EXPERTS_PER_GROUP):
        take = (e_in[e] > v2) & (i1 != e)
        v2 = jnp.where(take, e_in[e], v2)
        i2 = jnp.where(take, e, i2)
    t = jnp.exp(v2 - v1)
    w1 = g_w / (1.0 + t)
    w2 = g_w * t / (1.0 + t)
    e1 = gidx * EXPERTS_PER_GROUP + i1
    e2 = gidx * EXPERTS_PER_GROUP + i2

    erow = lax.broadcasted_iota(I32, (N_EXPERTS, tm), 0)
    oh1 = (erow == e1).astype(F32)
    oh2 = (erow == e2).astype(F32)
    c = oh1 + oh2
    tri = (lax.broadcasted_iota(I32, (tm, tm), 0) < lax.broadcasted_iota(I32, (tm, tm), 1)).astype(BF16)
    excl = jnp.dot(c.astype(BF16), tri, preferred_element_type=F32) + base_ref[...]
    r1 = jnp.sum(oh1 * excl, axis=0, keepdims=True).astype(I32)
    r2 = jnp.sum(oh2 * excl, axis=0, keepdims=True).astype(I32)
    base_ref[...] = base_ref[...] + jnp.sum(c, axis=1, keepdims=True)

    zi = jnp.zeros((1, tm), I32)
    ids_ref[...] = jnp.concatenate([e1, e2, r1, r2, zi, zi, zi, zi], axis=0)
    zf = jnp.zeros((1, tm), F32)
    wts_ref[...] = jnp.concatenate([w1, w2, zf, zf, zf, zf, zf, zf], axis=0)
    cnt_ref[...] = jnp.broadcast_to(base_ref[...], cnt_ref.shape)


def _router(h, w_router, b_router, tm=512):
    seq = h.shape[0]
    n_log = N_EXPERT_GROUPS + N_EXPERTS
    w_t = jnp.zeros((ROUTER_ROWS, D_MODEL), F32).at[:n_log].set(w_router.T)
    b_t = jnp.zeros((ROUTER_ROWS, LANES), F32).at[:n_log].set(jnp.broadcast_to(b_router[:, None], (n_log, LANES)))
    return pl.pallas_call(
        _router_kernel,
        grid=(seq // tm,),
        in_specs=[pl.BlockSpec((tm, D_MODEL), lambda i: (i, 0)),
                  pl.BlockSpec((ROUTER_ROWS, D_MODEL), lambda i: (0, 0)),
                  pl.BlockSpec((ROUTER_ROWS, LANES), lambda i: (0, 0))],
        out_specs=[pl.BlockSpec((8, tm), lambda i: (0, i)),
                   pl.BlockSpec((8, tm), lambda i: (0, i)),
                   pl.BlockSpec((N_EXPERTS, LANES), lambda i: (0, 0))],
        out_shape=[jax.ShapeDtypeStruct((8, seq), I32), jax.ShapeDtypeStruct((8, seq), F32),
                   jax.ShapeDtypeStruct((N_EXPERTS, LANES), F32)],
        scratch_shapes=[pltpu.VMEM((N_EXPERTS, 1), F32)],
        compiler_params=_params(("arbitrary",)),
        name="moe_router",
    )(h, w_t, b_t)


ROW_TILE = D_MODEL // LANES


def _rows_to_tiles(x, dst_ref):
    n = x.shape[0]
    for c in range(ROW_TILE):
        dst_ref[pl.ds(c, n, stride=ROW_TILE), :] = x[:, c * LANES:(c + 1) * LANES]


def _tiles_to_rows(src_ref):
    n = src_ref.shape[0] // ROW_TILE
    return jnp.concatenate([src_ref[pl.ds(c, n, stride=ROW_TILE), :] for c in range(ROW_TILE)], axis=1)


def _row_tile(ref, p):
    return ref.at[pl.ds(pl.multiple_of(p * ROW_TILE, ROW_TILE), ROW_TILE)]


def _dispatch_kernel(pos_ref, h_ref, xs_in_ref, xs_ref, ht_ref, sem):
    del xs_in_ref
    step = pl.program_id(0)
    tm = h_ref.shape[0]
    slot = lax.rem(step, 2)

    def row_copy(s, i, p):
        return pltpu.make_async_copy(_row_tile(ht_ref.at[s], i), _row_tile(xs_ref, p), sem.at[s])

    def wait_all(s):
        def wait(i, c):
            row_copy(s, 0, 0).wait()
            row_copy(s, 0, 0).wait()
            return c
        lax.fori_loop(0, tm, wait, 0, unroll=8)

    _rows_to_tiles(h_ref[...], ht_ref.at[slot])

    def start(i, c):
        row_copy(slot, i, pos_ref[0, 0, i]).start(priority=0)
        row_copy(slot, i, pos_ref[0, 0, tm + i]).start(priority=1)
        return c

    lax.fori_loop(0, tm, start, 0, unroll=8)

    @pl.when(step > 0)
    def _():
        wait_all(1 - slot)

    @pl.when(step == pl.num_programs(0) - 1)
    def _():
        wait_all(slot)


def _dispatch(h, pos, buf, tm=256):
    seq = h.shape[0]
    nt = seq // tm
    pos3 = pos.reshape(2, nt, tm).transpose(1, 0, 2).reshape(nt, 1, 2 * tm)
    return pl.pallas_call(
        _dispatch_kernel,
        grid=(nt,),
        in_specs=[pl.BlockSpec((1, 1, 2 * tm), lambda i: (i, 0, 0), memory_space=pltpu.SMEM),
                  pl.BlockSpec((tm, D_MODEL), lambda i: (i, 0)),
                  pl.BlockSpec(memory_space=pl.ANY)],
        out_specs=pl.BlockSpec(memory_space=pl.ANY),
        out_shape=jax.ShapeDtypeStruct(buf.shape, F32),
        scratch_shapes=[pltpu.VMEM((2, tm * ROW_TILE, LANES), F32), pltpu.SemaphoreType.DMA((2,))],
        input_output_aliases={2: 0},
        compiler_params=_params(("arbitrary",)),
        name="moe_dispatch",
    )(pos3, h, buf)


def _expert_kernel(te_ref, x_ref, wgu_ref, wd_ref, y_ref):
    del te_ref
    gu = jnp.dot(_tiles_to_rows(x_ref).astype(BF16), wgu_ref[...], preferred_element_type=F32)
    gate, up = gu[:, :D_EXPERT], gu[:, D_EXPERT:]
    hid = gate * (1.0 / (1.0 + jnp.exp(-gate))) * up
    _rows_to_tiles(jnp.dot(hid.astype(BF16), wd_ref[...], preferred_element_type=F32), y_ref)


def _experts(xs, tile_expert, w_gate_up, w_down, layer):
    blk = MOE_TILE * ROW_TILE
    return pl.pallas_call(
        _expert_kernel,
        grid_spec=pltpu.PrefetchScalarGridSpec(
            num_scalar_prefetch=1,
            grid=(xs.shape[0] // blk,),
            in_specs=[pl.BlockSpec((blk, LANES), lambda i, te: (i, 0)),
                      pl.BlockSpec((None, None, D_MODEL, 2 * D_EXPERT), lambda i, te: (layer, te[i], 0, 0)),
                      pl.BlockSpec((None, None, D_EXPERT, D_MODEL), lambda i, te: (layer, te[i], 0, 0))],
            out_specs=pl.BlockSpec((blk, LANES), lambda i, te: (i, 0))),
        out_shape=jax.ShapeDtypeStruct(xs.shape, F32),
        compiler_params=_params(("arbitrary",)),
        name="moe_experts",
    )(tile_expert, xs, w_gate_up, w_down)


def _combine_kernel(pos_ref, pos_next_ref, ys_ref, h_ref, w_ref, g_ref, b_ref, hn_ref, hb_ref, buf_ref, sem):
    step = pl.program_id(0)
    tm = h_ref.shape[0]
    slot = lax.rem(step, 2)

    def row_copy(s, k, i, p):
        return pltpu.make_async_copy(_row_tile(ys_ref, p), _row_tile(buf_ref.at[s, k], i), sem.at[s])

    def gather(p_ref, s):
        def start(i, c):
            row_copy(s, 0, i, p_ref[0, 0, i]).start(priority=0)
            row_copy(s, 1, i, p_ref[0, 0, tm + i]).start(priority=1)
            return c
        lax.fori_loop(0, tm, start, 0, unroll=8)

    @pl.when(step == 0)
    def _():
        gather(pos_ref, 0)

    @pl.when(step + 1 < pl.num_programs(0))
    def _():
        gather(pos_next_ref, 1 - slot)

    def wait(i, c):
        row_copy(slot, 0, 0, 0).wait()
        row_copy(slot, 1, 0, 0).wait()
        return c

    lax.fori_loop(0, tm, wait, 0, unroll=8)
    w = w_ref[...]
    ffn = (w[:, 0:1] * _tiles_to_rows(buf_ref.at[slot, 0])
           + w[:, 1:2] * _tiles_to_rows(buf_ref.at[slot, 1]))
    _layer_norm_store(DEEPNORM_ALPHA * h_ref[...] + ffn, g_ref, b_ref, hn_ref, hb_ref)


def _combine(ys, pos, wts, h, ln_g, ln_b, tm=256):
    seq = h.shape[0]
    nt = seq // tm
    pos3 = pos.reshape(2, nt, tm).transpose(1, 0, 2).reshape(nt, 1, 2 * tm)
    row = lambda i: (i, 0)
    fix = lambda i: (0, 0)
    return pl.pallas_call(
        _combine_kernel,
        grid=(nt,),
        in_specs=[pl.BlockSpec((1, 1, 2 * tm), lambda i: (i, 0, 0), memory_space=pltpu.SMEM),
                  pl.BlockSpec((1, 1, 2 * tm), lambda i: (jnp.minimum(i + 1, nt - 1), 0, 0),
                               memory_space=pltpu.SMEM),
                  pl.BlockSpec(memory_space=pl.ANY),
                  pl.BlockSpec((tm, D_MODEL), row),
                  pl.BlockSpec((tm, 2), row),
                  pl.BlockSpec((1, D_MODEL), fix),
                  pl.BlockSpec((1, D_MODEL), fix)],
        out_specs=[pl.BlockSpec((tm, D_MODEL), row)] * 2,
        out_shape=[jax.ShapeDtypeStruct((seq, D_MODEL), F32), jax.ShapeDtypeStruct((seq, D_MODEL), BF16)],
        scratch_shapes=[pltpu.VMEM((2, 2, tm * ROW_TILE, LANES), F32), pltpu.SemaphoreType.DMA((2,))],
        compiler_params=_params(("arbitrary",)),
        name="moe_combine_ln",
    )(pos3, pos3, ys, h, wts, ln_g.reshape(1, -1), ln_b.reshape(1, -1))


def _moe_layer(h, w_router, b_router, w_gate_up, w_down, layer, ln_g, ln_b, buf):
    seq = h.shape[0]
    ids, wts, cnt = _router(h, w_router, b_router)
    counts = cnt[:, 0].astype(I32)
    padded = ((counts + MOE_TILE - 1) // MOE_TILE) * MOE_TILE
    ends = jnp.cumsum(padded)
    offs = ends - padded
    eids = jnp.arange(N_EXPERTS, dtype=I32)[:, None, None]
    pos = jnp.sum(jnp.where(ids[None, 0:2] == eids, offs[:, None, None], 0), axis=0) + ids[2:4]
    n_rows = 2 * seq + N_EXPERTS * MOE_TILE
    tile_start = jnp.arange(n_rows // MOE_TILE, dtype=I32) * MOE_TILE
    tile_expert = jnp.minimum(jnp.sum((tile_start[:, None] >= ends[None, :]).astype(I32), axis=1), N_EXPERTS - 1)
    if buf is None:
        buf = jnp.zeros((n_rows * ROW_TILE, LANES), F32)
    xs = _dispatch(h, pos, buf)
    ys = _experts(xs, tile_expert.astype(I32), w_gate_up, w_down, layer)
    return _combine(ys, pos, wts[0:2].T, h, ln_g, ln_b) + (ys,)


B_KV_GROUPS = 4
B_HPG = N_HEADS // B_KV_GROUPS
CMP_LEN = 32
CMP_STRIDE = 16
CMP_HIDDEN = 256
SLC_BLOCK = 64
N_SELECT = 16
WIN = 512
FORCE_SCORE = 1e4
N_FORCED = 3
LIST_W = 272
LOG2E = math.log2(math.e)
MASK_PENALTY = 1e30
M_FLOOR = -1e29
CMP_STEP_BLOCKS = 2
CMP_PARTS = 8
SEL_CHUNK = 4


def _group_slopes(g):
    sl = [v * LOG2E for v in _slopes(N_HEADS)]
    out = []
    for r in range(B_HPG):
        v = jnp.full((1, 1), sl[r], F32)
        for k in range(1, B_KV_GROUPS):
            v = jnp.where(g == k, sl[k * B_HPG + r], v)
        out.append(v)
    return out


def _qproj_kernel(hb_ref, w_ref, b_ref, qt_ref, gt_ref):
    pt = lax.dot_general(w_ref[...], hb_ref[...], (((1,), (1,)), ((), ())), preferred_element_type=F32)
    pt = pt + b_ref[:, 0:1]
    qt_ref[...] = (pt[:D_MODEL] * (HEAD_DIM ** -0.5 * LOG2E)).astype(BF16)
    gt_ref[...] = 1.0 / (1.0 + jnp.exp(-pt[D_MODEL:]))


def _qproj(hb, w_q, b_q, tm=512):
    seq = hb.shape[0]
    n = w_q.shape[1]
    npad = D_MODEL + LANES
    w_t = jnp.zeros((npad, D_MODEL), BF16).at[:n].set(w_q.T.astype(BF16))
    b_t = jnp.zeros((npad, LANES), F32).at[:n].set(jnp.broadcast_to(b_q[:, None], (n, LANES)))
    return pl.pallas_call(
        _qproj_kernel,
        grid=(seq // tm,),
        in_specs=[pl.BlockSpec((tm, D_MODEL), lambda i: (i, 0)),
                  pl.BlockSpec((npad, D_MODEL), lambda i: (0, 0)),
                  pl.BlockSpec((npad, LANES), lambda i: (0, 0))],
        out_specs=[pl.BlockSpec((D_MODEL, tm), lambda i: (0, i)),
                   pl.BlockSpec((LANES, tm), lambda i: (0, i))],
        out_shape=[jax.ShapeDtypeStruct((D_MODEL, seq), BF16), jax.ShapeDtypeStruct((LANES, seq), F32)],
        compiler_params=_params(("arbitrary",)),
        name="nsa_qproj",
    )(hb, w_t, b_t)


def _compress_kernel(x_ref, w1a_ref, w1b_ref, pa_ref, pb_ref, b1_ref, w2_ref, b2_ref, o_ref):
    x = x_ref[0, 0]
    n = x.shape[0]
    a = jnp.dot((x + pa_ref[0]).astype(BF16), w1a_ref[0], preferred_element_type=F32)
    b = jnp.dot((x + pb_ref[0]).astype(BF16), w1b_ref[0], preferred_element_type=F32)
    b = pltpu.roll(b, n - 1, 0)
    u = a + b + b1_ref[0]
    hid = 0.5 * u * (1.0 + jnp.tanh(math.sqrt(2.0 / math.pi) * (u + 0.044715 * (u * u * u))))
    o_ref[0, 0] = (jnp.dot(hid.astype(BF16), w2_ref[0], preferred_element_type=F32) + b2_ref[0]).astype(BF16)


def _compress(kvc, cmp_pos, cmp_w1, cmp_b1, cmp_w2, cmp_b2):
    seq = kvc.shape[0]
    nch = seq // CMP_STRIDE
    half = CMP_STRIDE * HEAD_DIM
    chunks = kvc.reshape(nch, CMP_STRIDE, 2, B_KV_GROUPS, HEAD_DIM).transpose(2, 3, 0, 1, 4).reshape(
        2, B_KV_GROUPS, nch, half)
    pos = cmp_pos.reshape(2, 1, CMP_LEN * HEAD_DIM)
    w1 = cmp_w1.astype(BF16)
    jfix = lambda j, g: (j, 0, 0)
    return pl.pallas_call(
        _compress_kernel,
        grid=(2, B_KV_GROUPS),
        in_specs=[pl.BlockSpec((1, 1, nch, half), lambda j, g: (j, g, 0, 0)),
                  pl.BlockSpec((1, half, CMP_HIDDEN), jfix),
                  pl.BlockSpec((1, half, CMP_HIDDEN), lambda j, g: (j, 1, 0)),
                  pl.BlockSpec((1, 1, half), jfix),
                  pl.BlockSpec((1, 1, half), lambda j, g: (j, 0, 1)),
                  pl.BlockSpec((1, 1, CMP_HIDDEN), jfix),
                  pl.BlockSpec((1, CMP_HIDDEN, HEAD_DIM), jfix),
                  pl.BlockSpec((1, 1, HEAD_DIM), jfix)],
        out_specs=pl.BlockSpec((1, 1, nch, HEAD_DIM), lambda j, g: (j, g, 0, 0)),
        out_shape=jax.ShapeDtypeStruct((2, B_KV_GROUPS, nch, HEAD_DIM), BF16),
        compiler_params=_params(("arbitrary", "arbitrary")),
        name="nsa_compress",
    )(chunks, w1, w1, pos, pos, cmp_b1.reshape(2, 1, CMP_HIDDEN), cmp_w2.astype(BF16),
      cmp_b2.reshape(2, 1, HEAD_DIM))


def _nsa_shared_kv(hb, w_kv, cmp_pos, cmp_w1, cmp_b1, cmp_w2, cmp_b2):
    seq = hb.shape[0]
    gw = B_KV_GROUPS * HEAD_DIM
    w = w_kv.astype(BF16)
    kvc = _matmul(hb, w[:, :2 * gw], F32, 512, 2 * gw, "kv_proj_cmp")
    kvr = _matmul(hb, w[:, 2 * gw:], BF16, 512, 4 * gw, "kv_proj_rest")
    cmp = _compress(kvc, cmp_pos, cmp_w1, cmp_b1, cmp_w2, cmp_b2)
    n_slc, n_win = seq // SLC_BLOCK, seq // Q_BLOCK

    def blocks(cols, nblk, blk, transpose):
        t = cols.reshape(nblk, blk, B_KV_GROUPS, HEAD_DIM)
        return t.transpose(2, 0, 3, 1) if transpose else t.transpose(2, 0, 1, 3)

    return dict(
        kc=cmp[0], vct=cmp[1].transpose(0, 2, 1),
        ks=blocks(kvr[:, 0 * gw:1 * gw], n_slc, SLC_BLOCK, False),
        vst=blocks(kvr[:, 1 * gw:2 * gw], n_slc, SLC_BLOCK, False),
        kw=blocks(kvr[:, 2 * gw:3 * gw], n_win, Q_BLOCK, False),
        vwt=blocks(kvr[:, 3 * gw:4 * gw], n_win, Q_BLOCK, True))


def _select_matrix(n_slc, nch):
    m = np.zeros((n_slc, nch), np.float32)
    r_s, r_c = SLC_BLOCK // CMP_STRIDE, CMP_LEN // CMP_STRIDE
    for a in range(r_s):
        for b in range(r_c):
            idx = r_s * np.arange(n_slc) + a - b
            ok = (idx >= 0) & (idx < nch - r_c + 1)
            m[np.arange(n_slc)[ok], idx[ok]] += 1.0
    return jnp.asarray(m, BF16)


def _q_heads(qt_ref, lanes=slice(None)):
    return jnp.concatenate([qt_ref[r * HEAD_DIM:(r + 1) * HEAD_DIM, lanes] for r in range(B_HPG)], axis=1)


def _nsa_cmp_kernel(qt_ref, kc_ref, vct_ref, gt_ref, mt_ref, oc_ref, sel_ref, flag_ref, bias_ref):
    g = pl.program_id(0)
    i = pl.program_id(1)
    nch = kc_ref.shape[1]
    n_slc = mt_ref.shape[0]
    first_blk = i * CMP_STEP_BLOCKS

    @pl.when(i == 0)
    def _():
        slopes = _group_slopes(g)
        rel = (lax.broadcasted_iota(I32, (nch, Q_BLOCK), 1)
               - (CMP_STRIDE * lax.broadcasted_iota(I32, (nch, Q_BLOCK), 0) + (CMP_LEN - 1))).astype(F32)
        for r in range(B_HPG):
            bias_ref[r] = slopes[r] * rel

    def compressed_branch(n):
        for sub in range(CMP_STEP_BLOCKS):
            one_block(n, sub)

    def one_block(n, sub):
        t0 = (first_blk + sub) * Q_BLOCK
        lanes = slice(sub * Q_BLOCK, (sub + 1) * Q_BLOCK)
        cidx = lax.broadcasted_iota(I32, (n, Q_BLOCK), 0)
        qidx = lax.broadcasted_iota(I32, (n, Q_BLOCK), 1)
        valid = (t0 + qidx) - (CMP_STRIDE * cidx + (CMP_LEN - 1)) >= 0
        st = jnp.dot(kc_ref[0, :n, :], _q_heads(qt_ref, lanes), preferred_element_type=F32)
        imp = jnp.zeros((n, Q_BLOCK), F32)
        outs = []
        for r in range(B_HPG):
            s = st[:, r * Q_BLOCK:(r + 1) * Q_BLOCK] - bias_ref[r, :n, :]
            s = jnp.where(valid, s, NEG_INF)
            m = jnp.maximum(jnp.max(s, axis=0, keepdims=True), M_FLOOR)
            p = jnp.exp2(s - m)
            l = jnp.sum(p, axis=0, keepdims=True)
            pn = p * (1.0 / jnp.maximum(l, TINY))
            imp = imp + pn
            o = jnp.dot(vct_ref[0, :, :n], pn.astype(BF16), preferred_element_type=F32)
            outs.append(o * gt_ref[pl.ds(g * B_HPG + r, 1), :][:, lanes])
        oc_ref[lanes, :] = jnp.concatenate(outs, axis=0).T
        hi = imp.astype(BF16)
        r1 = imp - hi.astype(F32)
        mid = r1.astype(BF16)
        lo = (r1 - mid.astype(F32)).astype(BF16)
        ns = n * n_slc // nch
        mt = mt_ref[:ns, :n]
        p_slc = (jnp.dot(mt, hi, preferred_element_type=F32) + jnp.dot(mt, mid, preferred_element_type=F32)
                 + jnp.dot(mt, lo, preferred_element_type=F32))
        sid = lax.broadcasted_iota(I32, (ns, Q_BLOCK), 0)
        cur = (t0 + lax.broadcasted_iota(I32, (ns, Q_BLOCK), 1)) // SLC_BLOCK
        forced = (sid == 0) | (sid == cur) | (sid == cur - 1)
        score = jnp.where(forced, -jnp.inf, p_slc)
        score = jnp.where(sid <= cur, score, -FORCE_SCORE)
        for _ in range(N_SELECT - N_FORCED):
            mx = jnp.max(score, axis=0, keepdims=True)
            first = jnp.min(jnp.where(score == mx, sid, ns), axis=0, keepdims=True)
            score = jnp.where(sid == first, -jnp.inf, score)
        sel = jnp.where((score == -jnp.inf) & (sid <= cur), 1.0, 0.0)
        sel_ref[0, sub, :ns, :] = sel
        flag_ref[0, sub, :ns, :] = jnp.max(sel, axis=1, keepdims=True).astype(I32)
        if ns < n_slc:
            sel_ref[0, sub, ns:, :] = jnp.zeros((n_slc - ns, Q_BLOCK), F32)
            flag_ref[0, sub, ns:, :] = jnp.zeros((n_slc - ns, 1), I32)

    n_parts = min(CMP_PARTS, n_slc // N_SELECT)
    part = nch // n_parts
    lax.switch(lax.div(8 * (first_blk + CMP_STEP_BLOCKS - 1) + 6, part),
               [functools.partial(compressed_branch, part * (k + 1)) for k in range(n_parts)])


def _nsa_cmp(qt, gt, kv, seq):
    nb = seq // Q_BLOCK
    nch, n_slc = seq // CMP_STRIDE, seq // SLC_BLOCK
    gw = B_HPG * HEAD_DIM
    nq = CMP_STEP_BLOCKS * Q_BLOCK
    return pl.pallas_call(
        _nsa_cmp_kernel,
        grid=(B_KV_GROUPS, nb // CMP_STEP_BLOCKS),
        in_specs=[pl.BlockSpec((gw, nq), lambda g, i: (g, i)),
                  pl.BlockSpec((1, nch, HEAD_DIM), lambda g, i: (g, 0, 0)),
                  pl.BlockSpec((1, HEAD_DIM, nch), lambda g, i: (g, 0, 0)),
                  pl.BlockSpec((LANES, nq), lambda g, i: (0, i)),
                  pl.BlockSpec((n_slc, nch), lambda g, i: (0, 0))],
        out_specs=[pl.BlockSpec((nq, gw), lambda g, i: (i, g)),
                   pl.BlockSpec((1, CMP_STEP_BLOCKS, n_slc, Q_BLOCK), lambda g, i: (g, i, 0, 0)),
                   pl.BlockSpec((1, CMP_STEP_BLOCKS, n_slc, 1), lambda g, i: (g, i, 0, 0))],
        out_shape=[jax.ShapeDtypeStruct((seq, D_MODEL), F32),
                   jax.ShapeDtypeStruct((B_KV_GROUPS, nb, n_slc, Q_BLOCK), F32),
                   jax.ShapeDtypeStruct((B_KV_GROUPS, nb, n_slc, 1), I32)],
        scratch_shapes=[pltpu.VMEM((B_HPG, nch, Q_BLOCK), F32)],
        compiler_params=_params(("arbitrary", "arbitrary")),
        name="nsa_compressed_select",
    )(qt, kv["kc"], kv["vct"], gt, _select_matrix(n_slc, nch))


def _fold8(x, op):
    return functools.reduce(op, [x[8 * k:8 * k + 8] for k in range(x.shape[0] // 8)])


def _softmax_tiles(sts, tiles, rel_bias, slopes, m_old):
    ps = [[] for _ in tiles]
    ms, ls, alphas = [], [], []
    for r in range(B_HPG):
        cols = slice(r * Q_BLOCK, (r + 1) * Q_BLOCK)
        us, shifts, tops = [], [], []
        for st, (mask, penalty, off) in zip(sts, tiles):
            u = st[:, cols] - rel_bias[r]
            u = u if mask is None else jnp.where(mask, u, NEG_INF)
            shift = slopes[r] * jnp.asarray(off, F32)
            shift = shift if penalty is None else shift + penalty
            us.append(u)
            shifts.append(shift)
            tops.append(_fold8(u, jnp.maximum) - shift)
        m_new = jnp.max(functools.reduce(jnp.maximum, tops), axis=0, keepdims=True)
        m_new = jnp.maximum(m_new, M_FLOOR)
        if m_old is not None:
            m_new = jnp.maximum(m_old[:, cols], m_new)
            alphas.append(jnp.exp2(m_old[:, cols] - m_new))
        l8 = jnp.zeros((8, Q_BLOCK), F32)
        for t, (u, shift) in enumerate(zip(us, shifts)):
            p = jnp.exp2(u - (m_new + shift))
            l8 = l8 + _fold8(p, jnp.add)
            ps[t].append(p.astype(BF16))
        ms.append(m_new)
        ls.append(jnp.sum(l8, axis=0, keepdims=True))
    ps = [jnp.concatenate(p, axis=1) for p in ps]
    alpha = None if m_old is None else jnp.concatenate(alphas, axis=1)
    return ps, jnp.concatenate(ms, axis=1), jnp.concatenate(ls, axis=1), alpha


def _nsa_sparse_kernel(lst_ref, qt_ref, ks_ref, vst_ref, kw_ref, vwt_ref, sel_ref, gt_ref, oc_ref, o_ref,
                       m_ref, alpha_ref, l_ref, accw_ref, acc_ref, st_ref, p_ref):
    g = pl.program_id(0)
    i = pl.program_id(1)
    t0 = i * Q_BLOCK
    slopes = _group_slopes(g)
    q4 = _q_heads(qt_ref)

    def gate(branch):
        return jnp.concatenate(
            [gt_ref[pl.ds(branch * N_HEADS + g * B_HPG + r, 1), :] for r in range(B_HPG)], axis=1)

    rel_w = (lax.broadcasted_iota(I32, (Q_BLOCK, Q_BLOCK), 1)
             - lax.broadcasted_iota(I32, (Q_BLOCK, Q_BLOCK), 0))
    n_prev = WIN // Q_BLOCK
    rel_w_f = rel_w.astype(F32)
    bias_w = [slopes[r] * rel_w_f for r in range(B_HPG)]
    tiles, blks = [], []
    for c in range(n_prev + 1):
        blk = i - n_prev + c
        off = (n_prev - c) * Q_BLOCK
        mask = (rel_w >= 0) if c == n_prev else (rel_w < 0) if c == 0 else None
        penalty = None if c == n_prev else jnp.where(blk >= 0, 0.0, MASK_PENALTY).astype(F32)
        tiles.append((mask, penalty, off))
        blks.append(jnp.maximum(blk, 0))
    sts = [jnp.dot(kw_ref[0, blk], q4, preferred_element_type=F32) for blk in blks]
    ps, _, l_w, _ = _softmax_tiles(sts, tiles, bias_w, slopes, None)
    acc_w = functools.reduce(lambda a, b: a + b, [
        jnp.dot(vwt_ref[0, blk], p, preferred_element_type=F32) for blk, p in zip(blks, ps)])
    accw_ref[...] = acc_w * (gate(2) / jnp.maximum(l_w, TINY))

    rel_s = (lax.broadcasted_iota(I32, (SLC_BLOCK, Q_BLOCK), 1)
             - lax.broadcasted_iota(I32, (SLC_BLOCK, Q_BLOCK), 0))
    rel_s_f = rel_s.astype(F32)
    bias_s = [slopes[r] * rel_s_f for r in range(B_HPG)]

    def block_id(j, b):
        return lst_ref[0, 0, j * SEL_CHUNK + b]

    def rows(b):
        return slice(b * SLC_BLOCK, (b + 1) * SLC_BLOCK)

    def scores(j, slot):
        for b in range(SEL_CHUNK):
            st_ref[slot, rows(b), :] = jnp.dot(ks_ref[0, block_id(j, b)], q4, preferred_element_type=F32)

    def weighted_values(j):
        v = jnp.concatenate([vst_ref[0, block_id(j, b)] for b in range(SEL_CHUNK)], axis=0)
        return lax.dot_general(v, p_ref[...], (((0,), (0,)), ((), ())), preferred_element_type=F32)

    m_ref[...] = jnp.full(m_ref.shape, M_FLOOR, F32)
    alpha_ref[...] = jnp.ones(alpha_ref.shape, F32)
    l_ref[...] = jnp.zeros(l_ref.shape, F32)
    acc_ref[...] = jnp.zeros(acc_ref.shape, F32)
    p_ref[...] = jnp.zeros(p_ref.shape, BF16)
    scores(0, 0)

    def sel_chunk(j, slot):
        acc_ref[...] = acc_ref[...] * alpha_ref[0:1, :] + weighted_values(jnp.maximum(j - 1, 0))
        tiles, sts = [], []
        for b in range(SEL_CHUNK):
            blk = block_id(j, b)
            off = t0 - blk * SLC_BLOCK
            penalty = jnp.where(sel_ref[0, 0, pl.ds(blk, 1), :] > 0.5, 0.0, MASK_PENALTY)
            mask = (rel_s >= -off) if b < 2 else None
            tiles.append((mask, penalty, off))
            sts.append(st_ref[slot, rows(b), :])
        ps, m, l, alpha = _softmax_tiles(sts, tiles, bias_s, slopes, m_ref[0:1, :])
        for b in range(SEL_CHUNK):
            p_ref[rows(b), :] = ps[b]
        m_ref[0:1, :] = m
        l_ref[0:1, :] = alpha * l_ref[0:1, :] + l
        alpha_ref[0:1, :] = alpha
        scores(j + 1, 1 - slot)

    def sel_pair(jj, c):
        sel_chunk(2 * jj, 0)
        sel_chunk(2 * jj + 1, 1)
        return c

    n_pairs = lst_ref[0, 0, LIST_W - 1]
    lax.fori_loop(0, n_pairs, sel_pair, 0)
    acc_s = acc_ref[...] * alpha_ref[0:1, :] + weighted_values(jnp.maximum(2 * n_pairs - 1, 0))

    total = accw_ref[...] + acc_s * (gate(1) / jnp.maximum(l_ref[0:1, :], TINY))
    ot = jnp.concatenate([total[:, r * Q_BLOCK:(r + 1) * Q_BLOCK] for r in range(B_HPG)], axis=0)
    o_ref[...] = (ot.T + oc_ref[...]).astype(BF16)


def _nsa_sparse(lists, qt, gt, kv, sel, oc, seq):
    nb = seq // Q_BLOCK
    n_slc, n_win = seq // SLC_BLOCK, seq // Q_BLOCK
    gw = B_HPG * HEAD_DIM
    grp = lambda g, i: (g, 0, 0, 0)
    return pl.pallas_call(
        _nsa_sparse_kernel,
        grid=(B_KV_GROUPS, nb),
        in_specs=[pl.BlockSpec((1, 1, LIST_W), lambda g, i: (g * nb + i, 0, 0), memory_space=pltpu.SMEM),
                  pl.BlockSpec((gw, Q_BLOCK), lambda g, i: (g, i)),
                  pl.BlockSpec((1, n_slc, SLC_BLOCK, HEAD_DIM), grp),
                  pl.BlockSpec((1, n_slc, SLC_BLOCK, HEAD_DIM), grp),
                  pl.BlockSpec((1, n_win, Q_BLOCK, HEAD_DIM), grp),
                  pl.BlockSpec((1, n_win, HEAD_DIM, Q_BLOCK), grp),
                  pl.BlockSpec((1, 1, n_slc, Q_BLOCK), lambda g, i: (g, i, 0, 0)),
                  pl.BlockSpec((LANES, Q_BLOCK), lambda g, i: (0, i)),
                  pl.BlockSpec((Q_BLOCK, gw), lambda g, i: (i, g))],
        out_specs=pl.BlockSpec((Q_BLOCK, gw), lambda g, i: (i, g)),
        scratch_shapes=[pltpu.VMEM((8, B_HPG * Q_BLOCK), F32),
                        pltpu.VMEM((8, B_HPG * Q_BLOCK), F32),
                        pltpu.VMEM((8, B_HPG * Q_BLOCK), F32),
                        pltpu.VMEM((HEAD_DIM, B_HPG * Q_BLOCK), F32),
                        pltpu.VMEM((HEAD_DIM, B_HPG * Q_BLOCK), F32),
                        pltpu.VMEM((2, SEL_CHUNK * SLC_BLOCK, B_HPG * Q_BLOCK), F32),
                        pltpu.VMEM((SEL_CHUNK * SLC_BLOCK, B_HPG * Q_BLOCK), BF16)],
        out_shape=jax.ShapeDtypeStruct((seq, D_MODEL), BF16),
        compiler_params=_params(("arbitrary", "arbitrary")),
        name="nsa_selected_window",
    )(lists, qt, kv["ks"], kv["vst"], kv["kw"], kv["vwt"], sel, gt, oc)


def _outproj_b_kernel(o_ref, h_ref, w_ref, g_ref, b_ref, hn_ref, hb_ref):
    mix = jnp.dot(o_ref[...], w_ref[...], preferred_element_type=F32)
    _layer_norm_store(DEEPNORM_ALPHA * h_ref[...] + mix, g_ref, b_ref, hn_ref, hb_ref)


def _outproj_b(o, h, w_out, ln_g, ln_b, tm=512):
    seq = h.shape[0]
    row = lambda i: (i, 0)
    fix = lambda i: (0, 0)
    return pl.pallas_call(
        _outproj_b_kernel,
        grid=(seq // tm,),
        in_specs=[pl.BlockSpec((tm, D_MODEL), row), pl.BlockSpec((tm, D_MODEL), row),
                  pl.BlockSpec((D_MODEL, D_MODEL), fix), pl.BlockSpec((1, D_MODEL), fix),
                  pl.BlockSpec((1, D_MODEL), fix)],
        out_specs=[pl.BlockSpec((tm, D_MODEL), row)] * 2,
        out_shape=[jax.ShapeDtypeStruct((seq, D_MODEL), F32), jax.ShapeDtypeStruct((seq, D_MODEL), BF16)],
        compiler_params=_params(("arbitrary",)),
        name="outproj_ln_b",
    )(o, h, w_out.astype(BF16), ln_g.reshape(1, -1), ln_b.reshape(1, -1))


def _nsa_layer(h, hb, kv, w_q, b_q, w_out, ln_g, ln_b):
    seq = h.shape[0]
    nb, n_slc = seq // Q_BLOCK, seq // SLC_BLOCK
    qt, gt = _qproj(hb, w_q, b_q)
    oc, sel, flags = _nsa_cmp(qt, gt, kv, seq)
    f = flags.reshape(B_KV_GROUPS, nb, n_slc)
    newest = 2 * jnp.arange(nb, dtype=I32)[None, :, None] + 1
    age = jnp.mod(newest - jnp.arange(n_slc, dtype=I32)[None, None, :], n_slc)
    order = jnp.argsort((1 - f) * n_slc + age, axis=2).astype(I32).reshape(B_KV_GROUPS * nb, n_slc)
    n_pairs = (jnp.sum(f, axis=2).reshape(-1) + 2 * SEL_CHUNK - 1) // (2 * SEL_CHUNK)
    lists = jnp.concatenate(
        [order, jnp.zeros((B_KV_GROUPS * nb, LIST_W - n_slc - 1), I32), n_pairs.astype(I32)[:, None]], axis=1)
    o = _nsa_sparse(lists.reshape(B_KV_GROUPS * nb, 1, LIST_W), qt, gt, kv, sel, oc, seq)
    return _outproj_b(o, h, w_out, ln_g, ln_b)


def kernel(x, a_w_in, a_w_out, b_w_kv, b_cmp_pos, b_cmp_w1, b_cmp_b1, b_cmp_w2, b_cmp_b2, b_w_q, b_b_q, b_w_out, moe_w_router, moe_b_router, moe_w_gate_up, moe_w_down, ln_mix_g, ln_mix_b, ln_ffn_g, ln_ffn_b):
    seq = x.shape[1]
    h = x.reshape(seq, D_MODEL)
    hb = h.astype(BF16)
    a_w_in_b, w_gate_up_b, w_down_b = a_w_in.astype(BF16), moe_w_gate_up.astype(BF16), moe_w_down.astype(BF16)
    buf = None
    for layer in range(DEPTH):
        if layer < N_A_LAYERS:
            h, hb = _dilated_layer(h, hb, a_w_in_b, layer, a_w_out[layer], ln_mix_g[layer], ln_mix_b[layer])
        else:
            if layer == N_A_LAYERS:
                kv = _nsa_shared_kv(hb, b_w_kv, b_cmp_pos, b_cmp_w1, b_cmp_b1, b_cmp_w2, b_cmp_b2)
            j = layer - N_A_LAYERS
            h, hb = _nsa_layer(h, hb, kv, b_w_q[j], b_b_q[j], b_w_out[j], ln_mix_g[layer], ln_mix_b[layer])
        h, hb, buf = _moe_layer(h, moe_w_router[layer], moe_b_router[layer], w_gate_up_b, w_down_b, layer,
                                ln_ffn_g[layer], ln_ffn_b[layer], buf)
    return h.reshape(1, seq, D_MODEL)
```
